```python
import jax, jax.numpy as jnp
from jax import lax
import numpy as np

D_MODEL = 1024
BATCH = 16
SEQ = 4096
DEPTH = 4

N_HEADS = 8
QK_NOPE_DIM = 64
QK_ROPE_DIM = 32
QK_DIM = QK_NOPE_DIM + QK_ROPE_DIM
V_HEAD_DIM = 64
Q_LORA_RANK = 384
KV_LORA_RANK = 256
ROPE_THETA = 10000.0
Q_BLOCK = 128
CONV_CHANNELS = 512
CONV_WIDTH = 31
FFN_HIDDEN = -(-8 * D_MODEL // (3 * 256)) * 256
N_MOD = 6
EPS = 1e-6
NEG_INF = -1e30

IN_WIDTHS = (Q_LORA_RANK, KV_LORA_RANK, QK_ROPE_DIM, 2 * CONV_CHANNELS, D_MODEL, D_MODEL)
SPLIT_IDX = (Q_LORA_RANK,
             Q_LORA_RANK + KV_LORA_RANK,
             Q_LORA_RANK + KV_LORA_RANK + QK_ROPE_DIM,
             Q_LORA_RANK + KV_LORA_RANK + QK_ROPE_DIM + 2 * CONV_CHANNELS,
             Q_LORA_RANK + KV_LORA_RANK + QK_ROPE_DIM + 2 * CONV_CHANNELS + D_MODEL)
IN_WIDTH = sum(IN_WIDTHS)

kernel_name = "hybrid_mla_conformer_adaln_block"


def rms_norm(x, gain):
    xf = x.astype(jnp.float32)
    y = xf * lax.rsqrt(jnp.mean(xf * xf, axis=-1, keepdims=True) + EPS)
    return (y * gain.astype(jnp.float32)).astype(x.dtype)


def layer_norm(x, gain, bias):
    xf = x.astype(jnp.float32)
    mu = jnp.mean(xf, axis=-1, keepdims=True)
    xc = xf - mu
    y = xc * lax.rsqrt(jnp.mean(xc * xc, axis=-1, keepdims=True) + EPS)
    return (y * gain.astype(jnp.float32) + bias.astype(jnp.float32)).astype(x.dtype)


def rope_tables(positions, dtype):
    inv_freq = ROPE_THETA ** (-jnp.arange(0, QK_ROPE_DIM, 2, dtype=jnp.float32) / QK_ROPE_DIM)
    ang = positions.astype(jnp.float32)[..., None] * inv_freq
    return jnp.cos(ang).astype(dtype), jnp.sin(ang).astype(dtype)


def apply_rope(t, cos, sin):
    t1, t2 = jnp.split(t, 2, axis=-1)
    return jnp.concatenate([t1 * cos - t2 * sin, t2 * cos + t1 * sin], axis=-1)


def mla_attention(q_nope, q_rope, k_nope, k_rope, v):
    seq = q_nope.shape[1]
    scale = QK_DIM ** -0.5
    outs = []
    for i in range(seq // Q_BLOCK):
        s0, s1 = i * Q_BLOCK, (i + 1) * Q_BLOCK
        logits = (jnp.einsum('bqhd,bkhd->bhqk', q_nope[:, s0:s1], k_nope[:, :s1])
                  + jnp.einsum('bqhr,bkr->bhqk', q_rope[:, s0:s1], k_rope[:, :s1]))
        logits = logits.astype(jnp.float32) * scale
        causal = (s0 + jnp.arange(Q_BLOCK))[:, None] >= jnp.arange(s1)[None, :]
        probs = jax.nn.softmax(jnp.where(causal, logits, NEG_INF), axis=-1).astype(v.dtype)
        outs.append(jnp.einsum('bhqk,bkhd->bqhd', probs, v[:, :s1]))
    return jnp.concatenate(outs, axis=1)


def causal_depthwise_conv(u, w, b):
    rhs = w.astype(u.dtype)[:, None, :]
    y = lax.conv_general_dilated(u, rhs, window_strides=(1,), padding=[(CONV_WIDTH - 1, 0)],
                                 dimension_numbers=('NWC', 'WIO', 'NWC'),
                                 feature_group_count=u.shape[-1])
    return y + b


def hybrid_layer(x, c_act, cos, sin, w_ada, b_ada, g_mix, w_in, g_q, w_uq, g_kv, w_ukv, w_o_attn,
                 w_dw, b_dw, g_cn, b_cn, w_pw2, w_out, g_ffn, w_gu, w_down):
    bsz, seq, _ = x.shape
    mod = (c_act @ w_ada + b_ada)[:, None, :]
    sh1, sc1, gt1, sh2, sc2, gt2 = jnp.split(mod, N_MOD, axis=-1)

    h = rms_norm(x, g_mix) * (1 + sc1) + sh1
    proj = h @ w_in
    q_lat, kv_lat, k_rope, glu_in, gate_a, gate_b = jnp.split(proj, SPLIT_IDX, axis=-1)

    q = (rms_norm(q_lat, g_q) @ w_uq).reshape(bsz, seq, N_HEADS, QK_DIM)
    q_nope = q[..., :QK_NOPE_DIM]
    q_rope = apply_rope(q[..., QK_NOPE_DIM:], cos[:, :, None, :], sin[:, :, None, :])
    k_rope = apply_rope(k_rope, cos, sin)
    kv = (rms_norm(kv_lat, g_kv) @ w_ukv).reshape(bsz, seq, N_HEADS, QK_NOPE_DIM + V_HEAD_DIM)
    k_nope, v = kv[..., :QK_NOPE_DIM], kv[..., QK_NOPE_DIM:]
    attn = mla_attention(q_nope, q_rope, k_nope, k_rope, v).reshape(bsz, seq, N_HEADS * V_HEAD_DIM)
    y_attn = attn @ w_o_attn

    glu_a, glu_b = jnp.split(glu_in, 2, axis=-1)
    u = causal_depthwise_conv(glu_a * jax.nn.sigmoid(glu_b), w_dw, b_dw)
    y_conv = jax.nn.silu(layer_norm(u, g_cn, b_cn)) @ w_pw2

    y = jax.nn.sigmoid(gate_a) * y_attn + jax.nn.sigmoid(gate_b) * y_conv
    x = x + gt1 * (y @ w_out)

    h = rms_norm(x, g_ffn) * (1 + sc2) + sh2
    g, up = jnp.split(h @ w_gu, 2, axis=-1)
    x = x + gt2 * ((jax.nn.silu(g) * up) @ w_down)
    return x


def _fwd_setup_inputs(seed: int = 0) -> dict:
    key = jax.random.key(seed)
    ks = jax.random.split(key, 24)
    f32 = jnp.float32
    L, D = DEPTH, D_MODEL

    def w(k, shape, fan_in, mult=1.0):
        return jax.random.normal(k, shape, f32) * (mult * fan_in ** -0.5)

    def gain(k, shape):
        return 1.0 + 0.02 * jax.random.normal(k, shape, f32)

    def bias(k, shape):
        return 0.01 * jax.random.normal(k, shape, f32)

    x = jax.random.normal(ks[0], (BATCH, SEQ, D), f32)
    c = jax.random.normal(ks[1], (BATCH, D), f32)
    offset = jax.random.randint(ks[2], (BATCH, 1), 0, 1024, dtype=jnp.int32)
    positions = offset + jnp.arange(SEQ, dtype=jnp.int32)[None, :]
    return {
        "x": x,
        "c": c,
        "positions": positions,
        "w_ada": w(ks[3], (L, D, N_MOD * D), D, 0.5),
        "b_ada": bias(ks[4], (L, N_MOD * D)),
        "g_mix": gain(ks[5], (L, D)),
        "w_in": w(ks[6], (L, D, IN_WIDTH), D),
        "g_q": gain(ks[7], (L, Q_LORA_RANK)),
        "w_uq": w(ks[8], (L, Q_LORA_RANK, N_HEADS * QK_DIM), Q_LORA_RANK),
        "g_kv": gain(ks[9], (L, KV_LORA_RANK)),
        "w_ukv": w(ks[10], (L, KV_LORA_RANK, N_HEADS * (QK_NOPE_DIM + V_HEAD_DIM)), KV_LORA_RANK),
        "w_o_attn": w(ks[11], (L, N_HEADS * V_HEAD_DIM, D), N_HEADS * V_HEAD_DIM),
        "w_dw": w(ks[12], (L, CONV_WIDTH, CONV_CHANNELS), CONV_WIDTH),
        "b_dw": bias(ks[13], (L, CONV_CHANNELS)),
        "g_cn": gain(ks[14], (L, CONV_CHANNELS)),
        "b_cn": bias(ks[15], (L, CONV_CHANNELS)),
        "w_pw2": w(ks[16], (L, CONV_CHANNELS, D), CONV_CHANNELS),
        "w_out": w(ks[17], (L, D, D), D),
        "g_ffn": gain(ks[18], (L, D)),
        "w_gu": w(ks[19], (L, D, 2 * FFN_HIDDEN), D),
        "w_down": w(ks[20], (L, FFN_HIDDEN, D), FFN_HIDDEN),
        "g_final": gain(ks[21], (D,)),
    }


def _fwd_reference(x, c, positions, w_ada, b_ada, g_mix, w_in, g_q, w_uq, g_kv, w_ukv, w_o_attn,
              w_dw, b_dw, g_cn, b_cn, w_pw2, w_out, g_ffn, w_gu, w_down, g_final):
    cos, sin = rope_tables(positions, x.dtype)
    c_act = jax.nn.silu(c)
    for l in range(DEPTH):
        x = hybrid_layer(x, c_act, cos, sin, w_ada[l], b_ada[l], g_mix[l], w_in[l], g_q[l], w_uq[l],
                         g_kv[l], w_ukv[l], w_o_attn[l], w_dw[l], b_dw[l], g_cn[l], b_cn[l],
                         w_pw2[l], w_out[l], g_ffn[l], w_gu[l], w_down[l])
    return rms_norm(x, g_final)


import jax as _jax
import jax.numpy as _jnp

TWIN_FORMAT = 'train_step'
FWD_PARAMS = ['x', 'c', 'positions', 'w_ada', 'b_ada', 'g_mix', 'w_in', 'g_q', 'w_uq', 'g_kv', 'w_ukv', 'w_o_attn', 'w_dw', 'b_dw', 'g_cn', 'b_cn', 'w_pw2', 'w_out', 'g_ffn', 'w_gu', 'w_down', 'g_final']
TWIN_WEIGHTS = ['w_ada', 'b_ada', 'g_mix', 'w_in', 'g_q', 'w_uq', 'g_kv', 'w_ukv', 'w_o_attn', 'w_dw', 'b_dw', 'g_cn', 'b_cn', 'w_pw2', 'w_out', 'g_ffn', 'w_gu', 'w_down', 'g_final']
TWIN_DIFF_INPUT = 'x'
TWIN_INPUTS = ['x', 'c', 'positions', 'w_ada', 'b_ada', 'g_mix', 'w_in', 'g_q', 'w_uq', 'g_kv', 'w_ukv', 'w_o_attn', 'w_dw', 'b_dw', 'g_cn', 'b_cn', 'w_pw2', 'w_out', 'g_ffn', 'w_gu', 'w_down', 'g_final', 'loss_target', 'm_w_ada', 'm_b_ada', 'm_g_mix', 'm_w_in', 'm_g_q', 'm_w_uq', 'm_g_kv', 'm_w_ukv', 'm_w_o_attn', 'm_w_dw', 'm_b_dw', 'm_g_cn', 'm_b_cn', 'm_w_pw2', 'm_w_out', 'm_g_ffn', 'm_w_gu', 'm_w_down', 'm_g_final', 'v_w_ada', 'v_b_ada', 'v_g_mix', 'v_w_in', 'v_g_q', 'v_w_uq', 'v_g_kv', 'v_w_ukv', 'v_w_o_attn', 'v_w_dw', 'v_b_dw', 'v_g_cn', 'v_b_cn', 'v_w_pw2', 'v_w_out', 'v_g_ffn', 'v_w_gu', 'v_w_down', 'v_g_final']
TWIN_OUTPUTS = ['loss', 'grad_x', 'grad_w_ada', 'grad_b_ada', 'grad_g_mix', 'grad_w_in', 'grad_g_q', 'grad_w_uq', 'grad_g_kv', 'grad_w_ukv', 'grad_w_o_attn', 'grad_w_dw', 'grad_b_dw', 'grad_g_cn', 'grad_b_cn', 'grad_w_pw2', 'grad_w_out', 'grad_g_ffn', 'grad_w_gu', 'grad_w_down', 'grad_g_final', 'delta_w_ada', 'delta_b_ada', 'delta_g_mix', 'delta_w_in', 'delta_g_q', 'delta_w_uq', 'delta_g_kv', 'delta_w_ukv', 'delta_w_o_attn', 'delta_w_dw', 'delta_b_dw', 'delta_g_cn', 'delta_b_cn', 'delta_w_pw2', 'delta_w_out', 'delta_g_ffn', 'delta_w_gu', 'delta_w_down', 'delta_g_final', 'new_m_w_ada', 'new_m_b_ada', 'new_m_g_mix', 'new_m_w_in', 'new_m_g_q', 'new_m_w_uq', 'new_m_g_kv', 'new_m_w_ukv', 'new_m_w_o_attn', 'new_m_w_dw', 'new_m_b_dw', 'new_m_g_cn', 'new_m_b_cn', 'new_m_w_pw2', 'new_m_w_out', 'new_m_g_ffn', 'new_m_w_gu', 'new_m_w_down', 'new_m_g_final', 'new_v_w_ada', 'new_v_b_ada', 'new_v_g_mix', 'new_v_w_in', 'new_v_g_q', 'new_v_w_uq', 'new_v_g_kv', 'new_v_w_ukv', 'new_v_w_o_attn', 'new_v_w_dw', 'new_v_b_dw', 'new_v_g_cn', 'new_v_b_cn', 'new_v_w_pw2', 'new_v_w_out', 'new_v_g_ffn', 'new_v_w_gu', 'new_v_w_down', 'new_v_g_final']
TWIN_LEAF_KINDS = {'loss': 'loss', 'grad_x': 'grad_x', 'grad_w_ada': 'grad_w', 'grad_b_ada': 'grad_w', 'grad_g_mix': 'grad_w', 'grad_w_in': 'grad_w', 'grad_g_q': 'grad_w', 'grad_w_uq': 'grad_w', 'grad_g_kv': 'grad_w', 'grad_w_ukv': 'grad_w', 'grad_w_o_attn': 'grad_w', 'grad_w_dw': 'grad_w', 'grad_b_dw': 'grad_w', 'grad_g_cn': 'grad_w', 'grad_b_cn': 'grad_w', 'grad_w_pw2': 'grad_w', 'grad_w_out': 'grad_w', 'grad_g_ffn': 'grad_w', 'grad_w_gu': 'grad_w', 'grad_w_down': 'grad_w', 'grad_g_final': 'grad_w', 'delta_w_ada': 'delta_w', 'delta_b_ada': 'delta_w', 'delta_g_mix': 'delta_w', 'delta_w_in': 'delta_w', 'delta_g_q': 'delta_w', 'delta_w_uq': 'delta_w', 'delta_g_kv': 'delta_w', 'delta_w_ukv': 'delta_w', 'delta_w_o_attn': 'delta_w', 'delta_w_dw': 'delta_w', 'delta_b_dw': 'delta_w', 'delta_g_cn': 'delta_w', 'delta_b_cn': 'delta_w', 'delta_w_pw2': 'delta_w', 'delta_w_out': 'delta_w', 'delta_g_ffn': 'delta_w', 'delta_w_gu': 'delta_w', 'delta_w_down': 'delta_w', 'delta_g_final': 'delta_w', 'new_m_w_ada': 'new_m', 'new_m_b_ada': 'new_m', 'new_m_g_mix': 'new_m', 'new_m_w_in': 'new_m', 'new_m_g_q': 'new_m', 'new_m_w_uq': 'new_m', 'new_m_g_kv': 'new_m', 'new_m_w_ukv': 'new_m', 'new_m_w_o_attn': 'new_m', 'new_m_w_dw': 'new_m', 'new_m_b_dw': 'new_m', 'new_m_g_cn': 'new_m', 'new_m_b_cn': 'new_m', 'new_m_w_pw2': 'new_m', 'new_m_w_out': 'new_m', 'new_m_g_ffn': 'new_m', 'new_m_w_gu': 'new_m', 'new_m_w_down': 'new_m', 'new_m_g_final': 'new_m', 'new_v_w_ada': 'new_v', 'new_v_b_ada': 'new_v', 'new_v_g_mix': 'new_v', 'new_v_w_in': 'new_v', 'new_v_g_q': 'new_v', 'new_v_w_uq': 'new_v', 'new_v_g_kv': 'new_v', 'new_v_w_ukv': 'new_v', 'new_v_w_o_attn': 'new_v', 'new_v_w_dw': 'new_v', 'new_v_b_dw': 'new_v', 'new_v_g_cn': 'new_v', 'new_v_b_cn': 'new_v', 'new_v_w_pw2': 'new_v', 'new_v_w_out': 'new_v', 'new_v_g_ffn': 'new_v', 'new_v_w_gu': 'new_v', 'new_v_w_down': 'new_v', 'new_v_g_final': 'new_v'}


def _forward(args):
    return _fwd_reference(*[args[k] for k in FWD_PARAMS])


def _output_shape():
    out = _jax.eval_shape(lambda: _forward(_fwd_setup_inputs(0)))
    return out.shape, out.dtype

N_MICROBATCH = 1
ADAM_LR = 0.001
ADAM_B1 = 0.9
ADAM_B2 = 0.999
ADAM_EPS = 1e-08
ADAM_WD = 0.01
ADAM_STEP = 10
PER_EXAMPLE_BATCH_AXIS = {'x': 0, 'c': 0, 'positions': 0, 'loss_target': 0}
SHARED_INPUTS = []
_WEIGHT_DTYPES = {'w_ada': _jnp.float32, 'b_ada': _jnp.float32, 'g_mix': _jnp.float32, 'w_in': _jnp.float32, 'g_q': _jnp.float32, 'w_uq': _jnp.float32, 'g_kv': _jnp.float32, 'w_ukv': _jnp.float32, 'w_o_attn': _jnp.float32, 'w_dw': _jnp.float32, 'b_dw': _jnp.float32, 'g_cn': _jnp.float32, 'b_cn': _jnp.float32, 'w_pw2': _jnp.float32, 'w_out': _jnp.float32, 'g_ffn': _jnp.float32, 'w_gu': _jnp.float32, 'w_down': _jnp.float32, 'g_final': _jnp.float32}
MOMENT_SCALE = {'w_ada': 6.000249e-02, 'b_ada': 9.614871e-02, 'g_mix': 3.284454e-02, 'w_in': 1.894894e-02, 'g_q': 1.186313e-02, 'w_uq': 8.465189e-03, 'g_kv': 3.404784e-02, 'w_ukv': 1.673795e-02, 'w_o_attn': 1.551192e-02, 'w_dw': 3.747703e-02, 'b_dw': 7.133089e-02, 'g_cn': 4.370471e-02, 'b_cn': 3.624943e-02, 'w_pw2': 2.563025e-02, 'w_out': 2.973574e-02, 'g_ffn': 7.597451e-02, 'w_gu': 3.348466e-02, 'w_down': 5.469840e-02, 'g_final': 6.396417e+01}


def _to_microbatches(a, axis):
    t = _jnp.moveaxis(a, axis, 0)
    t = t.reshape((N_MICROBATCH, t.shape[0] // N_MICROBATCH) + t.shape[1:])
    return _jnp.moveaxis(t, 1, axis + 1)


def setup_inputs(seed: int = 0) -> dict:
    inp = _fwd_setup_inputs(seed)
    key = _jax.random.fold_in(_jax.random.key(seed), 7919)
    shape, _ = _output_shape()
    out = dict(inp)
    out["loss_target"] = _jax.random.normal(_jax.random.fold_in(key, 0), shape, _jnp.float32)
    for i, name in enumerate(TWIN_WEIGHTS):
        w = inp[name].astype(_jnp.float32)
        if MOMENT_SCALE is None:
            s = _jnp.sqrt(_jnp.mean(_jnp.square(w)) + 1e-30)
        else:
            s = MOMENT_SCALE[name]
        km, kv = _jax.random.split(_jax.random.fold_in(key, i + 1))
        out[name] = w
        out["m_" + name] = s * _jax.random.normal(km, w.shape, _jnp.float32)
        out["v_" + name] = (s * s) * _jax.random.uniform(kv, w.shape, _jnp.float32, 0.5, 1.5)
    if N_MICROBATCH > 1:
        for name, axis in PER_EXAMPLE_BATCH_AXIS.items():
            out[name] = _to_microbatches(out[name], axis)
    return {'x': out['x'], 'c': out['c'], 'positions': out['positions'], 'w_ada': out['w_ada'], 'b_ada': out['b_ada'], 'g_mix': out['g_mix'], 'w_in': out['w_in'], 'g_q': out['g_q'], 'w_uq': out['w_uq'], 'g_kv': out['g_kv'], 'w_ukv': out['w_ukv'], 'w_o_attn': out['w_o_attn'], 'w_dw': out['w_dw'], 'b_dw': out['b_dw'], 'g_cn': out['g_cn'], 'b_cn': out['b_cn'], 'w_pw2': out['w_pw2'], 'w_out': out['w_out'], 'g_ffn': out['g_ffn'], 'w_gu': out['w_gu'], 'w_down': out['w_down'], 'g_final': out['g_final'], 'loss_target': out['loss_target'], 'm_w_ada': out['m_w_ada'], 'm_b_ada': out['m_b_ada'], 'm_g_mix': out['m_g_mix'], 'm_w_in': out['m_w_in'], 'm_g_q': out['m_g_q'], 'm_w_uq': out['m_w_uq'], 'm_g_kv': out['m_g_kv'], 'm_w_ukv': out['m_w_ukv'], 'm_w_o_attn': out['m_w_o_attn'], 'm_w_dw': out['m_w_dw'], 'm_b_dw': out['m_b_dw'], 'm_g_cn': out['m_g_cn'], 'm_b_cn': out['m_b_cn'], 'm_w_pw2': out['m_w_pw2'], 'm_w_out': out['m_w_out'], 'm_g_ffn': out['m_g_ffn'], 'm_w_gu': out['m_w_gu'], 'm_w_down': out['m_w_down'], 'm_g_final': out['m_g_final'], 'v_w_ada': out['v_w_ada'], 'v_b_ada': out['v_b_ada'], 'v_g_mix': out['v_g_mix'], 'v_w_in': out['v_w_in'], 'v_g_q': out['v_g_q'], 'v_w_uq': out['v_w_uq'], 'v_g_kv': out['v_g_kv'], 'v_w_ukv': out['v_w_ukv'], 'v_w_o_attn': out['v_w_o_attn'], 'v_w_dw': out['v_w_dw'], 'v_b_dw': out['v_b_dw'], 'v_g_cn': out['v_g_cn'], 'v_b_cn': out['v_b_cn'], 'v_w_pw2': out['v_w_pw2'], 'v_w_out': out['v_w_out'], 'v_g_ffn': out['v_g_ffn'], 'v_w_gu': out['v_w_gu'], 'v_w_down': out['v_w_down'], 'v_g_final': out['v_g_final']}


def _loss(weights, diff, rest, loss_target):
    with _jax.named_scope("forward"):
        args = {**rest, TWIN_DIFF_INPUT: diff, **{k: w.astype(_WEIGHT_DTYPES[k]) for k, w in weights.items()}}
        y = _forward(args)
    with _jax.named_scope("loss_head"):
        err = _jnp.square(y.astype(_jnp.float32) - loss_target)
        return 0.5 * _jnp.sum(_jnp.mean(err, axis=-1)) if err.ndim else 0.5 * err


def _adamw(w, g, m, v):
    m = ADAM_B1 * m + (1.0 - ADAM_B1) * g
    v = ADAM_B2 * v + (1.0 - ADAM_B2) * _jnp.square(g)
    m_hat = m / (1.0 - ADAM_B1 ** ADAM_STEP)
    v_hat = v / (1.0 - ADAM_B2 ** ADAM_STEP)
    delta = -ADAM_LR * (m_hat / (_jnp.sqrt(v_hat) + ADAM_EPS) + ADAM_WD * w)
    return delta, m, v


def reference(x, c, positions, w_ada, b_ada, g_mix, w_in, g_q, w_uq, g_kv, w_ukv, w_o_attn, w_dw, b_dw, g_cn, b_cn, w_pw2, w_out, g_ffn, w_gu, w_down, g_final, loss_target, m_w_ada, m_b_ada, m_g_mix, m_w_in, m_g_q, m_w_uq, m_g_kv, m_w_ukv, m_w_o_attn, m_w_dw, m_b_dw, m_g_cn, m_b_cn, m_w_pw2, m_w_out, m_g_ffn, m_w_gu, m_w_down, m_g_final, v_w_ada, v_b_ada, v_g_mix, v_w_in, v_g_q, v_w_uq, v_g_kv, v_w_ukv, v_w_o_attn, v_w_dw, v_b_dw, v_g_cn, v_b_cn, v_w_pw2, v_w_out, v_g_ffn, v_w_gu, v_w_down, v_g_final):
    given = dict(x=x, c=c, positions=positions, w_ada=w_ada, b_ada=b_ada, g_mix=g_mix, w_in=w_in, g_q=g_q, w_uq=w_uq, g_kv=g_kv, w_ukv=w_ukv, w_o_attn=w_o_attn, w_dw=w_dw, b_dw=b_dw, g_cn=g_cn, b_cn=b_cn, w_pw2=w_pw2, w_out=w_out, g_ffn=g_ffn, w_gu=w_gu, w_down=w_down, g_final=g_final, loss_target=loss_target, m_w_ada=m_w_ada, m_b_ada=m_b_ada, m_g_mix=m_g_mix, m_w_in=m_w_in, m_g_q=m_g_q, m_w_uq=m_w_uq, m_g_kv=m_g_kv, m_w_ukv=m_w_ukv, m_w_o_attn=m_w_o_attn, m_w_dw=m_w_dw, m_b_dw=m_b_dw, m_g_cn=m_g_cn, m_b_cn=m_b_cn, m_w_pw2=m_w_pw2, m_w_out=m_w_out, m_g_ffn=m_g_ffn, m_w_gu=m_w_gu, m_w_down=m_w_down, m_g_final=m_g_final, v_w_ada=v_w_ada, v_b_ada=v_b_ada, v_g_mix=v_g_mix, v_w_in=v_w_in, v_g_q=v_g_q, v_w_uq=v_w_uq, v_g_kv=v_g_kv, v_w_ukv=v_w_ukv, v_w_o_attn=v_w_o_attn, v_w_dw=v_w_dw, v_b_dw=v_b_dw, v_g_cn=v_g_cn, v_b_cn=v_b_cn, v_w_pw2=v_w_pw2, v_w_out=v_w_out, v_g_ffn=v_g_ffn, v_w_gu=v_w_gu, v_w_down=v_w_down, v_g_final=v_g_final)
    weights = {n: given[n] for n in TWIN_WEIGHTS}
    shared = {n: given[n] for n in SHARED_INPUTS}
    per_example = {n: given[n] for n in ['x', 'c', 'positions']}
    grad_fn = _jax.value_and_grad(_loss, argnums=(0, 1))

    def one_microbatch(ex, loss_target):
        ex = dict(ex)
        diff = ex.pop(TWIN_DIFF_INPUT)
        return grad_fn(weights, diff, {**shared, **ex}, loss_target)

    if N_MICROBATCH == 1:
        loss, (grad_w, grad_x) = one_microbatch(per_example, given["loss_target"])
    else:
        def body(carry, xs):
            loss_sum, grad_sum = carry
            l_k, (gw_k, gx_k) = one_microbatch(xs[0], xs[1])
            with _jax.named_scope("update"):
                return (loss_sum + l_k, _jax.tree.map(_jnp.add, grad_sum, gw_k)), gx_k

        init = (_jnp.zeros((), _jnp.float32), _jax.tree.map(_jnp.zeros_like, weights))
        (loss, grad_w), grad_x = _jax.lax.scan(body, init, (per_example, given["loss_target"]))
    with _jax.named_scope("update"):
        delta_w, new_m, new_v = {}, {}, {}
        for n in TWIN_WEIGHTS:
            delta_w[n], new_m[n], new_v[n] = _adamw(weights[n], grad_w[n], given["m_" + n], given["v_" + n])
    return (loss, grad_x, *[grad_w[n] for n in TWIN_WEIGHTS], *[delta_w[n] for n in TWIN_WEIGHTS],
            *[new_m[n] for n in TWIN_WEIGHTS], *[new_v[n] for n in TWIN_WEIGHTS])
```

```python
import functools

import numpy as np
import jax
import jax.numpy as jnp
from jax import lax
from jax.experimental import pallas as pl
from jax.experimental.pallas import tpu as pltpu

F32 = jnp.float32
BF16 = jnp.bfloat16
MESH = pl.DeviceIdType.MESH

EPS = 1e-6
NEG_INF = -1e30
NOPE, ROPE, VDIM = 64, 32, 64
QK_DIM = NOPE + ROPE
SLOT = 128
CONV_W = 31
HALO = 32
N_MOD = 6
ROPE_THETA = 10000.0
N_CHIPS = 4
N_DEV = 8
SUB = 8
LANES = 128
VMEM_LIMIT = 56 * 1024 * 1024

ADAM_LR, ADAM_B1, ADAM_B2, ADAM_EPS, ADAM_WD, ADAM_STEP = 0.001, 0.9, 0.999, 1e-08, 0.01, 10


def _tile(n, cap, mult):
    best = None
    for d in range(mult, min(n, cap) + 1, mult):
        if n % d == 0:
            best = d
    return best if best is not None else n


def _params(sem):
    return pltpu.CompilerParams(dimension_semantics=sem, vmem_limit_bytes=VMEM_LIMIT)


def _sig(x):
    return 1.0 / (1.0 + jnp.exp(-x))


def _sum8(x):
    r, w = x.shape
    return jnp.sum(x.reshape(r // SUB, SUB, w), axis=0)


def _lanes(v, n):
    return v if n == v.shape[1] else jnp.tile(v, (1, n // v.shape[1]))


def _rstd(x):
    return lax.rsqrt(jnp.mean(x * x, axis=-1, keepdims=True) + EPS)


def _norm_bwd(dxhat, xhat, rstd):
    return rstd * (dxhat - xhat * jnp.mean(dxhat * xhat, axis=-1, keepdims=True))


def mm_fused(name, As, pairs, extras, epilogue, out_dtypes, tm, tn, S):
    T = As[0].shape[0]
    N = pairs[0][1].shape[1]
    nex = S // tm
    in_specs, args = [], []
    for a in As:
        in_specs.append(pl.BlockSpec((tm, a.shape[1]), lambda i, j: (i, 0)))
        args.append(a)
    for _, b in pairs:
        in_specs.append(pl.BlockSpec((b.shape[0], tn), lambda i, j: (0, j)))
        args.append(b)
    for kind, arr in extras:
        if kind == 'tile':
            in_specs.append(pl.BlockSpec((tm, tn), lambda i, j: (i, j)))
        elif kind == 'row128':
            in_specs.append(pl.BlockSpec((tm, LANES), lambda i, j: (i, 0)))
        elif kind == 'vec':
            in_specs.append(pl.BlockSpec((1, tn), lambda i, j: (0, j)))
        else:
            in_specs.append(pl.BlockSpec((None, 1, tn), lambda i, j: (i // nex, 0, j)))
        args.append(arr)
    n_a, n_p, n_e = len(As), len(pairs), len(extras)

    def body(*refs):
        a_refs, b_refs = refs[:n_a], refs[n_a:n_a + n_p]
        e_refs, o_refs = refs[n_a + n_p:n_a + n_p + n_e], refs[n_a + n_p + n_e:]
        accs = [jnp.dot(a_refs[ai][...], b_refs[k][...], preferred_element_type=F32)
                for k, (ai, _) in enumerate(pairs)]
        outs = epilogue(accs, [r[...] for r in e_refs])
        for o_ref, o in zip(o_refs, outs):
            o_ref[...] = o.astype(o_ref.dtype)

    return pl.pallas_call(
        body, name=name, grid=(T // tm, N // tn), in_specs=in_specs,
        out_specs=[pl.BlockSpec((tm, tn), lambda i, j: (i, j)) for _ in out_dtypes],
        out_shape=[jax.ShapeDtypeStruct((T, N), dt) for dt in out_dtypes],
        compiler_params=_params(("parallel", "parallel")))(*args)


def mm_tn(name, A, G):
    T, K = A.shape
    N = G.shape[1]
    tt = _tile(T, 512, 16)
    tk = _tile(K, 1536, LANES)
    tn = _tile(N, 1536, LANES)

    def body(a_ref, g_ref, o_ref):
        part = lax.dot_general(a_ref[...], g_ref[...], (((0,), (0,)), ((), ())),
                               preferred_element_type=F32)

        @pl.when(pl.program_id(2) == 0)
        def _():
            o_ref[...] = part

        @pl.when(pl.program_id(2) > 0)
        def _():
            o_ref[...] += part

    return pl.pallas_call(
        body, name=name, grid=(K // tk, N // tn, T // tt),
        in_specs=[pl.BlockSpec((tt, tk), lambda k, n, t: (t, k)),
                  pl.BlockSpec((tt, tn), lambda k, n, t: (t, n))],
        out_specs=pl.BlockSpec((tk, tn), lambda k, n, t: (k, n)),
        out_shape=jax.ShapeDtypeStruct((K, N), F32),
        compiler_params=_params(("parallel", "parallel", "arbitrary")))(A, G)


def rowwise(name, ins, outs, fn, tm, S):
    T = next(a.shape[0] for k, a in ins if k == 'tile')
    nex = S // tm
    n_ex = T // S
    in_specs, args = [], []
    for kind, arr in ins:
        if kind == 'tile':
            in_specs.append(pl.BlockSpec((tm, arr.shape[1]), lambda i: (i, 0)))
        elif kind == 'vec':
            in_specs.append(pl.BlockSpec((1, arr.shape[1]), lambda i: (0, 0)))
        else:
            in_specs.append(pl.BlockSpec((None, 1, arr.shape[2]), lambda i: (i // nex, 0, 0)))
        args.append(arr)
    out_specs, out_shape = [], []
    for o in outs:
        if o[0] == 'tile':
            out_specs.append(pl.BlockSpec((tm, o[1]), lambda i: (i, 0)))
            out_shape.append(jax.ShapeDtypeStruct((T, o[1]), o[2]))
        elif o[0] == 'acc':
            out_specs.append(pl.BlockSpec((SUB, o[1]), lambda i: (0, 0)))
            out_shape.append(jax.ShapeDtypeStruct((SUB, o[1]), F32))
        else:
            out_specs.append(pl.BlockSpec((SUB, o[1]), lambda i: (i // nex, 0)))
            out_shape.append(jax.ShapeDtypeStruct((n_ex * SUB, o[1]), F32))
    n_in = len(ins)

    def body(*refs):
        i = pl.program_id(0)
        vals = fn(*[r[...] for r in refs[:n_in]])
        for o, o_ref, v in zip(outs, refs[n_in:], vals):
            if o[0] == 'tile':
                o_ref[...] = v.astype(o_ref.dtype)
            else:
                part = _sum8(v)
                first = (i == 0) if o[0] == 'acc' else (i % nex == 0)

                @pl.when(first)
                def _(o_ref=o_ref, part=part):
                    o_ref[...] = part

                @pl.when(jnp.logical_not(first))
                def _(o_ref=o_ref, part=part):
                    o_ref[...] += part

    return pl.pallas_call(
        body, name=name, grid=(T // tm,), in_specs=in_specs, out_specs=out_specs,
        out_shape=out_shape, compiler_params=_params(("arbitrary",)))(*args)


NT = (((1,), (1,)), ((), ()))
TN = (((0,), (0,)), ((), ()))


def _causal(s):
    r = lax.broadcasted_iota(jnp.int32, s.shape, 0)
    c = lax.broadcasted_iota(jnp.int32, s.shape, 1)
    return jnp.where(r >= c, s, NEG_INF)


def attn_fwd(Q, K, V, n_b, S, n_h, blk, scale):
    T = n_b * S
    nb = S // blk

    def body(q_ref, k_ref, v_ref, o_ref, lse_ref, m_s, l_s, acc_s):
        i, j = pl.program_id(2), pl.program_id(3)

        @pl.when(j == 0)
        def _():
            m_s[...] = jnp.full(m_s.shape, NEG_INF, F32)
            l_s[...] = jnp.zeros(l_s.shape, F32)
            acc_s[...] = jnp.zeros(acc_s.shape, F32)

        def step(masked):
            s = lax.dot_general(q_ref[...], k_ref[...], NT, preferred_element_type=F32) * scale
            if masked:
                s = _causal(s)
            m_prev = m_s[...]
            m_new = jnp.maximum(m_prev, jnp.max(s, axis=1, keepdims=True))
            alpha = jnp.exp(m_prev - m_new)
            p = jnp.exp(s - m_new[:, :1])
            l_s[...] = alpha * l_s[...] + jnp.sum(p, axis=1, keepdims=True)
            acc_s[...] = alpha * acc_s[...] + jnp.dot(p.astype(BF16), v_ref[...],
                                                      preferred_element_type=F32)
            m_s[...] = m_new

        @pl.when(j < i)
        def _():
            step(False)

        @pl.when(j == i)
        def _():
            step(True)
            l = l_s[...]
            o_ref[...] = (acc_s[...] / l).astype(o_ref.dtype)
            lse_ref[...] = m_s[...] + jnp.log(l)

    qmap = lambda b, h, i, j: (b * nb + i, h)
    kmap = lambda b, h, i, j: (b * nb + jnp.minimum(i, j), h)
    return pl.pallas_call(
        body, name="attn_fwd", grid=(n_b, n_h, nb, nb),
        in_specs=[pl.BlockSpec((blk, SLOT), qmap), pl.BlockSpec((blk, SLOT), kmap),
                  pl.BlockSpec((blk, SLOT), kmap)],
        out_specs=[pl.BlockSpec((blk, SLOT), qmap), pl.BlockSpec((blk, SLOT), qmap)],
        out_shape=[jax.ShapeDtypeStruct((T, n_h * SLOT), BF16),
                   jax.ShapeDtypeStruct((T, n_h * SLOT), F32)],
        scratch_shapes=[pltpu.VMEM((blk, SLOT), F32)] * 3,
        compiler_params=_params(("parallel", "parallel", "arbitrary", "arbitrary")))(Q, K, V)


def _attn_bwd_core(q, k, v, o, do, lse, scale, masked):
    s = lax.dot_general(q, k, NT, preferred_element_type=F32) * scale
    if masked:
        s = _causal(s)
    p = jnp.exp(s - lse[:, :1])
    dp = lax.dot_general(do, v, NT, preferred_element_type=F32)
    delta = jnp.sum(do.astype(F32) * o.astype(F32), axis=1, keepdims=True)
    ds = p * (dp - delta) * scale
    return p, ds


def attn_bwd_kv(Q, K, V, O, dO, LSE, n_b, S, n_h, blk, scale):
    T = n_b * S
    nb = S // blk

    def body(q_ref, k_ref, v_ref, o_ref, do_ref, lse_ref, dk_ref, dv_ref, dk_s, dv_s):
        j, i = pl.program_id(2), pl.program_id(3)

        @pl.when(i == 0)
        def _():
            dk_s[...] = jnp.zeros(dk_s.shape, F32)
            dv_s[...] = jnp.zeros(dv_s.shape, F32)

        def step(masked):
            q, do = q_ref[...], do_ref[...]
            p, ds = _attn_bwd_core(q, k_ref[...], v_ref[...], o_ref[...], do, lse_ref[...],
                                   scale, masked)
            dv_s[...] += lax.dot_general(p.astype(BF16), do, TN, preferred_element_type=F32)
            dk_s[...] += lax.dot_general(ds.astype(BF16), q, TN, preferred_element_type=F32)

        @pl.when(i > j)
        def _():
            step(False)

        @pl.when(i == j)
        def _():
            step(True)

        @pl.when(i == nb - 1)
        def _():
            dk_ref[...] = dk_s[...]
            dv_ref[...] = dv_s[...].astype(dv_ref.dtype)

    kvmap = lambda b, h, j, i: (b * nb + j, h)
    qmap = lambda b, h, j, i: (b * nb + jnp.maximum(i, j), h)
    qs, ks = pl.BlockSpec((blk, SLOT), qmap), pl.BlockSpec((blk, SLOT), kvmap)
    return pl.pallas_call(
        body, name="attn_bwd_kv", grid=(n_b, n_h, nb, nb),
        in_specs=[qs, ks, ks, qs, qs, qs], out_specs=[ks, ks],
        out_shape=[jax.ShapeDtypeStruct((T, n_h * SLOT), F32),
                   jax.ShapeDtypeStruct((T, n_h * SLOT), BF16)],
        scratch_shapes=[pltpu.VMEM((blk, SLOT), F32)] * 2,
        compiler_params=_params(("parallel", "parallel", "arbitrary", "arbitrary")))(
            Q, K, V, O, dO, LSE)


def attn_bwd_q(Q, K, V, O, dO, LSE, cosq, sinq, n_b, S, n_h, blk, scale):
    T = n_b * S
    nb = S // blk

    def body(q_ref, k_ref, v_ref, o_ref, do_ref, lse_ref, cos_ref, sin_ref, ga_ref, gb_ref, dq_s):
        i, j = pl.program_id(2), pl.program_id(3)

        @pl.when(j == 0)
        def _():
            dq_s[...] = jnp.zeros(dq_s.shape, F32)

        def step(masked):
            k = k_ref[...]
            _, ds = _attn_bwd_core(q_ref[...], k, v_ref[...], o_ref[...], do_ref[...],
                                   lse_ref[...], scale, masked)
            dq_s[...] += jnp.dot(ds.astype(BF16), k, preferred_element_type=F32)

        @pl.when(j < i)
        def _():
            step(False)

        @pl.when(j == i)
        def _():
            step(True)
            dq = dq_s[...]
            ga_ref[...] = (dq * cos_ref[...]).astype(ga_ref.dtype)
            gb_ref[...] = (dq * sin_ref[...]).astype(gb_ref.dtype)

    qmap = lambda b, h, i, j: (b * nb + i, h)
    kmap = lambda b, h, i, j: (b * nb + jnp.minimum(i, j), h)
    tmap = lambda b, h, i, j: (b * nb + i, 0)
    qs, ks, ts = (pl.BlockSpec((blk, SLOT), qmap), pl.BlockSpec((blk, SLOT), kmap),
                  pl.BlockSpec((blk, SLOT), tmap))
    return pl.pallas_call(
        body, name="attn_bwd_q", grid=(n_b, n_h, nb, nb),
        in_specs=[qs, ks, ks, qs, qs, qs, ts, ts], out_specs=[qs, qs],
        out_shape=[jax.ShapeDtypeStruct((T, n_h * SLOT), BF16)] * 2,
        scratch_shapes=[pltpu.VMEM((blk, SLOT), F32)],
        compiler_params=_params(("parallel", "parallel", "arbitrary", "arbitrary")))(
            Q, K, V, O, dO, LSE, cosq, sinq)


def _layer_norm_parts(u):
    xc = u - jnp.mean(u, axis=-1, keepdims=True)
    rstd = lax.rsqrt(jnp.mean(xc * xc, axis=-1, keepdims=True) + EPS)
    return xc * rstd, rstd


def conv_fwd(cu, w, b_dw, g_cn, b_cn, tm, S):
    T, C = cu.shape
    nex, hb = S // tm, tm // HALO

    def body(cur_ref, prev_ref, w_ref, b_ref, g_ref, bc_ref, u_ref, s_ref, ext):
        first = pl.program_id(0) % nex == 0
        ext[pl.ds(0, HALO), :] = jnp.where(first, 0.0, prev_ref[...])
        ext[pl.ds(HALO, tm), :] = cur_ref[...]
        acc = jnp.zeros((tm, C), F32)
        for j in range(CONV_W):
            acc = acc + w_ref[pl.ds(j, 1), :] * ext[pl.ds(HALO - CONV_W + 1 + j, tm), :]
        u = acc + b_ref[...]
        ln = _layer_norm_parts(u)[0] * g_ref[...] + bc_ref[...]
        u_ref[...] = u
        s_ref[...] = (ln * _sig(ln)).astype(s_ref.dtype)

    vec = pl.BlockSpec((1, C), lambda i: (0, 0))
    return pl.pallas_call(
        body, name="conv_fwd", grid=(T // tm,),
        in_specs=[pl.BlockSpec((tm, C), lambda i: (i, 0)),
                  pl.BlockSpec((HALO, C), lambda i: (jnp.maximum(i * hb - 1, 0), 0)),
                  pl.BlockSpec((HALO, C), lambda i: (0, 0)), vec, vec, vec],
        out_specs=[pl.BlockSpec((tm, C), lambda i: (i, 0))] * 2,
        out_shape=[jax.ShapeDtypeStruct((T, C), F32), jax.ShapeDtypeStruct((T, C), BF16)],
        scratch_shapes=[pltpu.VMEM((HALO + tm, C), F32)],
        compiler_params=_params(("arbitrary",)))(cu, cu, w, b_dw, g_cn, b_cn)


def conv_bwd(du, cu, glu_a, glu_b, w, tm, S):
    T, C = du.shape
    nex, hb = S // tm, tm // HALO
    last_blk = T // HALO - 1

    def body(du_ref, nxt_ref, cu_ref, prev_ref, a_ref, b_ref, w_ref,
             da_ref, db_ref, dw_ref, dbias_ref, extd, extc):
        i = pl.program_id(0)
        first = i % nex == 0
        last = i % nex == nex - 1
        du_cur = du_ref[...]
        extd[pl.ds(0, tm), :] = du_cur
        extd[pl.ds(tm, HALO), :] = jnp.where(last, 0.0, nxt_ref[...])
        extc[pl.ds(0, HALO), :] = jnp.where(first, 0.0, prev_ref[...])
        extc[pl.ds(HALO, tm), :] = cu_ref[...]

        @pl.when(i == 0)
        def _():
            dw_ref[...] = jnp.zeros(dw_ref.shape, F32)
            dbias_ref[...] = jnp.zeros(dbias_ref.shape, F32)

        dcu = jnp.zeros((tm, C), F32)
        for j in range(CONV_W):
            dcu = dcu + w_ref[pl.ds(j, 1), :] * extd[pl.ds(CONV_W - 1 - j, tm), :]
            dw_ref[pl.ds(SUB * j, SUB), :] += _sum8(
                du_cur * extc[pl.ds(HALO - CONV_W + 1 + j, tm), :])
        dbias_ref[...] += _sum8(du_cur)
        sb = _sig(b_ref[...])
        da_ref[...] = (dcu * sb).astype(da_ref.dtype)
        db_ref[...] = (dcu * a_ref[...] * sb * (1.0 - sb)).astype(db_ref.dtype)

    cur = pl.BlockSpec((tm, C), lambda i: (i, 0))
    return pl.pallas_call(
        body, name="conv_bwd", grid=(T // tm,),
        in_specs=[cur, pl.BlockSpec((HALO, C), lambda i: (jnp.minimum((i + 1) * hb, last_blk), 0)),
                  cur, pl.BlockSpec((HALO, C), lambda i: (jnp.maximum(i * hb - 1, 0), 0)),
                  cur, cur, pl.BlockSpec((HALO, C), lambda i: (0, 0))],
        out_specs=[cur, cur, pl.BlockSpec((HALO * SUB, C), lambda i: (0, 0)),
                   pl.BlockSpec((SUB, C), lambda i: (0, 0))],
        out_shape=[jax.ShapeDtypeStruct((T, C), BF16), jax.ShapeDtypeStruct((T, C), BF16),
                   jax.ShapeDtypeStruct((HALO * SUB, C), F32), jax.ShapeDtypeStruct((SUB, C), F32)],
        scratch_shapes=[pltpu.VMEM((tm + HALO, C), F32), pltpu.VMEM((HALO + tm, C), F32)],
        compiler_params=_params(("arbitrary",)))(du, du, cu, cu, glu_a, glu_b, w)


def silu_small(c_all):
    def body(c_ref, o_ref):
        v = c_ref[...]
        o_ref[...] = v * _sig(v)
    return pl.pallas_call(body, name="silu_c", out_shape=jax.ShapeDtypeStruct(c_all.shape, F32))(c_all)


def ada_fwd(c_act, w_ada, b_ada):
    n_l, D, ns = w_ada.shape
    B = c_act.shape[0]
    tn = _tile(ns, 512, LANES)

    def body(c_ref, w_ref, b_ref, o_ref):
        o_ref[...] = jnp.dot(c_ref[...], w_ref[...], preferred_element_type=F32,
                             precision=lax.Precision.HIGHEST) + b_ref[...]

    return pl.pallas_call(
        body, name="ada_fwd", grid=(n_l, ns // tn),
        in_specs=[pl.BlockSpec((B, D), lambda l, j: (0, 0)),
                  pl.BlockSpec((None, D, tn), lambda l, j: (l, 0, j)),
                  pl.BlockSpec((None, 1, tn), lambda l, j: (l, 0, j))],
        out_specs=pl.BlockSpec((None, B, tn), lambda l, j: (l, 0, j)),
        out_shape=jax.ShapeDtypeStruct((n_l, B, ns), F32),
        compiler_params=_params(("parallel", "parallel")))(c_act, w_ada, b_ada.reshape(n_l, 1, ns))


def ada_bwd_w(c_act_t, dmod):
    D, B = c_act_t.shape
    n_l, _, ns = dmod.shape
    tn = _tile(ns, 512, LANES)

    def body(c_ref, d_ref, o_ref):
        o_ref[...] = jnp.dot(c_ref[...], d_ref[...], preferred_element_type=F32,
                             precision=lax.Precision.HIGHEST)

    return pl.pallas_call(
        body, name="ada_bwd_w", grid=(n_l, ns // tn),
        in_specs=[pl.BlockSpec((D, B), lambda l, j: (0, 0)),
                  pl.BlockSpec((None, B, tn), lambda l, j: (l, 0, j))],
        out_specs=pl.BlockSpec((None, D, tn), lambda l, j: (l, 0, j)),
        out_shape=jax.ShapeDtypeStruct((n_l, D, ns), F32),
        compiler_params=_params(("parallel", "parallel")))(c_act_t, dmod)


def group_sum(name, v, group):
    rows, W = v.shape
    n = rows // group

    def body(v_ref, o_ref):
        o_ref[...] = jnp.sum(v_ref[...].reshape(n, group, W), axis=1)

    return pl.pallas_call(body, name=name, out_shape=jax.ShapeDtypeStruct((n, W), F32),
                          compiler_params=_params(None))(v)


def lane_total(name, v):
    def body(v_ref, o_ref):
        o_ref[...] = jnp.broadcast_to(jnp.sum(v_ref[...], axis=1, keepdims=True), o_ref.shape)
    return pl.pallas_call(body, name=name, out_shape=jax.ShapeDtypeStruct((1, LANES), F32))(v)


def adamw(name, w, g, m, v):
    rows, cols = w.shape
    tr = _tile(rows, max(SUB, (1 << 19) // cols // SUB * SUB), SUB)

    def body(w_ref, g_ref, m_ref, v_ref, d_ref, nm_ref, nv_ref):
        gg = g_ref[...]
        nm = ADAM_B1 * m_ref[...] + (1.0 - ADAM_B1) * gg
        nv = ADAM_B2 * v_ref[...] + (1.0 - ADAM_B2) * (gg * gg)
        m_hat = nm / (1.0 - ADAM_B1 ** ADAM_STEP)
        v_hat = nv / (1.0 - ADAM_B2 ** ADAM_STEP)
        d_ref[...] = -ADAM_LR * (m_hat / (jnp.sqrt(v_hat) + ADAM_EPS) + ADAM_WD * w_ref[...])
        nm_ref[...] = nm
        nv_ref[...] = nv

    spec = pl.BlockSpec((tr, cols), lambda i: (i, 0))
    return pl.pallas_call(
        body, name=name, grid=(rows // tr,), in_specs=[spec] * 4, out_specs=[spec] * 3,
        out_shape=[jax.ShapeDtypeStruct((rows, cols), F32)] * 3,
        compiler_params=_params(("parallel",)))(w, g, m, v)


def rope_tables(pos_col, invf, one_nope, rope_mask, tm):
    T = pos_col.shape[0]

    def body(p_ref, f_ref, o_ref, r_ref, cq_ref, sq_ref, ck_ref):
        ang = p_ref[...] * f_ref[...]
        cs = jnp.cos(ang) * r_ref[...]
        cq_ref[...] = o_ref[...] + cs
        sq_ref[...] = jnp.sin(ang) * r_ref[...]
        ck_ref[...] = cs

    vec = pl.BlockSpec((1, LANES), lambda i: (0, 0))
    out = pl.BlockSpec((tm, LANES), lambda i: (i, 0))
    return pl.pallas_call(
        body, name="rope_tables", grid=(T // tm,),
        in_specs=[pl.BlockSpec((tm, 1), lambda i: (i, 0)), vec, vec, vec], out_specs=[out] * 3,
        out_shape=[jax.ShapeDtypeStruct((T, LANES), F32)] * 3,
        compiler_params=_params(("parallel",)))(pos_col, invf, one_nope, rope_mask)


def _place():
    return lax.axis_index("x"), lax.axis_index("y"), lax.axis_index("c")


def allgather8(name, v):
    R, W = v.shape

    def body(x_ref, out_ref, send_sems, recv_sems, local_sem):
        x, y, c = _place()
        me = 4 * x + 2 * y + c
        mine = pltpu.make_async_copy(x_ref, out_ref.at[me], local_sem)
        mine.start()
        sends, peers = [], []
        for k in range(1, N_DEV):
            px, py, pc = x ^ ((k >> 2) & 1), y ^ ((k >> 1) & 1), c ^ (k & 1)
            peers.append((px, py, pc))
            cp = pltpu.make_async_remote_copy(
                src_ref=x_ref, dst_ref=out_ref.at[me], send_sem=send_sems.at[k - 1],
                recv_sem=recv_sems.at[k - 1], device_id=(px, py, pc), device_id_type=MESH)
            cp.start()
            sends.append(cp)
        for k, (px, py, pc) in enumerate(peers):
            pltpu.make_async_remote_copy(
                src_ref=x_ref, dst_ref=out_ref.at[4 * px + 2 * py + pc], send_sem=send_sems.at[k],
                recv_sem=recv_sems.at[k], device_id=(px, py, pc), device_id_type=MESH).wait_recv()
        for cp in sends:
            cp.wait_send()
        mine.wait()

    return pl.pallas_call(
        body, name=name, out_shape=jax.ShapeDtypeStruct((N_DEV, R, W), v.dtype),
        in_specs=[pl.BlockSpec(memory_space=pltpu.VMEM)],
        out_specs=pl.BlockSpec(memory_space=pltpu.VMEM),
        scratch_shapes=[pltpu.SemaphoreType.DMA((N_DEV - 1,)), pltpu.SemaphoreType.DMA((N_DEV - 1,)),
                        pltpu.SemaphoreType.DMA])(v)


def _other_chips(x, y):
    return [(1 - x, y), (x, 1 - y), (1 - x, 1 - y)]


def weight_allgather(w):
    _, rh, C = w.shape

    def body(w_ref, out_ref, send_sems, recv_sems, local_sem):
        x, y, c = _place()
        s_me = 2 * x + y
        chips = _other_chips(x, y)

        def copy(k, src, dst, to):
            return pltpu.make_async_remote_copy(src_ref=src, dst_ref=dst, send_sem=send_sems.at[k],
                                                recv_sem=recv_sems.at[k], device_id=to,
                                                device_id_type=MESH)

        mine = pltpu.make_async_copy(w_ref, out_ref.at[s_me], local_sem)
        mine.start()
        first = [copy(j, w_ref.at[c], out_ref.at[s_me, c], (cx, cy, c))
                 for j, (cx, cy) in enumerate(chips)]
        for cp in first:
            cp.start()
        passed = []
        for j, (cx, cy) in enumerate(chips):
            landed = out_ref.at[2 * cx + cy, c]
            copy(j, w_ref.at[c], landed, (cx, cy, c)).wait_recv()
            fw = copy(3 + j, landed, landed, (x, y, 1 - c))
            fw.start()
            passed.append(fw)
        for j, (cx, cy) in enumerate(chips):
            copy(3 + j, w_ref.at[c], out_ref.at[2 * cx + cy, 1 - c], (x, y, 1 - c)).wait_recv()
        for cp in first + passed:
            cp.wait_send()
        mine.wait()

    return pl.pallas_call(
        body, name="weight_allgather",
        out_shape=jax.ShapeDtypeStruct((N_CHIPS, 2, rh, C), w.dtype),
        in_specs=[pl.BlockSpec(memory_space=pl.ANY)], out_specs=pl.BlockSpec(memory_space=pl.ANY),
        scratch_shapes=[pltpu.SemaphoreType.DMA((6,)), pltpu.SemaphoreType.DMA((6,)),
                        pltpu.SemaphoreType.DMA])(w)


def sibling_swap(g):
    _, n_s, rh, C = g.shape

    def body(g_ref, a_ref, send_sem, recv_sem):
        x, y, c = _place()
        cp = pltpu.make_async_remote_copy(src_ref=g_ref.at[1 - c], dst_ref=a_ref, send_sem=send_sem,
                                          recv_sem=recv_sem, device_id=(x, y, 1 - c),
                                          device_id_type=MESH)
        cp.start()
        cp.wait()

    return pl.pallas_call(
        body, name="grad_sibling_swap", out_shape=jax.ShapeDtypeStruct((n_s, rh, C), g.dtype),
        in_specs=[pl.BlockSpec(memory_space=pl.ANY)], out_specs=pl.BlockSpec(memory_space=pl.ANY),
        scratch_shapes=[pltpu.SemaphoreType.DMA, pltpu.SemaphoreType.DMA])(g)


def chip_partial(g, a, c_idx, tr):
    _, n_s, rh, C = g.shape

    def body(c_ref, g_ref, a_ref, o_ref):
        o_ref[...] = (g_ref[...] + a_ref[...]).astype(o_ref.dtype)

    return pl.pallas_call(
        body, name="grad_chip_partial",
        grid_spec=pltpu.PrefetchScalarGridSpec(
            num_scalar_prefetch=1, grid=(n_s, rh // tr),
            in_specs=[pl.BlockSpec((None, None, tr, C), lambda s, r, cr: (cr[0], s, r, 0)),
                      pl.BlockSpec((None, tr, C), lambda s, r, cr: (s, r, 0))],
            out_specs=pl.BlockSpec((None, tr, C), lambda s, r, cr: (s, r, 0))),
        out_shape=jax.ShapeDtypeStruct((n_s, rh, C), BF16),
        compiler_params=_params(("parallel", "parallel")))(c_idx, g, a)


def chip_exchange(p):
    _, rh, C = p.shape

    def body(p_ref, b_ref, send_sems, recv_sems):
        x, y, c = _place()
        cps = []
        for j, (cx, cy) in enumerate(_other_chips(x, y)):
            cp = pltpu.make_async_remote_copy(
                src_ref=p_ref.at[2 * cx + cy], dst_ref=b_ref.at[j], send_sem=send_sems.at[j],
                recv_sem=recv_sems.at[j], device_id=(cx, cy, c), device_id_type=MESH)
            cp.start()
            cps.append(cp)
        for cp in cps:
            cp.wait()

    return pl.pallas_call(
        body, name="grad_chip_exchange", out_shape=jax.ShapeDtypeStruct((3, rh, C), p.dtype),
        in_specs=[pl.BlockSpec(memory_space=pl.ANY)], out_specs=pl.BlockSpec(memory_space=pl.ANY),
        scratch_shapes=[pltpu.SemaphoreType.DMA((3,)), pltpu.SemaphoreType.DMA((3,))])(p)


def shard_total(p, b, s_idx, tr):
    _, rh, C = p.shape

    def body(s_ref, p_ref, b0, b1, b2, o_ref):
        o_ref[...] = ((p_ref[...].astype(F32) + b0[...].astype(F32)) + b1[...].astype(F32)
                      ) + b2[...].astype(F32)

    def bspec(j):
        return pl.BlockSpec((None, tr, C), lambda r, sr: (j, r, 0))

    return pl.pallas_call(
        body, name="grad_shard_total",
        grid_spec=pltpu.PrefetchScalarGridSpec(
            num_scalar_prefetch=1, grid=(rh // tr,),
            in_specs=[pl.BlockSpec((None, tr, C), lambda r, sr: (sr[0], r, 0)),
                      bspec(0), bspec(1), bspec(2)],
            out_specs=pl.BlockSpec((tr, C), lambda r, sr: (r, 0))),
        out_shape=jax.ShapeDtypeStruct((rh, C), F32),
        compiler_params=_params(("parallel",)))(s_idx, p, b, b, b)


def sibling_share(f):
    rh, C = f.shape

    def body(f_ref, out_ref, send_sem, recv_sem, local_sem):
        x, y, c = _place()
        mine = pltpu.make_async_copy(f_ref, out_ref.at[c], local_sem)
        mine.start()
        cp = pltpu.make_async_remote_copy(src_ref=f_ref, dst_ref=out_ref.at[c], send_sem=send_sem,
                                          recv_sem=recv_sem, device_id=(x, y, 1 - c),
                                          device_id_type=MESH)
        cp.start()
        cp.wait_send()
        pltpu.make_async_remote_copy(src_ref=f_ref, dst_ref=out_ref.at[1 - c], send_sem=send_sem,
                                     recv_sem=recv_sem, device_id=(x, y, 1 - c),
                                     device_id_type=MESH).wait_recv()
        mine.wait()

    return pl.pallas_call(
        body, name="grad_sibling_share", out_shape=jax.ShapeDtypeStruct((2, rh, C), f.dtype),
        in_specs=[pl.BlockSpec(memory_space=pl.ANY)], out_specs=pl.BlockSpec(memory_space=pl.ANY),
        scratch_shapes=[pltpu.SemaphoreType.DMA, pltpu.SemaphoreType.DMA, pltpu.SemaphoreType.DMA])(f)


def _rot_cols(w):
    h = w.shape[-1] // 2
    return jnp.concatenate([-w[..., h:], w[..., :h]], axis=-1)


def _slots(parts, lead, n_h):
    width = sum(p.shape[-1] for p in parts)
    pad = jnp.zeros(lead + (n_h, SLOT - width), parts[0].dtype)
    return jnp.concatenate(parts + [pad], axis=-1).reshape(lead + (n_h * SLOT,))


def layout_weights(w_in, w_uq, w_ukv, w_o_attn, dims):
    QL, KVL, C, D, n_h = dims
    o = 0
    w_ql, o = w_in[..., o:o + QL], o + QL
    w_kvl, o = w_in[..., o:o + KVL], o + KVL
    w_kr, o = w_in[..., o:o + ROPE], o + ROPE
    w_glu_a, o = w_in[..., o:o + C], o + C
    w_glu_b, o = w_in[..., o:o + C], o + C
    w_ga, o = w_in[..., o:o + D], o + D
    w_gb = w_in[..., o:o + D]
    z = lambda n: jnp.zeros(w_kr.shape[:-1] + (n,), w_kr.dtype)
    kr_a = jnp.concatenate([z(NOPE), w_kr, z(SLOT - QK_DIM)], axis=-1)
    kr_b = jnp.concatenate([z(NOPE), _rot_cols(w_kr), z(SLOT - QK_DIM)], axis=-1)
    lead = w_uq.shape[:-1]
    q = w_uq.reshape(lead + (n_h, QK_DIM))
    zq = jnp.zeros(lead + (n_h, NOPE), w_uq.dtype)
    wq_a = _slots([q[..., :NOPE], q[..., NOPE:]], lead, n_h)
    wq_b = _slots([zq, _rot_cols(q[..., NOPE:])], lead, n_h)
    lead = w_ukv.shape[:-1]
    kv = w_ukv.reshape(lead + (n_h, NOPE + VDIM))
    w_kn = _slots([kv[..., :NOPE]], lead, n_h)
    w_v = _slots([kv[..., NOPE:]], lead, n_h)
    lead = w_o_attn.shape[:-2]
    wo = w_o_attn.reshape(lead + (n_h, VDIM, D))
    wo = jnp.concatenate([wo, jnp.zeros(lead + (n_h, SLOT - VDIM, D), wo.dtype)], axis=-2)
    w_o = wo.reshape(lead + (n_h * SLOT, D))
    return dict(ql=w_ql, kvl=w_kvl, kr_a=kr_a, kr_b=kr_b, glu_a=w_glu_a, glu_b=w_glu_b,
                ga=w_ga, gb=w_gb, q_a=wq_a, q_b=wq_b, kn=w_kn, v=w_v, o=w_o)


GATHERED = ("w_in", "w_uq", "w_ukv", "w_o_attn", "w_pw2", "w_out", "w_gu", "w_down")
ROW_SHARDED = ("w_out", "w_down")
PACK_COLS = 1024
PACK_ROWS = 256


def _pack_len(shards):
    n = sum(int(np.prod(shards[k].shape)) for k in GATHERED)
    unit = 2 * PACK_ROWS * PACK_COLS
    return n, -(-n // unit) * unit


def pack_shards(shards, dtype):
    n, padded = _pack_len(shards)
    flat = jnp.concatenate([shards[k].astype(dtype).reshape(-1) for k in GATHERED]
                           + [jnp.zeros((padded - n,), dtype)])
    return flat.reshape(2, padded // PACK_COLS // 2, PACK_COLS)


def unpack_full(buf, shards):
    out, off = {}, 0
    for k in GATHERED:
        shp = shards[k].shape
        n = int(np.prod(shp))
        piece = buf[:, off:off + n].reshape((N_CHIPS,) + shp)
        off += n
        out[k] = jnp.concatenate([piece[s] for s in range(N_CHIPS)],
                                 axis=1 if k in ROW_SHARDED else 2)
    return out


def pack_full_grads(grads, shards):
    n, padded = _pack_len(shards)
    rows = []
    for s in range(N_CHIPS):
        parts = []
        for k in GATHERED:
            shp = shards[k].shape
            if k in ROW_SHARDED:
                parts.append(grads[k][:, s * shp[1]:(s + 1) * shp[1], :].reshape(-1))
            else:
                parts.append(grads[k][:, :, s * shp[2]:(s + 1) * shp[2]].reshape(-1))
        parts.append(jnp.zeros((padded - n,), F32))
        rows.append(jnp.concatenate(parts))
    return jnp.stack(rows)


def unpack_shard(flat, shards):
    out, off = {}, 0
    for k in GATHERED:
        shp = shards[k].shape
        n = int(np.prod(shp))
        out[k] = flat[off:off + n].reshape(shp)
        off += n
    return out


def _pad_rows8(flat):
    n = flat.shape[0]
    w = -(-n // (SUB * LANES)) * LANES
    return jnp.concatenate([flat, jnp.zeros((SUB * w - n,), flat.dtype)]).reshape(SUB, w)


def kernel(x, c, positions, w_ada, b_ada, g_mix, w_in, g_q, w_uq, g_kv, w_ukv, w_o_attn, w_dw, b_dw, g_cn, b_cn, w_pw2, w_out, g_ffn, w_gu, w_down, g_final, loss_target, m_w_ada, m_b_ada, m_g_mix, m_w_in, m_g_q, m_w_uq, m_g_kv, m_w_ukv, m_w_o_attn, m_w_dw, m_b_dw, m_g_cn, m_b_cn, m_w_pw2, m_w_out, m_g_ffn, m_w_gu, m_w_down, m_g_final, v_w_ada, v_b_ada, v_g_mix, v_w_in, v_g_q, v_w_uq, v_g_kv, v_w_ukv, v_w_o_attn, v_w_dw, v_b_dw, v_g_cn, v_b_cn, v_w_pw2, v_w_out, v_g_ffn, v_w_gu, v_w_down, v_g_final):
    weights = dict(w_ada=w_ada, b_ada=b_ada, g_mix=g_mix, w_in=w_in, g_q=g_q, w_uq=w_uq, g_kv=g_kv,
                   w_ukv=w_ukv, w_o_attn=w_o_attn, w_dw=w_dw, b_dw=b_dw, g_cn=g_cn, b_cn=b_cn,
                   w_pw2=w_pw2, w_out=w_out, g_ffn=g_ffn, w_gu=w_gu, w_down=w_down, g_final=g_final)
    mom = dict(w_ada=m_w_ada, b_ada=m_b_ada, g_mix=m_g_mix, w_in=m_w_in, g_q=m_g_q, w_uq=m_w_uq,
               g_kv=m_g_kv, w_ukv=m_w_ukv, w_o_attn=m_w_o_attn, w_dw=m_w_dw, b_dw=m_b_dw,
               g_cn=m_g_cn, b_cn=m_b_cn, w_pw2=m_w_pw2, w_out=m_w_out, g_ffn=m_g_ffn, w_gu=m_w_gu,
               w_down=m_w_down, g_final=m_g_final)
    var = dict(w_ada=v_w_ada, b_ada=v_b_ada, g_mix=v_g_mix, w_in=v_w_in, g_q=v_g_q, w_uq=v_w_uq,
               g_kv=v_g_kv, w_ukv=v_w_ukv, w_o_attn=v_w_o_attn, w_dw=v_w_dw, b_dw=v_b_dw,
               g_cn=v_g_cn, b_cn=v_b_cn, w_pw2=v_w_pw2, w_out=v_w_out, g_ffn=v_g_ffn, w_gu=v_w_gu,
               w_down=v_w_down, g_final=v_g_final)
    order = list(weights)

    n_e, S, D = x.shape
    T = n_e * S
    n_l = w_in.shape[0]
    QL, KVL, C = g_q.shape[1], g_kv.shape[1], g_cn.shape[1]
    n_h = w_uq.shape[2] * N_CHIPS // QK_DIM
    F = w_gu.shape[2] * N_CHIPS // 2
    B = n_e * N_DEV
    dims = (QL, KVL, C, D, n_h)
    scale = QK_DIM ** -0.5
    tm = _tile(S, 512, HALO)
    blk = _tile(S, 512, LANES)
    mx, my, mc = _place()
    dev = 4 * mx + 2 * my + mc
    chip = 2 * mx + my
    c_idx = jnp.reshape(mc, (1,)).astype(jnp.int32)
    s_idx = jnp.reshape(chip, (1,)).astype(jnp.int32)

    xt = x.reshape(T, D)
    tgt = loss_target.reshape(T, D)

    shards = {k: weights[k] for k in GATHERED}
    gathered = weight_allgather(pack_shards(shards, BF16))
    full = unpack_full(gathered.reshape(N_CHIPS, -1), shards)

    ns_ada = w_ada.shape[2]
    n_c, n_wdw = n_e * D, int(np.prod(w_dw.shape))
    small_all = allgather8("gather_c_wdw", _pad_rows8(
        jnp.concatenate([c.reshape(-1), w_dw.reshape(-1)]))).reshape(N_DEV, -1)
    c_all = small_all[:, :n_c].reshape(B, D)
    wdw_parts = small_all[:, n_c:n_c + n_wdw].reshape((N_DEV,) + w_dw.shape)
    w_dw_full = jnp.concatenate([wdw_parts[2 * s] for s in range(N_CHIPS)], axis=2)
    w_dw_pad = jnp.concatenate([w_dw_full, jnp.zeros((n_l, HALO - CONV_W, C), F32)], axis=1)

    c_act = silu_small(c_all)
    b_ada_mine = lax.dynamic_slice_in_dim(b_ada, chip * ns_ada, ns_ada, axis=1)
    mod_part = ada_fwd(c_act, w_ada, b_ada_mine)
    mod_all = allgather8("gather_mod", mod_part.reshape(n_l * B, ns_ada)).reshape(
        N_DEV, n_l, B, ns_ada)
    mod_full = jnp.concatenate([mod_all[2 * s] for s in range(N_CHIPS)], axis=2)
    mod_mine = lax.dynamic_slice_in_dim(mod_full, dev * n_e, n_e, axis=1)
    mods = mod_mine.reshape(n_l, n_e, N_MOD, 1, D)

    lay = layout_weights(full["w_in"], full["w_uq"], full["w_ukv"], full["w_o_attn"], dims)
    w_g, w_u = full["w_gu"][..., :F], full["w_gu"][..., F:]
    tr_ = lambda a: jnp.swapaxes(a, -1, -2)
    layT = {k: tr_(v) for k, v in lay.items()}
    w_pw2T, w_outT, w_gT, w_uT, w_downT = (tr_(full["w_pw2"]), tr_(full["w_out"]), tr_(w_g),
                                           tr_(w_u), tr_(full["w_down"]))

    lane = np.arange(LANES)
    in_rope = (lane >= NOPE) & (lane < QK_DIM)
    inv_freq = ROPE_THETA ** (-np.arange(0, ROPE, 2, dtype=np.float32) / ROPE)
    invf = np.where(in_rope, inv_freq[(lane - NOPE) % (ROPE // 2)], 0.0).astype(np.float32)
    cos_q, sin_q, cos_k = rope_tables(
        positions.astype(F32).reshape(T, 1), jnp.asarray(invf).reshape(1, LANES),
        jnp.asarray((lane < NOPE).astype(np.float32)).reshape(1, LANES),
        jnp.asarray(in_rope.astype(np.float32)).reshape(1, LANES), tm)

    def rope_epi(accs, ex):
        n = accs[0].shape[1]
        return (accs[0] * _lanes(ex[0], n) + accs[1] * _lanes(ex[1], n),)

    def rms_epi(accs, ex):
        a = accs[0]
        return a, a * _rstd(a) * ex[0]

    def modnorm(name, xin, g, sc, sh):
        def fn(xv, gv, scv, shv):
            return (xv * _rstd(xv) * gv * (1.0 + scv) + shv,)
        return rowwise(name, [('tile', xin), ('vec', g), ('exvec', sc), ('exvec', sh)],
                       [('tile', D, BF16)], fn, tm, S)[0]

    def modnorm_bwd(name, dh, xin, dres, g, sc):
        def fn(dhv, xv, drv, gv, scv):
            rstd = _rstd(xv)
            xhat = xv * rstd
            dx = _norm_bwd(dhv * gv * (1.0 + scv), xhat, rstd) + drv
            return dx, dhv, dhv * xhat * gv, dhv * xhat * (1.0 + scv)
        return rowwise(name, [('tile', dh), ('tile', xin), ('tile', dres), ('vec', g), ('exvec', sc)],
                       [('tile', D, F32), ('exacc', D), ('exacc', D), ('acc', D)], fn, tm, S)

    def rms_bwd(name, dy, xin, g, width):
        def fn(dyv, xv, gv):
            rstd = _rstd(xv)
            xhat = xv * rstd
            return _norm_bwd(dyv * gv, xhat, rstd), dyv * xhat
        return rowwise(name, [('tile', dy), ('tile', xin), ('vec', g)],
                       [('tile', width, BF16), ('acc', width)], fn, tm, S)

    def gate_bwd(name, dxo, branch, gt):
        def fn(dv, bv, gv):
            return dv * gv, dv * bv
        return rowwise(name, [('tile', dxo), ('tile', branch), ('exvec', gt)],
                       [('tile', D, BF16), ('exacc', D)], fn, tm, S)

    tn_d = _tile(D, 512, LANES)
    tn_f = _tile(F, 1536, LANES)
    tn_s = _tile(n_h * SLOT, 512, LANES)
    row = lambda a: a.reshape(1, -1)

    saved = []
    xc = xt
    for l in range(n_l):
        W = {k: v[l] for k, v in lay.items()}
        sh1, sc1, gt1, sh2, sc2, gt2 = [mods[l, :, k] for k in range(N_MOD)]
        h1 = modnorm("modnorm_mix", xc, row(g_mix[l]), sc1, sh1)
        ga, gb = mm_fused("proj_gates", [h1], [(0, W["ga"]), (0, W["gb"])], [], lambda a, e: a,
                          [F32, F32], tm, tn_d, S)
        cu, glu_a, glu_b = mm_fused(
            "proj_glu", [h1], [(0, W["glu_a"]), (0, W["glu_b"])], [],
            lambda a, e: (a[0] * _sig(a[1]), a[0], a[1]), [F32, F32, F32], tm, C, S)
        q_lat, qn = mm_fused("proj_q_lat", [h1], [(0, W["ql"])], [('vec', row(g_q[l]))], rms_epi,
                             [F32, BF16], tm, QL, S)
        kv_lat, kvn = mm_fused("proj_kv_lat", [h1], [(0, W["kvl"])], [('vec', row(g_kv[l]))],
                               rms_epi, [F32, BF16], tm, KVL, S)
        kr = mm_fused("proj_k_rope", [h1], [(0, W["kr_a"]), (0, W["kr_b"])],
                      [('row128', cos_k), ('row128', sin_q)], rope_epi, [F32], tm, SLOT, S)[0]
        q_all = mm_fused("q_up", [qn], [(0, W["q_a"]), (0, W["q_b"])],
                         [('row128', cos_q), ('row128', sin_q)], rope_epi, [BF16], tm, tn_s, S)[0]
        k_all, v_all = mm_fused(
            "kv_up", [kvn], [(0, W["kn"]), (0, W["v"])], [('row128', kr)],
            lambda a, e: (a[0] + _lanes(e[0], a[0].shape[1]), a[1]), [BF16, BF16], tm, tn_s, S)
        o_all, lse = attn_fwd(q_all, k_all, v_all, n_e, S, n_h, blk, scale)
        u, s_act = conv_fwd(cu, w_dw_pad[l], row(b_dw[l]), row(g_cn[l]), row(b_cn[l]), tm, S)

        def merge_epi(a, e):
            return _sig(e[0]) * a[0] + _sig(e[1]) * a[1], a[0], a[1]
        y, ya, yc = mm_fused("merge", [o_all, s_act], [(0, W["o"]), (1, full["w_pw2"][l])],
                             [('tile', ga), ('tile', gb)], merge_epi, [BF16, F32, F32], tm, tn_d, S)
        x2, o_mix = mm_fused("mix_out", [y], [(0, full["w_out"][l])], [('tile', xc), ('exvec', gt1)],
                             lambda a, e: (e[0] + e[1] * a[0], a[0]), [F32, F32], tm, tn_d, S)
        h2 = modnorm("modnorm_ffn", x2, row(g_ffn[l]), sc2, sh2)

        def swiglu_epi(a, e):
            return a[0], a[1], a[0] * _sig(a[0]) * a[1]
        g_act, up, act = mm_fused("ffn_up", [h2], [(0, w_g[l]), (0, w_u[l])], [], swiglu_epi,
                                  [BF16, BF16, BF16], tm, tn_f, S)
        x3, dn = mm_fused("ffn_down", [act], [(0, full["w_down"][l])], [('tile', x2), ('exvec', gt2)],
                          lambda a, e: (e[0] + e[1] * a[0], a[0]), [F32, F32], tm, tn_d, S)
        saved.append(dict(x=xc, h1=h1, ga=ga, gb=gb, cu=cu, glu_a=glu_a, glu_b=glu_b, q_lat=q_lat,
                          qn=qn, kv_lat=kv_lat, kvn=kvn, q_all=q_all, k_all=k_all, v_all=v_all,
                          o_all=o_all, lse=lse, u=u, s_act=s_act, y=y, ya=ya, yc=yc, x2=x2,
                          o_mix=o_mix, h2=h2, g_act=g_act, up=up, act=act, dn=dn))
        xc = x3

    def loss_fn(xv, tv, gv):
        rstd = _rstd(xv)
        xhat = xv * rstd
        err = xhat * gv - tv
        dy = err * (1.0 / D)
        return _norm_bwd(dy * gv, xhat, rstd), err * err * (0.5 / D), dy * xhat
    dxc, loss_acc, dg_final = rowwise("loss_head", [('tile', xc), ('tile', tgt), ('vec', row(g_final))],
                                      [('tile', D, F32), ('acc', D), ('acc', D)], loss_fn, tm, S)

    gfull = {k: [None] * n_l for k in GATHERED}
    g_wdw = [None] * n_l
    small_acc = {k: [None] * n_l for k in ("g_mix", "g_q", "g_kv", "b_dw", "g_cn", "b_cn", "g_ffn")}
    dmod_acc = [None] * n_l
    lay_T = jax.linear_transpose(
        lambda a, b, cc, d: layout_weights(a, b, cc, d, dims),
        *[jax.ShapeDtypeStruct(full[k].shape[1:], F32) for k in ("w_in", "w_uq", "w_ukv", "w_o_attn")])

    for l in reversed(range(n_l)):
        sv = saved[l]
        W = {k: v[l] for k, v in layT.items()}
        sh1, sc1, gt1, sh2, sc2, gt2 = [mods[l, :, k] for k in range(N_MOD)]
        ddn, dgt2 = gate_bwd("ffn_gate_bwd", dxc, sv["dn"], gt2)

        def swiglu_bwd_epi(a, e):
            gv, uv = e[0].astype(F32), e[1].astype(F32)
            sg = _sig(gv)
            return a[0] * uv * sg * (1.0 + gv * (1.0 - sg)), a[0] * gv * sg
        dg_act, dup = mm_fused("ffn_down_bwd", [ddn], [(0, w_downT[l])],
                               [('tile', sv["g_act"]), ('tile', sv["up"])], swiglu_bwd_epi,
                               [BF16, BF16], tm, tn_f, S)
        gfull["w_down"][l] = mm_tn("ffn_down_dw", sv["act"], ddn)
        gfull["w_gu"][l] = jnp.concatenate([mm_tn("ffn_gate_dw", sv["h2"], dg_act),
                                            mm_tn("ffn_up_dw", sv["h2"], dup)], axis=1)
        dh2 = mm_fused("ffn_up_bwd", [dg_act, dup], [(0, w_gT[l]), (1, w_uT[l])], [],
                       lambda a, e: (a[0] + a[1],), [F32], tm, tn_d, S)[0]
        dx2, dsh2, dsc2, dg_ffn = modnorm_bwd("modnorm_ffn_bwd", dh2, sv["x2"], dxc,
                                              row(g_ffn[l]), sc2)
        ddo, dgt1 = gate_bwd("mix_gate_bwd", dx2, sv["o_mix"], gt1)

        def merge_bwd_epi(a, e):
            sa, sb = _sig(e[0]), _sig(e[1])
            dy = a[0]
            return dy * sa, dy * sb, dy * e[2] * sa * (1.0 - sa), dy * e[3] * sb * (1.0 - sb)
        dya, dyc, dga, dgb = mm_fused(
            "mix_out_bwd", [ddo], [(0, w_outT[l])],
            [('tile', sv["ga"]), ('tile', sv["gb"]), ('tile', sv["ya"]), ('tile', sv["yc"])],
            merge_bwd_epi, [BF16] * 4, tm, tn_d, S)
        gfull["w_out"][l] = mm_tn("mix_out_dw", sv["y"], ddo)
        do_all = mm_fused("attn_out_bwd", [dya], [(0, W["o"])], [], lambda a, e: a, [BF16],
                          tm, tn_s, S)[0]
        d_wo = mm_tn("attn_out_dw", sv["o_all"], dya)
        ds_act = mm_fused("conv_out_bwd", [dyc], [(0, w_pw2T[l])], [], lambda a, e: a, [F32],
                          tm, C, S)[0]
        gfull["w_pw2"][l] = mm_tn("conv_out_dw", sv["s_act"], dyc)

        def ln_silu_bwd(dsv, uv, gv, bv):
            xhat, rstd = _layer_norm_parts(uv)
            ln = xhat * gv + bv
            sg = _sig(ln)
            dln = dsv * sg * (1.0 + ln * (1.0 - sg))
            dxhat = dln * gv
            du_ = rstd * (dxhat - jnp.mean(dxhat, axis=-1, keepdims=True)
                          - xhat * jnp.mean(dxhat * xhat, axis=-1, keepdims=True))
            return du_, dln * xhat, dln
        du, dg_cn, db_cn = rowwise(
            "conv_norm_bwd", [('tile', ds_act), ('tile', sv["u"]), ('vec', row(g_cn[l])),
                              ('vec', row(b_cn[l]))],
            [('tile', C, F32), ('acc', C), ('acc', C)], ln_silu_bwd, tm, S)
        dglu_a, dglu_b, dw_acc, db_dw = conv_bwd(du, sv["cu"], sv["glu_a"], sv["glu_b"],
                                                 w_dw_pad[l], tm, S)
        g_wdw[l] = group_sum("conv_dw_rows", dw_acc, SUB)[:CONV_W]
        dk_all, dv_all = attn_bwd_kv(sv["q_all"], sv["k_all"], sv["v_all"], sv["o_all"], do_all,
                                     sv["lse"], n_e, S, n_h, blk, scale)
        gq_a, gq_b = attn_bwd_q(sv["q_all"], sv["k_all"], sv["v_all"], sv["o_all"], do_all,
                                sv["lse"], cos_q, sin_q, n_e, S, n_h, blk, scale)
        dqn = mm_fused("q_up_bwd", [gq_a, gq_b], [(0, W["q_a"]), (1, W["q_b"])], [],
                       lambda a, e: (a[0] + a[1],), [F32], tm, QL, S)[0]
        d_wqa = mm_tn("q_up_dw_a", sv["qn"], gq_a)
        d_wqb = mm_tn("q_up_dw_b", sv["qn"], gq_b)
        dq_lat, dg_q = rms_bwd("q_norm_bwd", dqn, sv["q_lat"], row(g_q[l]), QL)

        def k_split(dkv, ckv, skv):
            tot = dkv[:, :SLOT]
            for h in range(1, n_h):
                tot = tot + dkv[:, h * SLOT:(h + 1) * SLOT]
            return dkv, tot * ckv, tot * skv
        dk_b, dkr_a, dkr_b = rowwise("k_rope_bwd", [('tile', dk_all), ('tile', cos_k), ('tile', sin_q)],
                                     [('tile', n_h * SLOT, BF16), ('tile', SLOT, BF16),
                                      ('tile', SLOT, BF16)], k_split, tm, S)
        dkvn = mm_fused("kv_up_bwd", [dk_b, dv_all], [(0, W["kn"]), (1, W["v"])], [],
                        lambda a, e: (a[0] + a[1],), [F32], tm, KVL, S)[0]
        d_wkn = mm_tn("kv_up_dw_k", sv["kvn"], dk_b)
        d_wv = mm_tn("kv_up_dw_v", sv["kvn"], dv_all)
        dkv_lat, dg_kv = rms_bwd("kv_norm_bwd", dkvn, sv["kv_lat"], row(g_kv[l]), KVL)
        segs = [("ga", dga), ("gb", dgb), ("glu_a", dglu_a), ("glu_b", dglu_b), ("ql", dq_lat),
                ("kvl", dkv_lat), ("kr_a", dkr_a), ("kr_b", dkr_b)]
        dh1 = mm_fused("proj_bwd", [g for _, g in segs], [(k, W[nm]) for k, (nm, _) in enumerate(segs)],
                       [], lambda a, e: (functools.reduce(lambda p, q: p + q, a),), [F32],
                       tm, tn_d, S)[0]
        d_lay = {nm: mm_tn("proj_dw_" + nm, sv["h1"], g) for nm, g in segs}
        d_lay.update(q_a=d_wqa, q_b=d_wqb, kn=d_wkn, v=d_wv, o=d_wo)
        (gfull["w_in"][l], gfull["w_uq"][l], gfull["w_ukv"][l],
         gfull["w_o_attn"][l]) = lay_T({k: d_lay[k] for k in lay})
        dxc, dsh1, dsc1, dg_mix = modnorm_bwd("modnorm_mix_bwd", dh1, sv["x"], dx2,
                                              row(g_mix[l]), sc1)
        dmod_acc[l] = [dsh1, dsc1, dgt1, dsh2, dsc2, dgt2]
        for k, a in (("g_mix", dg_mix), ("g_q", dg_q), ("g_kv", dg_kv), ("b_dw", db_dw),
                     ("g_cn", dg_cn), ("b_cn", db_cn), ("g_ffn", dg_ffn)):
            small_acc[k][l] = a

    grad_x = dxc.reshape(n_e, S, D)

    dmod_rows = jnp.concatenate([a for l in range(n_l) for a in dmod_acc[l]], axis=0)
    dmod_own = group_sum("dmod_rows", dmod_rows, SUB).reshape(n_l, N_MOD, n_e, D)
    dmod_own = jnp.transpose(dmod_own, (0, 2, 1, 3)).reshape(n_l * n_e, N_MOD * D)
    dmod_all = allgather8("gather_dmod", dmod_own).reshape(N_DEV, n_l, n_e, N_MOD * D)
    dmod_all = jnp.transpose(dmod_all, (1, 0, 2, 3)).reshape(n_l, B, N_MOD * D)
    dmod_mine = lax.dynamic_slice_in_dim(dmod_all, chip * ns_ada, ns_ada, axis=2)
    grad_w_ada = ada_bwd_w(jnp.transpose(c_act), dmod_mine)
    grad_b_ada = group_sum("grad_b_ada", dmod_all.reshape(n_l * B, N_MOD * D), B)

    gstack = {k: jnp.stack(gfull[k]) for k in GATHERED}
    _, padded = _pack_len(shards)
    rh = padded // PACK_COLS // 2
    gpk = pack_full_grads(gstack, shards).reshape(N_CHIPS, 2, rh, PACK_COLS)
    gpk = jnp.transpose(gpk, (1, 0, 2, 3))
    part = chip_partial(gpk, sibling_swap(gpk), c_idx, PACK_ROWS)
    half = shard_total(part, chip_exchange(part), s_idx, PACK_ROWS)
    red = unpack_shard(sibling_share(half).reshape(-1), shards)

    wdw_full_g = jnp.stack(g_wdw)
    pieces = [loss_acc, dg_final] + [small_acc[k][l] for k in small_acc for l in range(n_l)]
    widths = [p.shape[1] for p in pieces]
    n_acc = sum(widths)
    wdw_blk = _pad_rows8(wdw_full_g.reshape(-1))
    wdw_w = wdw_blk.shape[1]
    gathered_small = allgather8("gather_small_grads", jnp.concatenate(pieces + [wdw_blk], axis=1))
    acc_sum = group_sum("small_total", gathered_small[:, :, :n_acc].reshape(N_DEV * SUB, n_acc),
                        N_DEV * SUB)
    wdw_sum = group_sum("wdw_total", gathered_small[:, :, n_acc:].reshape(N_DEV, SUB * wdw_w), N_DEV)
    offs = np.cumsum([0] + widths)
    take = lambda i: acc_sum[:, offs[i]:offs[i + 1]]
    loss = lane_total("loss_total", take(0))[0, 0]
    g_small = {"g_final": take(1).reshape(-1)}
    i = 2
    for k in small_acc:
        g_small[k] = jnp.concatenate([take(i + l) for l in range(n_l)], axis=0)
        i += n_l
    wdw_total = wdw_sum.reshape(-1)[:n_wdw * N_CHIPS].reshape(wdw_full_g.shape)
    grad_w_dw = lax.dynamic_slice_in_dim(wdw_total, chip * w_dw.shape[2], w_dw.shape[2], axis=2)

    grads = dict(w_ada=grad_w_ada, b_ada=grad_b_ada, w_dw=grad_w_dw, **g_small, **red)

    deltas, new_m, new_v = {}, {}, {}
    for k in order:
        shp = weights[k].shape
        two = (1, shp[0]) if len(shp) == 1 else (int(np.prod(shp[:-1])), shp[-1])
        d, nm, nv = adamw("adamw_" + k, weights[k].reshape(two), grads[k].reshape(two),
                          mom[k].reshape(two), var[k].reshape(two))
        deltas[k], new_m[k], new_v[k] = d.reshape(shp), nm.reshape(shp), nv.reshape(shp)
        grads[k] = grads[k].reshape(shp)

    return (loss, grad_x, *[grads[k] for k in order], *[deltas[k] for k in order],
            *[new_m[k] for k in order], *[new_v[k] for k in order])
```

```python
import functools

import numpy as np
import jax
import jax.numpy as jnp
from jax import lax
from jax.experimental import pallas as pl
from jax.experimental.pallas import tpu as pltpu

F32 = jnp.float32
BF16 = jnp.bfloat16
MESH = pl.DeviceIdType.MESH

EPS = 1e-6
NEG_INF = -1e30
NOPE, ROPE, VDIM = 64, 32, 64
QK_DIM = NOPE + ROPE
SLOT = 128
CONV_W = 31
HALO = 32
N_MOD = 6
ROPE_THETA = 10000.0
N_CHIPS = 4
N_DEV = 8
SUB = 8
LANES = 128
VMEM_LIMIT = 56 * 1024 * 1024

ADAM_LR, ADAM_B1, ADAM_B2, ADAM_EPS, ADAM_WD, ADAM_STEP = 0.001, 0.9, 0.999, 1e-08, 0.01, 10


def _tile(n, cap, mult):
    best = None
    for d in range(mult, min(n, cap) + 1, mult):
        if n % d == 0:
            best = d
    return best if best is not None else n


def _params(sem):
    return pltpu.CompilerParams(dimension_semantics=sem, vmem_limit_bytes=VMEM_LIMIT)


def _sig(x):
    return 1.0 / (1.0 + jnp.exp(-x))


def _sum8(x):
    r, w = x.shape
    return jnp.sum(x.reshape(r // SUB, SUB, w), axis=0)


def _lanes(v, n):
    return v if n == v.shape[1] else jnp.tile(v, (1, n // v.shape[1]))


def _rstd(x):
    return lax.rsqrt(jnp.mean(x * x, axis=-1, keepdims=True) + EPS)


def _norm_bwd(dxhat, xhat, rstd):
    return rstd * (dxhat - xhat * jnp.mean(dxhat * xhat, axis=-1, keepdims=True))


def mm_fused(name, As, pairs, extras, epilogue, out_dtypes, tm, tn, S, N=None):
    T = As[0].shape[0]
    pairs = [(p[0], p[1], p[2] if len(p) > 2 else False, p[3] if len(p) > 3 else 0) for p in pairs]
    if N is None:
        N = pairs[0][1].shape[0] if pairs[0][2] else pairs[0][1].shape[1]
    nex = S // tm
    in_specs, args = [], []
    for a in As:
        in_specs.append(pl.BlockSpec((tm, a.shape[1]), lambda i, j: (i, 0)))
        args.append(a)
    for ai, b, trans, off in pairs:
        kdim = As[ai].shape[1]
        if trans:
            in_specs.append(pl.BlockSpec((tn, kdim), lambda i, j, off=off: (j, off)))
        else:
            in_specs.append(pl.BlockSpec((kdim, tn), lambda i, j, off=off: (0, j + off)))
        args.append(b)
    for kind, arr in extras:
        if kind == 'tile':
            in_specs.append(pl.BlockSpec((tm, tn), lambda i, j: (i, j)))
        elif kind == 'row128':
            in_specs.append(pl.BlockSpec((tm, LANES), lambda i, j: (i, 0)))
        elif kind == 'vec':
            in_specs.append(pl.BlockSpec((1, tn), lambda i, j: (0, j)))
        else:
            in_specs.append(pl.BlockSpec((None, 1, tn), lambda i, j: (i // nex, 0, j)))
        args.append(arr)
    n_a, n_p, n_e = len(As), len(pairs), len(extras)

    def body(*refs):
        a_refs, b_refs = refs[:n_a], refs[n_a:n_a + n_p]
        e_refs, o_refs = refs[n_a + n_p:n_a + n_p + n_e], refs[n_a + n_p + n_e:]
        accs = [lax.dot_general(a_refs[ai][...], b_refs[k][...], NT if trans else NN,
                                preferred_element_type=F32)
                for k, (ai, _, trans, _) in enumerate(pairs)]
        outs = epilogue(accs, [r[...] for r in e_refs])
        for o_ref, o in zip(o_refs, outs):
            o_ref[...] = o.astype(o_ref.dtype)

    return pl.pallas_call(
        body, name=name, grid=(T // tm, N // tn), in_specs=in_specs,
        out_specs=[pl.BlockSpec((tm, tn), lambda i, j: (i, j)) for _ in out_dtypes],
        out_shape=[jax.ShapeDtypeStruct((T, N), dt) for dt in out_dtypes],
        compiler_params=_params(("parallel", "parallel")))(*args)


def mm_tn(name, A, G):
    T, K = A.shape
    N = G.shape[1]
    tt = _tile(T, 512, 16)
    tk = _tile(K, 1536, LANES)
    tn = _tile(N, 1536, LANES)

    def body(a_ref, g_ref, o_ref):
        part = lax.dot_general(a_ref[...], g_ref[...], (((0,), (0,)), ((), ())),
                               preferred_element_type=F32)

        @pl.when(pl.program_id(2) == 0)
        def _():
            o_ref[...] = part

        @pl.when(pl.program_id(2) > 0)
        def _():
            o_ref[...] += part

    return pl.pallas_call(
        body, name=name, grid=(K // tk, N // tn, T // tt),
        in_specs=[pl.BlockSpec((tt, tk), lambda k, n, t: (t, k)),
                  pl.BlockSpec((tt, tn), lambda k, n, t: (t, n))],
        out_specs=pl.BlockSpec((tk, tn), lambda k, n, t: (k, n)),
        out_shape=jax.ShapeDtypeStruct((K, N), F32),
        compiler_params=_params(("parallel", "parallel", "arbitrary")))(A, G)


def rowwise(name, ins, outs, fn, tm, S):
    T = next(a.shape[0] for k, a in ins if k == 'tile')
    nex = S // tm
    n_ex = T // S
    in_specs, args = [], []
    for kind, arr in ins:
        if kind == 'tile':
            in_specs.append(pl.BlockSpec((tm, arr.shape[1]), lambda i: (i, 0)))
        elif kind == 'vec':
            in_specs.append(pl.BlockSpec((1, arr.shape[1]), lambda i: (0, 0)))
        else:
            in_specs.append(pl.BlockSpec((None, 1, arr.shape[2]), lambda i: (i // nex, 0, 0)))
        args.append(arr)
    out_specs, out_shape = [], []
    for o in outs:
        if o[0] == 'tile':
            out_specs.append(pl.BlockSpec((tm, o[1]), lambda i: (i, 0)))
            out_shape.append(jax.ShapeDtypeStruct((T, o[1]), o[2]))
        elif o[0] == 'acc':
            out_specs.append(pl.BlockSpec((SUB, o[1]), lambda i: (0, 0)))
            out_shape.append(jax.ShapeDtypeStruct((SUB, o[1]), F32))
        else:
            out_specs.append(pl.BlockSpec((SUB, o[1]), lambda i: (i // nex, 0)))
            out_shape.append(jax.ShapeDtypeStruct((n_ex * SUB, o[1]), F32))
    n_in = len(ins)

    def body(*refs):
        i = pl.program_id(0)
        vals = fn(*[r[...] for r in refs[:n_in]])
        for o, o_ref, v in zip(outs, refs[n_in:], vals):
            if o[0] == 'tile':
                o_ref[...] = v.astype(o_ref.dtype)
            else:
                part = _sum8(v)
                first = (i == 0) if o[0] == 'acc' else (i % nex == 0)

                @pl.when(first)
                def _(o_ref=o_ref, part=part):
                    o_ref[...] = part

                @pl.when(jnp.logical_not(first))
                def _(o_ref=o_ref, part=part):
                    o_ref[...] += part

    return pl.pallas_call(
        body, name=name, grid=(T // tm,), in_specs=in_specs, out_specs=out_specs,
        out_shape=out_shape, compiler_params=_params(("arbitrary",)))(*args)


NN = (((1,), (0,)), ((), ()))
NT = (((1,), (1,)), ((), ()))
TN = (((0,), (0,)), ((), ()))
LN2 = 0.6931471805599453
ATTN_TQ = 1024
ATTN_LB = 512


def attn_fwd(Q, K, V, n_b, S, n_h, tq, lb):
    T = n_b * S
    nq = S // tq
    ratio = tq // lb
    tk = lb

    def body(q_ref, k_ref, v_ref, o_ref, lse_ref, m_s, l_s, acc_s):
        i = pl.program_id(2)
        m_s[...] = jnp.full(m_s.shape, NEG_INF, F32)
        l_s[...] = jnp.zeros(l_s.shape, F32)
        acc_s[...] = jnp.zeros(acc_s.shape, F32)

        def kv_step(j, diag_off, rsplit):
            start = pl.multiple_of(j * tk, tk)
            k = k_ref[pl.ds(start, tk), :]
            v = v_ref[pl.ds(start, tk), :]
            rc = tq // rsplit
            for r in range(rsplit):
                if diag_off is not None and diag_off > r * rc + rc - 1:
                    continue
                rows = pl.ds(r * rc, rc)
                s = lax.dot_general(q_ref[rows, :], k, NT, preferred_element_type=F32)
                if diag_off is not None and diag_off + tk - 1 > r * rc:
                    rr = lax.broadcasted_iota(jnp.int32, s.shape, 0) + r * rc
                    cc = lax.broadcasted_iota(jnp.int32, s.shape, 1) + diag_off
                    s = jnp.where(rr >= cc, s, NEG_INF)
                m_prev = m_s[rows, :]
                m_new = jnp.maximum(m_prev, jnp.max(s, axis=1, keepdims=True))
                alpha = jnp.exp2(m_prev - m_new)
                p = jnp.exp2(s - _lanes(m_new, tk))
                l_s[rows, :] = alpha * l_s[rows, :] + jnp.sum(p, axis=1, keepdims=True)
                acc_s[rows, :] = alpha * acc_s[rows, :] + jnp.dot(p.astype(BF16), v,
                                                                  preferred_element_type=F32)
                m_s[rows, :] = m_new

        def below_diagonal(j, carry):
            kv_step(j, None, 1)
            return carry
        lax.fori_loop(0, i * ratio, below_diagonal, 0)
        for d in range(ratio):
            kv_step(i * ratio + d, d * tk, ratio)
        l = l_s[...]
        o_ref[...] = (acc_s[...] / l).astype(o_ref.dtype)
        lse = m_s[...] + jnp.log(l) * (1.0 / LN2)
        for u in range(ratio):
            lse_ref[u] = jnp.transpose(lse[u * lb:(u + 1) * lb, :])[:SUB, :]

    qmap = lambda b, h, i: (b * nq + i, h)
    kmap = lambda b, h, i: (b, h)
    return pl.pallas_call(
        body, name="attn_fwd", grid=(n_b, n_h, nq),
        in_specs=[pl.BlockSpec((tq, SLOT), qmap), pl.BlockSpec((S, SLOT), kmap),
                  pl.BlockSpec((S, SLOT), kmap)],
        out_specs=[pl.BlockSpec((tq, SLOT), qmap),
                   pl.BlockSpec((None, ratio, SUB, lb), lambda b, h, i: (b * n_h + h, i, 0, 0))],
        out_shape=[jax.ShapeDtypeStruct((T, n_h * SLOT), BF16),
                   jax.ShapeDtypeStruct((n_b * n_h, S // lb, SUB, lb), F32)],
        scratch_shapes=[pltpu.VMEM((tq, SLOT), F32)] * 3,
        compiler_params=_params(("parallel", "parallel", "arbitrary")))(Q, K, V)


def attn_bwd(Q, K, V, O, dO, LSE, cosq, sinq, n_b, S, n_h, blk, scale):
    T = n_b * S
    nb = S // blk

    def body(q_ref, k_ref, v_ref, o_ref, do_ref, lse_ref, cos_ref, sin_ref,
             dk_ref, dv_ref, ga_ref, gb_ref, dq_s, delta_s, dk_s, dv_s):
        j = pl.program_id(2)

        @pl.when(j == 0)
        def _():
            dq_s[...] = jnp.zeros(dq_s.shape, F32)
            for i in range(nb):
                rows = pl.ds(i * blk, blk)
                d = jnp.sum(do_ref[rows, :].astype(F32) * o_ref[rows, :].astype(F32),
                            axis=1, keepdims=True)
                delta_s[i] = jnp.transpose(jnp.broadcast_to(d, (blk, SLOT)))[:SUB, :]

        k = k_ref[...]
        v = v_ref[...]
        dk_s[...] = jnp.zeros(dk_s.shape, F32)
        dv_s[...] = jnp.zeros(dv_s.shape, F32)

        def q_step(i, masked):
            rows = pl.ds(pl.multiple_of(i * blk, blk), blk)
            q = q_ref[rows, :]
            do = do_ref[rows, :]
            st = lax.dot_general(k, q, NT, preferred_element_type=F32)
            if masked:
                kv_i = lax.broadcasted_iota(jnp.int32, st.shape, 0)
                q_i = lax.broadcasted_iota(jnp.int32, st.shape, 1)
                st = jnp.where(q_i >= kv_i, st, NEG_INF)
            pt = jnp.exp2(st - lse_ref[i][:1, :])
            dpt = lax.dot_general(v, do, NT, preferred_element_type=F32)
            dst = (pt * (dpt - delta_s[i][:1, :])).astype(BF16)
            dv_s[...] += jnp.dot(pt.astype(BF16), do, preferred_element_type=F32)
            dk_s[...] += jnp.dot(dst, q, preferred_element_type=F32)
            dq_s[rows, :] += lax.dot_general(dst, k, TN, preferred_element_type=F32)

        q_step(j, True)

        def above_diagonal(i, carry):
            q_step(i, False)
            return carry
        lax.fori_loop(j + 1, nb, above_diagonal, 0)
        dk_ref[...] = dk_s[...] * LN2
        dv_ref[...] = dv_s[...].astype(dv_ref.dtype)

        @pl.when(j == nb - 1)
        def _():
            dq = dq_s[...] * scale
            ga_ref[...] = (dq * cos_ref[...]).astype(ga_ref.dtype)
            gb_ref[...] = (dq * sin_ref[...]).astype(gb_ref.dtype)

    full = pl.BlockSpec((S, SLOT), lambda b, h, j: (b, h))
    kv = pl.BlockSpec((blk, SLOT), lambda b, h, j: (b * nb + j, h))
    tab = pl.BlockSpec((S, SLOT), lambda b, h, j: (b, 0))
    stat = pl.BlockSpec((None, nb, SUB, blk), lambda b, h, j: (b * n_h + h, 0, 0, 0))
    return pl.pallas_call(
        body, name="attn_bwd", grid=(n_b, n_h, nb),
        in_specs=[full, kv, kv, full, full, stat, tab, tab], out_specs=[kv, kv, full, full],
        out_shape=[jax.ShapeDtypeStruct((T, n_h * SLOT), F32)]
        + [jax.ShapeDtypeStruct((T, n_h * SLOT), BF16)] * 3,
        scratch_shapes=[pltpu.VMEM((S, SLOT), F32), pltpu.VMEM((nb, SUB, blk), F32),
                        pltpu.VMEM((blk, SLOT), F32), pltpu.VMEM((blk, SLOT), F32)],
        compiler_params=_params(("parallel", "parallel", "arbitrary")))(
            Q, K, V, O, dO, LSE, cosq, sinq)


def _layer_norm_parts(u):
    xc = u - jnp.mean(u, axis=-1, keepdims=True)
    rstd = lax.rsqrt(jnp.mean(xc * xc, axis=-1, keepdims=True) + EPS)
    return xc * rstd, rstd


def conv_fwd(cu, w, b_dw, g_cn, b_cn, tm, S):
    T, C = cu.shape
    nex, hb = S // tm, tm // HALO

    def body(cur_ref, prev_ref, w_ref, b_ref, g_ref, bc_ref, u_ref, s_ref, ext):
        first = pl.program_id(0) % nex == 0
        ext[pl.ds(0, HALO), :] = jnp.where(first, 0.0, prev_ref[...])
        ext[pl.ds(HALO, tm), :] = cur_ref[...]
        acc = jnp.zeros((tm, C), F32)
        for j in range(CONV_W):
            acc = acc + w_ref[pl.ds(j, 1), :] * ext[pl.ds(HALO - CONV_W + 1 + j, tm), :]
        u = acc + b_ref[...]
        ln = _layer_norm_parts(u)[0] * g_ref[...] + bc_ref[...]
        u_ref[...] = u
        s_ref[...] = (ln * _sig(ln)).astype(s_ref.dtype)

    vec = pl.BlockSpec((1, C), lambda i: (0, 0))
    return pl.pallas_call(
        body, name="conv_fwd", grid=(T // tm,),
        in_specs=[pl.BlockSpec((tm, C), lambda i: (i, 0)),
                  pl.BlockSpec((HALO, C), lambda i: (jnp.maximum(i * hb - 1, 0), 0)),
                  pl.BlockSpec((HALO, C), lambda i: (0, 0)), vec, vec, vec],
        out_specs=[pl.BlockSpec((tm, C), lambda i: (i, 0))] * 2,
        out_shape=[jax.ShapeDtypeStruct((T, C), F32), jax.ShapeDtypeStruct((T, C), BF16)],
        scratch_shapes=[pltpu.VMEM((HALO + tm, C), F32)],
        compiler_params=_params(("arbitrary",)))(cu, cu, w, b_dw, g_cn, b_cn)


def conv_bwd(du, cu, glu_a, glu_b, w, tm, S):
    T, C = du.shape
    nex, hb = S // tm, tm // HALO
    last_blk = T // HALO - 1

    def body(du_ref, nxt_ref, cu_ref, prev_ref, a_ref, b_ref, w_ref,
             da_ref, db_ref, dw_ref, dbias_ref, extd, extc):
        i = pl.program_id(0)
        first = i % nex == 0
        last = i % nex == nex - 1
        du_cur = du_ref[...]
        extd[pl.ds(0, tm), :] = du_cur
        extd[pl.ds(tm, HALO), :] = jnp.where(last, 0.0, nxt_ref[...])
        extc[pl.ds(0, HALO), :] = jnp.where(first, 0.0, prev_ref[...])
        extc[pl.ds(HALO, tm), :] = cu_ref[...]

        @pl.when(i == 0)
        def _():
            dw_ref[...] = jnp.zeros(dw_ref.shape, F32)
            dbias_ref[...] = jnp.zeros(dbias_ref.shape, F32)

        dcu = jnp.zeros((tm, C), F32)
        for j in range(CONV_W):
            dcu = dcu + w_ref[pl.ds(j, 1), :] * extd[pl.ds(CONV_W - 1 - j, tm), :]
            dw_ref[pl.ds(SUB * j, SUB), :] += _sum8(
                du_cur * extc[pl.ds(HALO - CONV_W + 1 + j, tm), :])
        dbias_ref[...] += _sum8(du_cur)
        sb = _sig(b_ref[...])
        da_ref[...] = (dcu * sb).astype(da_ref.dtype)
        db_ref[...] = (dcu * a_ref[...] * sb * (1.0 - sb)).astype(db_ref.dtype)

    cur = pl.BlockSpec((tm, C), lambda i: (i, 0))
    return pl.pallas_call(
        body, name="conv_bwd", grid=(T // tm,),
        in_specs=[cur, pl.BlockSpec((HALO, C), lambda i: (jnp.minimum((i + 1) * hb, last_blk), 0)),
                  cur, pl.BlockSpec((HALO, C), lambda i: (jnp.maximum(i * hb - 1, 0), 0)),
                  cur, cur, pl.BlockSpec((HALO, C), lambda i: (0, 0))],
        out_specs=[cur, cur, pl.BlockSpec((HALO * SUB, C), lambda i: (0, 0)),
                   pl.BlockSpec((SUB, C), lambda i: (0, 0))],
        out_shape=[jax.ShapeDtypeStruct((T, C), BF16), jax.ShapeDtypeStruct((T, C), BF16),
                   jax.ShapeDtypeStruct((HALO * SUB, C), F32), jax.ShapeDtypeStruct((SUB, C), F32)],
        scratch_shapes=[pltpu.VMEM((tm + HALO, C), F32), pltpu.VMEM((HALO + tm, C), F32)],
        compiler_params=_params(("arbitrary",)))(du, du, cu, cu, glu_a, glu_b, w)


def silu_small(c_all):
    def body(c_ref, o_ref):
        v = c_ref[...]
        o_ref[...] = v * _sig(v)
    return pl.pallas_call(body, name="silu_c", out_shape=jax.ShapeDtypeStruct(c_all.shape, F32))(c_all)


def ada_fwd(c_act, w_ada, b_ada):
    n_l, D, ns = w_ada.shape
    B = c_act.shape[0]
    tn = _tile(ns, 512, LANES)

    def body(c_ref, w_ref, b_ref, o_ref):
        o_ref[...] = jnp.dot(c_ref[...], w_ref[...], preferred_element_type=F32,
                             precision=lax.Precision.HIGHEST) + b_ref[...]

    return pl.pallas_call(
        body, name="ada_fwd", grid=(n_l, ns // tn),
        in_specs=[pl.BlockSpec((B, D), lambda l, j: (0, 0)),
                  pl.BlockSpec((None, D, tn), lambda l, j: (l, 0, j)),
                  pl.BlockSpec((None, 1, tn), lambda l, j: (l, 0, j))],
        out_specs=pl.BlockSpec((None, B, tn), lambda l, j: (l, 0, j)),
        out_shape=jax.ShapeDtypeStruct((n_l, B, ns), F32),
        compiler_params=_params(("parallel", "parallel")))(c_act, w_ada, b_ada.reshape(n_l, 1, ns))


def ada_bwd_w(c_act_t, dmod):
    D, B = c_act_t.shape
    n_l, _, ns = dmod.shape
    tn = _tile(ns, 512, LANES)

    def body(c_ref, d_ref, o_ref):
        o_ref[...] = jnp.dot(c_ref[...], d_ref[...], preferred_element_type=F32,
                             precision=lax.Precision.HIGHEST)

    return pl.pallas_call(
        body, name="ada_bwd_w", grid=(n_l, ns // tn),
        in_specs=[pl.BlockSpec((D, B), lambda l, j: (0, 0)),
                  pl.BlockSpec((None, B, tn), lambda l, j: (l, 0, j))],
        out_specs=pl.BlockSpec((None, D, tn), lambda l, j: (l, 0, j)),
        out_shape=jax.ShapeDtypeStruct((n_l, D, ns), F32),
        compiler_params=_params(("parallel", "parallel")))(c_act_t, dmod)


def group_sum(name, v, group):
    rows, W = v.shape
    n = rows // group

    def body(v_ref, o_ref):
        o_ref[...] = jnp.sum(v_ref[...].reshape(n, group, W), axis=1)

    return pl.pallas_call(body, name=name, out_shape=jax.ShapeDtypeStruct((n, W), F32),
                          compiler_params=_params(None))(v)


def lane_total(name, v):
    def body(v_ref, o_ref):
        o_ref[...] = jnp.broadcast_to(jnp.sum(v_ref[...], axis=1, keepdims=True), o_ref.shape)
    return pl.pallas_call(body, name=name, out_shape=jax.ShapeDtypeStruct((1, LANES), F32))(v)


def adamw(name, w, g, m, v):
    rows, cols = w.shape
    tr = _tile(rows, max(SUB, (1 << 19) // cols // SUB * SUB), SUB)

    def body(w_ref, g_ref, m_ref, v_ref, d_ref, nm_ref, nv_ref):
        gg = g_ref[...]
        nm = ADAM_B1 * m_ref[...] + (1.0 - ADAM_B1) * gg
        nv = ADAM_B2 * v_ref[...] + (1.0 - ADAM_B2) * (gg * gg)
        m_hat = nm / (1.0 - ADAM_B1 ** ADAM_STEP)
        v_hat = nv / (1.0 - ADAM_B2 ** ADAM_STEP)
        d_ref[...] = -ADAM_LR * (m_hat / (jnp.sqrt(v_hat) + ADAM_EPS) + ADAM_WD * w_ref[...])
        nm_ref[...] = nm
        nv_ref[...] = nv

    spec = pl.BlockSpec((tr, cols), lambda i: (i, 0))
    return pl.pallas_call(
        body, name=name, grid=(rows // tr,), in_specs=[spec] * 4, out_specs=[spec] * 3,
        out_shape=[jax.ShapeDtypeStruct((rows, cols), F32)] * 3,
        compiler_params=_params(("parallel",)))(w, g, m, v)


def rope_tables(pos_col, invf, one_nope, rope_mask, tm):
    T = pos_col.shape[0]

    def body(p_ref, f_ref, o_ref, r_ref, cq_ref, sq_ref, ck_ref):
        ang = p_ref[...] * f_ref[...]
        cs = jnp.cos(ang) * r_ref[...]
        cq_ref[...] = o_ref[...] + cs
        sq_ref[...] = jnp.sin(ang) * r_ref[...]
        ck_ref[...] = cs

    vec = pl.BlockSpec((1, LANES), lambda i: (0, 0))
    out = pl.BlockSpec((tm, LANES), lambda i: (i, 0))
    return pl.pallas_call(
        body, name="rope_tables", grid=(T // tm,),
        in_specs=[pl.BlockSpec((tm, 1), lambda i: (i, 0)), vec, vec, vec], out_specs=[out] * 3,
        out_shape=[jax.ShapeDtypeStruct((T, LANES), F32)] * 3,
        compiler_params=_params(("parallel",)))(pos_col, invf, one_nope, rope_mask)


def _place():
    return lax.axis_index("x"), lax.axis_index("y"), lax.axis_index("c")


def allgather8(name, v):
    R, W = v.shape

    def body(x_ref, out_ref, send_sems, recv_sems, local_sem):
        x, y, c = _place()
        me = 4 * x + 2 * y + c
        mine = pltpu.make_async_copy(x_ref, out_ref.at[me], local_sem)
        mine.start()
        sends, peers = [], []
        for k in range(1, N_DEV):
            px, py, pc = x ^ ((k >> 2) & 1), y ^ ((k >> 1) & 1), c ^ (k & 1)
            peers.append((px, py, pc))
            cp = pltpu.make_async_remote_copy(
                src_ref=x_ref, dst_ref=out_ref.at[me], send_sem=send_sems.at[k - 1],
                recv_sem=recv_sems.at[k - 1], device_id=(px, py, pc), device_id_type=MESH)
            cp.start()
            sends.append(cp)
        for k, (px, py, pc) in enumerate(peers):
            pltpu.make_async_remote_copy(
                src_ref=x_ref, dst_ref=out_ref.at[4 * px + 2 * py + pc], send_sem=send_sems.at[k],
                recv_sem=recv_sems.at[k], device_id=(px, py, pc), device_id_type=MESH).wait_recv()
        for cp in sends:
            cp.wait_send()
        mine.wait()

    return pl.pallas_call(
        body, name=name, out_shape=jax.ShapeDtypeStruct((N_DEV, R, W), v.dtype),
        in_specs=[pl.BlockSpec(memory_space=pltpu.VMEM)],
        out_specs=pl.BlockSpec(memory_space=pltpu.VMEM),
        scratch_shapes=[pltpu.SemaphoreType.DMA((N_DEV - 1,)), pltpu.SemaphoreType.DMA((N_DEV - 1,)),
                        pltpu.SemaphoreType.DMA])(v)


def _other_chips(x, y):
    return [(1 - x, y), (x, 1 - y), (1 - x, 1 - y)]


def weight_allgather(w):
    _, rh, C = w.shape

    def body(w_ref, out_ref, send_sems, recv_sems):
        x, y, c = _place()
        s_me = 2 * x + y
        chips = _other_chips(x, y)

        def copy(k, src, dst, to):
            return pltpu.make_async_remote_copy(src_ref=src, dst_ref=dst, send_sem=send_sems.at[k],
                                                recv_sem=recv_sems.at[k], device_id=to,
                                                device_id_type=MESH)

        first = [copy(j, w_ref.at[c], out_ref.at[s_me, c], (cx, cy, c))
                 for j, (cx, cy) in enumerate(chips)]
        for cp in first:
            cp.start()
        passed = []
        for j, (cx, cy) in enumerate(chips):
            landed = out_ref.at[2 * cx + cy, c]
            copy(j, w_ref.at[c], landed, (cx, cy, c)).wait_recv()
            fw = copy(3 + j, landed, landed, (x, y, 1 - c))
            fw.start()
            passed.append(fw)
        for j, (cx, cy) in enumerate(chips):
            copy(3 + j, w_ref.at[c], out_ref.at[2 * cx + cy, 1 - c], (x, y, 1 - c)).wait_recv()
        for cp in first + passed:
            cp.wait_send()

    return pl.pallas_call(
        body, name="weight_allgather",
        out_shape=jax.ShapeDtypeStruct((N_CHIPS, 2, rh, C), w.dtype),
        in_specs=[pl.BlockSpec(memory_space=pl.ANY)], out_specs=pl.BlockSpec(memory_space=pl.ANY),
        scratch_shapes=[pltpu.SemaphoreType.DMA((6,)), pltpu.SemaphoreType.DMA((6,))])(w)


def sibling_swap(g):
    _, n_s, rh, C = g.shape

    def body(g_ref, a_ref, send_sem, recv_sem):
        x, y, c = _place()
        cp = pltpu.make_async_remote_copy(src_ref=g_ref.at[1 - c], dst_ref=a_ref, send_sem=send_sem,
                                          recv_sem=recv_sem, device_id=(x, y, 1 - c),
                                          device_id_type=MESH)
        cp.start()
        cp.wait()

    return pl.pallas_call(
        body, name="grad_sibling_swap", out_shape=jax.ShapeDtypeStruct((n_s, rh, C), g.dtype),
        in_specs=[pl.BlockSpec(memory_space=pl.ANY)], out_specs=pl.BlockSpec(memory_space=pl.ANY),
        scratch_shapes=[pltpu.SemaphoreType.DMA, pltpu.SemaphoreType.DMA])(g)


def chip_partial(g, a, c_idx, tr):
    _, n_s, rh, C = g.shape

    def body(c_ref, g_ref, a_ref, o_ref):
        o_ref[...] = (g_ref[...] + a_ref[...]).astype(o_ref.dtype)

    return pl.pallas_call(
        body, name="grad_chip_partial",
        grid_spec=pltpu.PrefetchScalarGridSpec(
            num_scalar_prefetch=1, grid=(n_s, rh // tr),
            in_specs=[pl.BlockSpec((None, None, tr, C), lambda s, r, cr: (cr[0], s, r, 0)),
                      pl.BlockSpec((None, tr, C), lambda s, r, cr: (s, r, 0))],
            out_specs=pl.BlockSpec((None, tr, C), lambda s, r, cr: (s, r, 0))),
        out_shape=jax.ShapeDtypeStruct((n_s, rh, C), BF16),
        compiler_params=_params(("parallel", "parallel")))(c_idx, g, a)


def chip_exchange(p):
    _, rh, C = p.shape

    def body(p_ref, b_ref, send_sems, recv_sems):
        x, y, c = _place()
        cps = []
        for j, (cx, cy) in enumerate(_other_chips(x, y)):
            cp = pltpu.make_async_remote_copy(
                src_ref=p_ref.at[2 * cx + cy], dst_ref=b_ref.at[j], send_sem=send_sems.at[j],
                recv_sem=recv_sems.at[j], device_id=(cx, cy, c), device_id_type=MESH)
            cp.start()
            cps.append(cp)
        for cp in cps:
            cp.wait()

    return pl.pallas_call(
        body, name="grad_chip_exchange", out_shape=jax.ShapeDtypeStruct((3, rh, C), p.dtype),
        in_specs=[pl.BlockSpec(memory_space=pl.ANY)], out_specs=pl.BlockSpec(memory_space=pl.ANY),
        scratch_shapes=[pltpu.SemaphoreType.DMA((3,)), pltpu.SemaphoreType.DMA((3,))])(p)


def shard_total(p, b, s_idx, tr):
    _, rh, C = p.shape

    def body(s_ref, p_ref, b0, b1, b2, o_ref):
        o_ref[...] = ((p_ref[...].astype(F32) + b0[...].astype(F32)) + b1[...].astype(F32)
                      ) + b2[...].astype(F32)

    def bspec(j):
        return pl.BlockSpec((None, tr, C), lambda r, sr: (j, r, 0))

    return pl.pallas_call(
        body, name="grad_shard_total",
        grid_spec=pltpu.PrefetchScalarGridSpec(
            num_scalar_prefetch=1, grid=(rh // tr,),
            in_specs=[pl.BlockSpec((None, tr, C), lambda r, sr: (sr[0], r, 0)),
                      bspec(0), bspec(1), bspec(2)],
            out_specs=pl.BlockSpec((tr, C), lambda r, sr: (r, 0))),
        out_shape=jax.ShapeDtypeStruct((rh, C), F32),
        compiler_params=_params(("parallel",)))(s_idx, p, b, b, b)


def sibling_share(f):
    rh, C = f.shape

    def body(f_ref, out_ref, send_sem, recv_sem):
        x, y, c = _place()
        cp = pltpu.make_async_remote_copy(src_ref=f_ref, dst_ref=out_ref, send_sem=send_sem,
                                          recv_sem=recv_sem, device_id=(x, y, 1 - c),
                                          device_id_type=MESH)
        cp.start()
        cp.wait()

    return pl.pallas_call(
        body, name="grad_sibling_share", out_shape=jax.ShapeDtypeStruct((rh, C), f.dtype),
        in_specs=[pl.BlockSpec(memory_space=pl.ANY)], out_specs=pl.BlockSpec(memory_space=pl.ANY),
        scratch_shapes=[pltpu.SemaphoreType.DMA, pltpu.SemaphoreType.DMA])(f)


def _rot_cols(w):
    h = w.shape[-1] // 2
    return jnp.concatenate([-w[..., h:], w[..., :h]], axis=-1)


def _slots(parts, lead, n_h):
    width = sum(p.shape[-1] for p in parts)
    pad = jnp.zeros(lead + (n_h, SLOT - width), parts[0].dtype)
    return jnp.concatenate(parts + [pad], axis=-1).reshape(lead + (n_h * SLOT,))


def layout_weights(w_in, w_uq, w_ukv, w_o_attn, dims):
    QL, KVL, C, D, n_h = dims
    o = 0
    w_ql, o = w_in[..., o:o + QL], o + QL
    w_kvl, o = w_in[..., o:o + KVL], o + KVL
    w_kr, o = w_in[..., o:o + ROPE], o + ROPE
    w_glu_a, o = w_in[..., o:o + C], o + C
    w_glu_b, o = w_in[..., o:o + C], o + C
    w_ga, o = w_in[..., o:o + D], o + D
    w_gb = w_in[..., o:o + D]
    z = lambda n: jnp.zeros(w_kr.shape[:-1] + (n,), w_kr.dtype)
    kr_a = jnp.concatenate([z(NOPE), w_kr, z(SLOT - QK_DIM)], axis=-1)
    kr_b = jnp.concatenate([z(NOPE), _rot_cols(w_kr), z(SLOT - QK_DIM)], axis=-1)
    lead = w_uq.shape[:-1]
    q = w_uq.reshape(lead + (n_h, QK_DIM))
    zq = jnp.zeros(lead + (n_h, NOPE), w_uq.dtype)
    wq_a = _slots([q[..., :NOPE], q[..., NOPE:]], lead, n_h)
    wq_b = _slots([zq, _rot_cols(q[..., NOPE:])], lead, n_h)
    lead = w_ukv.shape[:-1]
    kv = w_ukv.reshape(lead + (n_h, NOPE + VDIM))
    w_kn = _slots([kv[..., :NOPE]], lead, n_h)
    w_v = _slots([kv[..., NOPE:]], lead, n_h)
    lead = w_o_attn.shape[:-2]
    wo = w_o_attn.reshape(lead + (n_h, VDIM, D))
    wo = jnp.concatenate([wo, jnp.zeros(lead + (n_h, SLOT - VDIM, D), wo.dtype)], axis=-2)
    w_o = wo.reshape(lead + (n_h * SLOT, D))
    return dict(ql=w_ql, kvl=w_kvl, kr_a=kr_a, kr_b=kr_b, glu_a=w_glu_a, glu_b=w_glu_b,
                ga=w_ga, gb=w_gb, q_a=wq_a, q_b=wq_b, kn=w_kn, v=w_v, o=w_o)


GATHERED = ("w_in", "w_uq", "w_ukv", "w_o_attn", "w_pw2", "w_out", "w_gu", "w_down")
ROW_SHARDED = ("w_out", "w_down")
PACK_COLS = 1024
PACK_ROWS = 256


def _pack_len(shards):
    n = sum(int(np.prod(shards[k].shape)) for k in GATHERED)
    unit = 2 * PACK_ROWS * PACK_COLS
    return n, -(-n // unit) * unit


def pack_shards(shards, dtype):
    n, padded = _pack_len(shards)
    flat = jnp.concatenate([shards[k].astype(dtype).reshape(-1) for k in GATHERED]
                           + [jnp.zeros((padded - n,), dtype)])
    return flat.reshape(2, padded // PACK_COLS // 2, PACK_COLS)


def unpack_full(buf, shards):
    out, off = {}, 0
    for k in GATHERED:
        shp = shards[k].shape
        n = int(np.prod(shp))
        piece = buf[:, off:off + n].reshape((N_CHIPS,) + shp)
        off += n
        out[k] = jnp.concatenate([piece[s] for s in range(N_CHIPS)],
                                 axis=1 if k in ROW_SHARDED else 2)
    return out


def pack_full_grads(grads, shards):
    n, padded = _pack_len(shards)
    rows = []
    for s in range(N_CHIPS):
        parts = []
        for k in GATHERED:
            shp = shards[k].shape
            if k in ROW_SHARDED:
                parts.append(grads[k][:, s * shp[1]:(s + 1) * shp[1], :].reshape(-1))
            else:
                parts.append(grads[k][:, :, s * shp[2]:(s + 1) * shp[2]].reshape(-1))
        parts.append(jnp.zeros((padded - n,), F32))
        rows.append(jnp.concatenate(parts))
    return jnp.stack(rows)


def unpack_shard(flat, shards):
    out, off = {}, 0
    for k in GATHERED:
        shp = shards[k].shape
        n = int(np.prod(shp))
        out[k] = flat[off:off + n].reshape(shp)
        off += n
    return out


def _pad_rows8(flat):
    n = flat.shape[0]
    w = -(-n // (SUB * LANES)) * LANES
    return jnp.concatenate([flat, jnp.zeros((SUB * w - n,), flat.dtype)]).reshape(SUB, w)


def kernel(x, c, positions, w_ada, b_ada, g_mix, w_in, g_q, w_uq, g_kv, w_ukv, w_o_attn, w_dw, b_dw, g_cn, b_cn, w_pw2, w_out, g_ffn, w_gu, w_down, g_final, loss_target, m_w_ada, m_b_ada, m_g_mix, m_w_in, m_g_q, m_w_uq, m_g_kv, m_w_ukv, m_w_o_attn, m_w_dw, m_b_dw, m_g_cn, m_b_cn, m_w_pw2, m_w_out, m_g_ffn, m_w_gu, m_w_down, m_g_final, v_w_ada, v_b_ada, v_g_mix, v_w_in, v_g_q, v_w_uq, v_g_kv, v_w_ukv, v_w_o_attn, v_w_dw, v_b_dw, v_g_cn, v_b_cn, v_w_pw2, v_w_out, v_g_ffn, v_w_gu, v_w_down, v_g_final):
    weights = dict(w_ada=w_ada, b_ada=b_ada, g_mix=g_mix, w_in=w_in, g_q=g_q, w_uq=w_uq, g_kv=g_kv,
                   w_ukv=w_ukv, w_o_attn=w_o_attn, w_dw=w_dw, b_dw=b_dw, g_cn=g_cn, b_cn=b_cn,
                   w_pw2=w_pw2, w_out=w_out, g_ffn=g_ffn, w_gu=w_gu, w_down=w_down, g_final=g_final)
    mom = dict(w_ada=m_w_ada, b_ada=m_b_ada, g_mix=m_g_mix, w_in=m_w_in, g_q=m_g_q, w_uq=m_w_uq,
               g_kv=m_g_kv, w_ukv=m_w_ukv, w_o_attn=m_w_o_attn, w_dw=m_w_dw, b_dw=m_b_dw,
               g_cn=m_g_cn, b_cn=m_b_cn, w_pw2=m_w_pw2, w_out=m_w_out, g_ffn=m_g_ffn, w_gu=m_w_gu,
               w_down=m_w_down, g_final=m_g_final)
    var = dict(w_ada=v_w_ada, b_ada=v_b_ada, g_mix=v_g_mix, w_in=v_w_in, g_q=v_g_q, w_uq=v_w_uq,
               g_kv=v_g_kv, w_ukv=v_w_ukv, w_o_attn=v_w_o_attn, w_dw=v_w_dw, b_dw=v_b_dw,
               g_cn=v_g_cn, b_cn=v_b_cn, w_pw2=v_w_pw2, w_out=v_w_out, g_ffn=v_g_ffn, w_gu=v_w_gu,
               w_down=v_w_down, g_final=v_g_final)
    order = list(weights)

    n_e, S, D = x.shape
    T = n_e * S
    n_l = w_in.shape[0]
    QL, KVL, C = g_q.shape[1], g_kv.shape[1], g_cn.shape[1]
    n_h = w_uq.shape[2] * N_CHIPS // QK_DIM
    F = w_gu.shape[2] * N_CHIPS // 2
    B = n_e * N_DEV
    dims = (QL, KVL, C, D, n_h)
    scale = QK_DIM ** -0.5
    tm = _tile(S, 512, HALO)
    blk = _tile(S, ATTN_LB, LANES)
    tq = _tile(S, ATTN_TQ, blk)
    q_scale = scale / LN2
    mx, my, mc = _place()
    dev = 4 * mx + 2 * my + mc
    chip = 2 * mx + my
    c_idx = jnp.reshape(mc, (1,)).astype(jnp.int32)
    s_idx = jnp.reshape(chip, (1,)).astype(jnp.int32)

    xt = x.reshape(T, D)
    tgt = loss_target.reshape(T, D)

    shards = {k: weights[k] for k in GATHERED}
    packed = pack_shards(shards, BF16)
    zero = jnp.zeros((), jnp.int32)
    gathered = lax.dynamic_update_slice(weight_allgather(packed), packed[None],
                                        (chip.astype(jnp.int32), zero, zero, zero))
    full = unpack_full(gathered.reshape(N_CHIPS, -1), shards)

    ns_ada = w_ada.shape[2]
    n_c, n_wdw = n_e * D, int(np.prod(w_dw.shape))
    small_all = allgather8("gather_c_wdw", _pad_rows8(
        jnp.concatenate([c.reshape(-1), w_dw.reshape(-1)]))).reshape(N_DEV, -1)
    c_all = small_all[:, :n_c].reshape(B, D)
    wdw_parts = small_all[:, n_c:n_c + n_wdw].reshape((N_DEV,) + w_dw.shape)
    w_dw_full = jnp.concatenate([wdw_parts[2 * s] for s in range(N_CHIPS)], axis=2)
    w_dw_pad = jnp.concatenate([w_dw_full, jnp.zeros((n_l, HALO - CONV_W, C), F32)], axis=1)

    c_act = silu_small(c_all)
    b_ada_mine = lax.dynamic_slice_in_dim(b_ada, chip * ns_ada, ns_ada, axis=1)
    mod_part = ada_fwd(c_act, w_ada, b_ada_mine)
    mod_all = allgather8("gather_mod", mod_part.reshape(n_l * B, ns_ada)).reshape(
        N_DEV, n_l, B, ns_ada)
    mod_full = jnp.concatenate([mod_all[2 * s] for s in range(N_CHIPS)], axis=2)
    mod_mine = lax.dynamic_slice_in_dim(mod_full, dev * n_e, n_e, axis=1)
    mods = mod_mine.reshape(n_l, n_e, N_MOD, 1, D)

    lay = layout_weights(full["w_in"], full["w_uq"], full["w_ukv"], full["w_o_attn"], dims)

    lane = np.arange(LANES)
    in_rope = (lane >= NOPE) & (lane < QK_DIM)
    inv_freq = ROPE_THETA ** (-np.arange(0, ROPE, 2, dtype=np.float32) / ROPE)
    invf = np.where(in_rope, inv_freq[(lane - NOPE) % (ROPE // 2)], 0.0).astype(np.float32)
    cos_q, sin_q, cos_k = rope_tables(
        positions.astype(F32).reshape(T, 1), jnp.asarray(invf).reshape(1, LANES),
        jnp.asarray((lane < NOPE).astype(np.float32)).reshape(1, LANES),
        jnp.asarray(in_rope.astype(np.float32)).reshape(1, LANES), tm)

    def rope_epi(accs, ex):
        n = accs[0].shape[1]
        return (accs[0] * _lanes(ex[0], n) + accs[1] * _lanes(ex[1], n),)

    def rms_epi(accs, ex):
        a = accs[0]
        return a, a * _rstd(a) * ex[0]

    def modnorm(name, xin, g, sc, sh):
        def fn(xv, gv, scv, shv):
            return (xv * _rstd(xv) * gv * (1.0 + scv) + shv,)
        return rowwise(name, [('tile', xin), ('vec', g), ('exvec', sc), ('exvec', sh)],
                       [('tile', D, BF16)], fn, tm, S)[0]

    def modnorm_bwd(name, dh, xin, dres, g, sc):
        def fn(dhv, xv, drv, gv, scv):
            rstd = _rstd(xv)
            xhat = xv * rstd
            dx = _norm_bwd(dhv * gv * (1.0 + scv), xhat, rstd) + drv
            return dx, dhv, dhv * xhat * gv, dhv * xhat * (1.0 + scv)
        return rowwise(name, [('tile', dh), ('tile', xin), ('tile', dres), ('vec', g), ('exvec', sc)],
                       [('tile', D, F32), ('exacc', D), ('exacc', D), ('acc', D)], fn, tm, S)

    def rms_bwd(name, dy, xin, g, width):
        def fn(dyv, xv, gv):
            rstd = _rstd(xv)
            xhat = xv * rstd
            return _norm_bwd(dyv * gv, xhat, rstd), dyv * xhat
        return rowwise(name, [('tile', dy), ('tile', xin), ('vec', g)],
                       [('tile', width, BF16), ('acc', width)], fn, tm, S)

    def gate_bwd(name, dxo, branch, gt):
        def fn(dv, bv, gv):
            return dv * gv, dv * bv
        return rowwise(name, [('tile', dxo), ('tile', branch), ('exvec', gt)],
                       [('tile', D, BF16), ('exacc', D)], fn, tm, S)

    tn_d = _tile(D, 512, LANES)
    tn_f = _tile(F, 1536, LANES)
    tn_s = _tile(n_h * SLOT, 512, LANES)
    row = lambda a: a.reshape(1, -1)

    saved = []
    xc = xt
    for l in range(n_l):
        W = {k: v[l] for k, v in lay.items()}
        sh1, sc1, gt1, sh2, sc2, gt2 = [mods[l, :, k] for k in range(N_MOD)]
        h1 = modnorm("modnorm_mix", xc, row(g_mix[l]), sc1, sh1)
        ga, gb = mm_fused("proj_gates", [h1], [(0, W["ga"]), (0, W["gb"])], [], lambda a, e: a,
                          [F32, F32], tm, tn_d, S)
        cu, glu_a, glu_b = mm_fused(
            "proj_glu", [h1], [(0, W["glu_a"]), (0, W["glu_b"])], [],
            lambda a, e: (a[0] * _sig(a[1]), a[0], a[1]), [F32, F32, F32], tm, C, S)
        q_lat, qn = mm_fused("proj_q_lat", [h1], [(0, W["ql"])], [('vec', row(g_q[l]))], rms_epi,
                             [F32, BF16], tm, QL, S)
        kv_lat, kvn = mm_fused("proj_kv_lat", [h1], [(0, W["kvl"])], [('vec', row(g_kv[l]))],
                               rms_epi, [F32, BF16], tm, KVL, S)
        kr = mm_fused("proj_k_rope", [h1], [(0, W["kr_a"]), (0, W["kr_b"])],
                      [('row128', cos_k), ('row128', sin_q)], rope_epi, [F32], tm, SLOT, S)[0]
        q_all = mm_fused("q_up", [qn], [(0, W["q_a"]), (0, W["q_b"])],
                         [('row128', cos_q), ('row128', sin_q)],
                         lambda a, e: (rope_epi(a, e)[0] * q_scale,), [BF16], tm, tn_s, S)[0]
        k_all, v_all = mm_fused(
            "kv_up", [kvn], [(0, W["kn"]), (0, W["v"])], [('row128', kr)],
            lambda a, e: (a[0] + _lanes(e[0], a[0].shape[1]), a[1]), [BF16, BF16], tm, tn_s, S)
        o_all, lse = attn_fwd(q_all, k_all, v_all, n_e, S, n_h, tq, blk)
        u, s_act = conv_fwd(cu, w_dw_pad[l], row(b_dw[l]), row(g_cn[l]), row(b_cn[l]), tm, S)

        def merge_epi(a, e):
            return _sig(e[0]) * a[0] + _sig(e[1]) * a[1], a[0], a[1]
        y, ya, yc = mm_fused("merge", [o_all, s_act], [(0, W["o"]), (1, full["w_pw2"][l])],
                             [('tile', ga), ('tile', gb)], merge_epi, [BF16, F32, F32], tm, tn_d, S)
        x2, o_mix = mm_fused("mix_out", [y], [(0, full["w_out"][l])], [('tile', xc), ('exvec', gt1)],
                             lambda a, e: (e[0] + e[1] * a[0], a[0]), [F32, F32], tm, tn_d, S)
        h2 = modnorm("modnorm_ffn", x2, row(g_ffn[l]), sc2, sh2)

        def swiglu_epi(a, e):
            return a[0], a[1], a[0] * _sig(a[0]) * a[1]
        w_gu_l = full["w_gu"][l]
        g_act, up, act = mm_fused("ffn_up", [h2], [(0, w_gu_l), (0, w_gu_l, False, F // tn_f)], [],
                                  swiglu_epi, [BF16, BF16, BF16], tm, tn_f, S, N=F)
        x3, dn = mm_fused("ffn_down", [act], [(0, full["w_down"][l])], [('tile', x2), ('exvec', gt2)],
                          lambda a, e: (e[0] + e[1] * a[0], a[0]), [F32, F32], tm, tn_d, S)
        saved.append(dict(x=xc, h1=h1, ga=ga, gb=gb, cu=cu, glu_a=glu_a, glu_b=glu_b, q_lat=q_lat,
                          qn=qn, kv_lat=kv_lat, kvn=kvn, q_all=q_all, k_all=k_all, v_all=v_all,
                          o_all=o_all, lse=lse, u=u, s_act=s_act, y=y, ya=ya, yc=yc, x2=x2,
                          o_mix=o_mix, h2=h2, g_act=g_act, up=up, act=act, dn=dn))
        xc = x3

    def loss_fn(xv, tv, gv):
        rstd = _rstd(xv)
        xhat = xv * rstd
        err = xhat * gv - tv
        dy = err * (1.0 / D)
        return _norm_bwd(dy * gv, xhat, rstd), err * err * (0.5 / D), dy * xhat
    dxc, loss_acc, dg_final = rowwise("loss_head", [('tile', xc), ('tile', tgt), ('vec', row(g_final))],
                                      [('tile', D, F32), ('acc', D), ('acc', D)], loss_fn, tm, S)

    gfull = {k: [None] * n_l for k in GATHERED}
    g_wdw = [None] * n_l
    small_acc = {k: [None] * n_l for k in ("g_mix", "g_q", "g_kv", "b_dw", "g_cn", "b_cn", "g_ffn")}
    dmod_acc = [None] * n_l
    lay_T = jax.linear_transpose(
        lambda a, b, cc, d: layout_weights(a, b, cc, d, dims),
        *[jax.ShapeDtypeStruct(full[k].shape[1:], F32) for k in ("w_in", "w_uq", "w_ukv", "w_o_attn")])

    for l in reversed(range(n_l)):
        sv = saved[l]
        W = {k: v[l] for k, v in lay.items()}
        sh1, sc1, gt1, sh2, sc2, gt2 = [mods[l, :, k] for k in range(N_MOD)]
        w_gu_l = full["w_gu"][l]
        ddn, dgt2 = gate_bwd("ffn_gate_bwd", dxc, sv["dn"], gt2)

        def swiglu_bwd_epi(a, e):
            gv, uv = e[0].astype(F32), e[1].astype(F32)
            sg = _sig(gv)
            return a[0] * uv * sg * (1.0 + gv * (1.0 - sg)), a[0] * gv * sg
        dg_act, dup = mm_fused("ffn_down_bwd", [ddn], [(0, full["w_down"][l], True)],
                               [('tile', sv["g_act"]), ('tile', sv["up"])], swiglu_bwd_epi,
                               [BF16, BF16], tm, tn_f, S)
        gfull["w_down"][l] = mm_tn("ffn_down_dw", sv["act"], ddn)
        gfull["w_gu"][l] = jnp.concatenate([mm_tn("ffn_gate_dw", sv["h2"], dg_act),
                                            mm_tn("ffn_up_dw", sv["h2"], dup)], axis=1)
        dh2 = mm_fused("ffn_up_bwd", [dg_act, dup], [(0, w_gu_l, True, 0), (1, w_gu_l, True, 1)], [],
                       lambda a, e: (a[0] + a[1],), [F32], tm, tn_d, S)[0]
        dx2, dsh2, dsc2, dg_ffn = modnorm_bwd("modnorm_ffn_bwd", dh2, sv["x2"], dxc,
                                              row(g_ffn[l]), sc2)
        ddo, dgt1 = gate_bwd("mix_gate_bwd", dx2, sv["o_mix"], gt1)

        def merge_bwd_epi(a, e):
            sa, sb = _sig(e[0]), _sig(e[1])
            dy = a[0]
            return dy * sa, dy * sb, dy * e[2] * sa * (1.0 - sa), dy * e[3] * sb * (1.0 - sb)
        dya, dyc, dga, dgb = mm_fused(
            "mix_out_bwd", [ddo], [(0, full["w_out"][l], True)],
            [('tile', sv["ga"]), ('tile', sv["gb"]), ('tile', sv["ya"]), ('tile', sv["yc"])],
            merge_bwd_epi, [BF16] * 4, tm, tn_d, S)
        gfull["w_out"][l] = mm_tn("mix_out_dw", sv["y"], ddo)
        do_all = mm_fused("attn_out_bwd", [dya], [(0, W["o"], True)], [], lambda a, e: a, [BF16],
                          tm, tn_s, S)[0]
        d_wo = mm_tn("attn_out_dw", sv["o_all"], dya)
        ds_act = mm_fused("conv_out_bwd", [dyc], [(0, full["w_pw2"][l], True)], [], lambda a, e: a,
                          [F32], tm, C, S)[0]
        gfull["w_pw2"][l] = mm_tn("conv_out_dw", sv["s_act"], dyc)

        def ln_silu_bwd(dsv, uv, gv, bv):
            xhat, rstd = _layer_norm_parts(uv)
            ln = xhat * gv + bv
            sg = _sig(ln)
            dln = dsv * sg * (1.0 + ln * (1.0 - sg))
            dxhat = dln * gv
            du_ = rstd * (dxhat - jnp.mean(dxhat, axis=-1, keepdims=True)
                          - xhat * jnp.mean(dxhat * xhat, axis=-1, keepdims=True))
            return du_, dln * xhat, dln
        du, dg_cn, db_cn = rowwise(
            "conv_norm_bwd", [('tile', ds_act), ('tile', sv["u"]), ('vec', row(g_cn[l])),
                              ('vec', row(b_cn[l]))],
            [('tile', C, F32), ('acc', C), ('acc', C)], ln_silu_bwd, tm, S)
        dglu_a, dglu_b, dw_acc, db_dw = conv_bwd(du, sv["cu"], sv["glu_a"], sv["glu_b"],
                                                 w_dw_pad[l], tm, S)
        g_wdw[l] = group_sum("conv_dw_rows", dw_acc, SUB)[:CONV_W]
        dk_all, dv_all, gq_a, gq_b = attn_bwd(sv["q_all"], sv["k_all"], sv["v_all"], sv["o_all"],
                                              do_all, sv["lse"], cos_q, sin_q, n_e, S, n_h, blk, scale)
        dqn = mm_fused("q_up_bwd", [gq_a, gq_b], [(0, W["q_a"], True), (1, W["q_b"], True)], [],
                       lambda a, e: (a[0] + a[1],), [F32], tm, QL, S)[0]
        d_wqa = mm_tn("q_up_dw_a", sv["qn"], gq_a)
        d_wqb = mm_tn("q_up_dw_b", sv["qn"], gq_b)
        dq_lat, dg_q = rms_bwd("q_norm_bwd", dqn, sv["q_lat"], row(g_q[l]), QL)

        def k_split(dkv, ckv, skv):
            tot = dkv[:, :SLOT]
            for h in range(1, n_h):
                tot = tot + dkv[:, h * SLOT:(h + 1) * SLOT]
            return dkv, tot * ckv, tot * skv
        dk_b, dkr_a, dkr_b = rowwise("k_rope_bwd", [('tile', dk_all), ('tile', cos_k), ('tile', sin_q)],
                                     [('tile', n_h * SLOT, BF16), ('tile', SLOT, BF16),
                                      ('tile', SLOT, BF16)], k_split, tm, S)
        dkvn = mm_fused("kv_up_bwd", [dk_b, dv_all], [(0, W["kn"], True), (1, W["v"], True)], [],
                        lambda a, e: (a[0] + a[1],), [F32], tm, KVL, S)[0]
        d_wkn = mm_tn("kv_up_dw_k", sv["kvn"], dk_b)
        d_wv = mm_tn("kv_up_dw_v", sv["kvn"], dv_all)
        dkv_lat, dg_kv = rms_bwd("kv_norm_bwd", dkvn, sv["kv_lat"], row(g_kv[l]), KVL)
        segs = [("ga", dga), ("gb", dgb), ("glu_a", dglu_a), ("glu_b", dglu_b), ("ql", dq_lat),
                ("kvl", dkv_lat), ("kr_a", dkr_a), ("kr_b", dkr_b)]
        dh1 = mm_fused("proj_bwd", [g for _, g in segs],
                       [(k, W[nm], True) for k, (nm, _) in enumerate(segs)],
                       [], lambda a, e: (functools.reduce(lambda p, q: p + q, a),), [F32],
                       tm, tn_d, S)[0]
        d_lay = {nm: mm_tn("proj_dw_" + nm, sv["h1"], g) for nm, g in segs}
        d_lay.update(q_a=d_wqa, q_b=d_wqb, kn=d_wkn, v=d_wv, o=d_wo)
        (gfull["w_in"][l], gfull["w_uq"][l], gfull["w_ukv"][l],
         gfull["w_o_attn"][l]) = lay_T({k: d_lay[k] for k in lay})
        dxc, dsh1, dsc1, dg_mix = modnorm_bwd("modnorm_mix_bwd", dh1, sv["x"], dx2,
                                              row(g_mix[l]), sc1)
        dmod_acc[l] = [dsh1, dsc1, dgt1, dsh2, dsc2, dgt2]
        for k, a in (("g_mix", dg_mix), ("g_q", dg_q), ("g_kv", dg_kv), ("b_dw", db_dw),
                     ("g_cn", dg_cn), ("b_cn", db_cn), ("g_ffn", dg_ffn)):
            small_acc[k][l] = a

    grad_x = dxc.reshape(n_e, S, D)

    dmod_rows = jnp.concatenate([a for l in range(n_l) for a in dmod_acc[l]], axis=0)
    dmod_own = group_sum("dmod_rows", dmod_rows, SUB).reshape(n_l, N_MOD, n_e, D)
    dmod_own = jnp.transpose(dmod_own, (0, 2, 1, 3)).reshape(n_l * n_e, N_MOD * D)
    dmod_all = allgather8("gather_dmod", dmod_own).reshape(N_DEV, n_l, n_e, N_MOD * D)
    dmod_all = jnp.transpose(dmod_all, (1, 0, 2, 3)).reshape(n_l, B, N_MOD * D)
    dmod_mine = lax.dynamic_slice_in_dim(dmod_all, chip * ns_ada, ns_ada, axis=2)
    grad_w_ada = ada_bwd_w(jnp.transpose(c_act), dmod_mine)
    grad_b_ada = group_sum("grad_b_ada", dmod_all.reshape(n_l * B, N_MOD * D), B)

    gstack = {k: jnp.stack(gfull[k]) for k in GATHERED}
    _, padded = _pack_len(shards)
    rh = padded // PACK_COLS // 2
    gpk = pack_full_grads(gstack, shards).reshape(N_CHIPS, 2, rh, PACK_COLS)
    gpk = jnp.transpose(gpk, (1, 0, 2, 3))
    part = chip_partial(gpk, sibling_swap(gpk), c_idx, PACK_ROWS)
    half = shard_total(part, chip_exchange(part), s_idx, PACK_ROWS)
    other = sibling_share(half)
    both = lax.dynamic_update_slice(jnp.broadcast_to(other[None], (2,) + other.shape), half[None],
                                    (mc.astype(jnp.int32), zero, zero))
    red = unpack_shard(both.reshape(-1), shards)

    wdw_full_g = jnp.stack(g_wdw)
    pieces = [loss_acc, dg_final] + [small_acc[k][l] for k in small_acc for l in range(n_l)]
    widths = [p.shape[1] for p in pieces]
    n_acc = sum(widths)
    wdw_blk = _pad_rows8(wdw_full_g.reshape(-1))
    wdw_w = wdw_blk.shape[1]
    gathered_small = allgather8("gather_small_grads", jnp.concatenate(pieces + [wdw_blk], axis=1))
    acc_sum = group_sum("small_total", gathered_small[:, :, :n_acc].reshape(N_DEV * SUB, n_acc),
                        N_DEV * SUB)
    wdw_sum = group_sum("wdw_total", gathered_small[:, :, n_acc:].reshape(N_DEV, SUB * wdw_w), N_DEV)
    offs = np.cumsum([0] + widths)
    take = lambda i: acc_sum[:, offs[i]:offs[i + 1]]
    loss = lane_total("loss_total", take(0))[0, 0]
    g_small = {"g_final": take(1).reshape(-1)}
    i = 2
    for k in small_acc:
        g_small[k] = jnp.concatenate([take(i + l) for l in range(n_l)], axis=0)
        i += n_l
    wdw_total = wdw_sum.reshape(-1)[:n_wdw * N_CHIPS].reshape(wdw_full_g.shape)
    grad_w_dw = lax.dynamic_slice_in_dim(wdw_total, chip * w_dw.shape[2], w_dw.shape[2], axis=2)

    grads = dict(w_ada=grad_w_ada, b_ada=grad_b_ada, w_dw=grad_w_dw, **g_small, **red)

    deltas, new_m, new_v = {}, {}, {}
    for k in order:
        shp = weights[k].shape
        two = (1, shp[0]) if len(shp) == 1 else (int(np.prod(shp[:-1])), shp[-1])
        d, nm, nv = adamw("adamw_" + k, weights[k].reshape(two), grads[k].reshape(two),
                          mom[k].reshape(two), var[k].reshape(two))
        deltas[k], new_m[k], new_v[k] = d.reshape(shp), nm.reshape(shp), nv.reshape(shp)
        grads[k] = grads[k].reshape(shp)

    return (loss, grad_x, *[grads[k] for k in order], *[deltas[k] for k in order],
            *[new_m[k] for k in order], *[new_v[k] for k in order])
```

```python
import functools

import numpy as np
import jax
import jax.numpy as jnp
from jax import lax
from jax.experimental import pallas as pl
from jax.experimental.pallas import tpu as pltpu

F32 = jnp.float32
BF16 = jnp.bfloat16
MESH = pl.DeviceIdType.MESH

EPS = 1e-6
NEG_INF = -1e30
NOPE, ROPE, VDIM = 64, 32, 64
QK_DIM = NOPE + ROPE
SLOT = 128
CONV_W = 31
HALO = 32
N_MOD = 6
ROPE_THETA = 10000.0
N_CHIPS = 4
N_DEV = 8
SUB = 8
LANES = 128
VMEM_LIMIT = 56 * 1024 * 1024

ADAM_LR, ADAM_B1, ADAM_B2, ADAM_EPS, ADAM_WD, ADAM_STEP = 0.001, 0.9, 0.999, 1e-08, 0.01, 10


def _tile(n, cap, mult):
    best = None
    for d in range(mult, min(n, cap) + 1, mult):
        if n % d == 0:
            best = d
    return best if best is not None else n


def _params(sem):
    return pltpu.CompilerParams(dimension_semantics=sem, vmem_limit_bytes=VMEM_LIMIT)


def _sig(x):
    return 1.0 / (1.0 + jnp.exp(-x))


def _sum8(x):
    r, w = x.shape
    return jnp.sum(x.reshape(r // SUB, SUB, w), axis=0)


def _lanes(v, n):
    return v if n == v.shape[1] else jnp.tile(v, (1, n // v.shape[1]))


def _rstd(x):
    return lax.rsqrt(jnp.mean(x * x, axis=-1, keepdims=True) + EPS)


def _norm_bwd(dxhat, xhat, rstd):
    return rstd * (dxhat - xhat * jnp.mean(dxhat * xhat, axis=-1, keepdims=True))


def mm_fused(name, As, pairs, extras, epilogue, out_dtypes, tm, tn, S, N=None):
    T = As[0].shape[0]
    pairs = [(p[0], p[1], p[2] if len(p) > 2 else False, p[3] if len(p) > 3 else 0) for p in pairs]
    if N is None:
        N = pairs[0][1].shape[0] if pairs[0][2] else pairs[0][1].shape[1]
    nex = S // tm
    in_specs, args = [], []
    for a in As:
        in_specs.append(pl.BlockSpec((tm, a.shape[1]), lambda i, j: (i, 0)))
        args.append(a)
    for ai, b, trans, off in pairs:
        kdim = As[ai].shape[1]
        if trans:
            in_specs.append(pl.BlockSpec((tn, kdim), lambda i, j, off=off: (j, off)))
        else:
            in_specs.append(pl.BlockSpec((kdim, tn), lambda i, j, off=off: (0, j + off)))
        args.append(b)
    for kind, arr in extras:
        if kind == 'tile':
            in_specs.append(pl.BlockSpec((tm, tn), lambda i, j: (i, j)))
        elif kind == 'row128':
            in_specs.append(pl.BlockSpec((tm, LANES), lambda i, j: (i, 0)))
        elif kind == 'vec':
            in_specs.append(pl.BlockSpec((1, tn), lambda i, j: (0, j)))
        else:
            in_specs.append(pl.BlockSpec((None, 1, tn), lambda i, j: (i // nex, 0, j)))
        args.append(arr)
    n_a, n_p, n_e = len(As), len(pairs), len(extras)

    def body(*refs):
        a_refs, b_refs = refs[:n_a], refs[n_a:n_a + n_p]
        e_refs, o_refs = refs[n_a + n_p:n_a + n_p + n_e], refs[n_a + n_p + n_e:]
        accs = [lax.dot_general(a_refs[ai][...], b_refs[k][...], NT if trans else NN,
                                preferred_element_type=F32)
                for k, (ai, _, trans, _) in enumerate(pairs)]
        outs = epilogue(accs, [r[...] for r in e_refs])
        for o_ref, o in zip(o_refs, outs):
            o_ref[...] = o.astype(o_ref.dtype)

    return pl.pallas_call(
        body, name=name, grid=(T // tm, N // tn), in_specs=in_specs,
        out_specs=[pl.BlockSpec((tm, tn), lambda i, j: (i, j)) for _ in out_dtypes],
        out_shape=[jax.ShapeDtypeStruct((T, N), dt) for dt in out_dtypes],
        compiler_params=_params(("parallel", "parallel")))(*args)


def mm_tn(name, A, G):
    T, K = A.shape
    N = G.shape[1]
    tt = _tile(T, 512, 16)
    tk = _tile(K, 1536, LANES)
    tn = _tile(N, 1536, LANES)

    def body(a_ref, g_ref, o_ref):
        part = lax.dot_general(a_ref[...], g_ref[...], (((0,), (0,)), ((), ())),
                               preferred_element_type=F32)

        @pl.when(pl.program_id(2) == 0)
        def _():
            o_ref[...] = part

        @pl.when(pl.program_id(2) > 0)
        def _():
            o_ref[...] += part

    return pl.pallas_call(
        body, name=name, grid=(K // tk, N // tn, T // tt),
        in_specs=[pl.BlockSpec((tt, tk), lambda k, n, t: (t, k)),
                  pl.BlockSpec((tt, tn), lambda k, n, t: (t, n))],
        out_specs=pl.BlockSpec((tk, tn), lambda k, n, t: (k, n)),
        out_shape=jax.ShapeDtypeStruct((K, N), F32),
        compiler_params=_params(("parallel", "parallel", "arbitrary")))(A, G)


def rowwise(name, ins, outs, fn, tm, S):
    T = next(a.shape[0] for k, a in ins if k == 'tile')
    nex = S // tm
    n_ex = T // S
    in_specs, args = [], []
    for kind, arr in ins:
        if kind == 'tile':
            in_specs.append(pl.BlockSpec((tm, arr.shape[1]), lambda i: (i, 0)))
        elif kind == 'vec':
            in_specs.append(pl.BlockSpec((1, arr.shape[1]), lambda i: (0, 0)))
        else:
            in_specs.append(pl.BlockSpec((None, 1, arr.shape[2]), lambda i: (i // nex, 0, 0)))
        args.append(arr)
    out_specs, out_shape = [], []
    for o in outs:
        if o[0] == 'tile':
            out_specs.append(pl.BlockSpec((tm, o[1]), lambda i: (i, 0)))
            out_shape.append(jax.ShapeDtypeStruct((T, o[1]), o[2]))
        elif o[0] == 'acc':
            out_specs.append(pl.BlockSpec((SUB, o[1]), lambda i: (0, 0)))
            out_shape.append(jax.ShapeDtypeStruct((SUB, o[1]), F32))
        else:
            out_specs.append(pl.BlockSpec((SUB, o[1]), lambda i: (i // nex, 0)))
            out_shape.append(jax.ShapeDtypeStruct((n_ex * SUB, o[1]), F32))
    n_in = len(ins)

    def body(*refs):
        i = pl.program_id(0)
        vals = fn(*[r[...] for r in refs[:n_in]])
        for o, o_ref, v in zip(outs, refs[n_in:], vals):
            if o[0] == 'tile':
                o_ref[...] = v.astype(o_ref.dtype)
            else:
                part = _sum8(v)
                first = (i == 0) if o[0] == 'acc' else (i % nex == 0)

                @pl.when(first)
                def _(o_ref=o_ref, part=part):
                    o_ref[...] = part

                @pl.when(jnp.logical_not(first))
                def _(o_ref=o_ref, part=part):
                    o_ref[...] += part

    return pl.pallas_call(
        body, name=name, grid=(T // tm,), in_specs=in_specs, out_specs=out_specs,
        out_shape=out_shape, compiler_params=_params(("arbitrary",)))(*args)


NN = (((1,), (0,)), ((), ()))
NT = (((1,), (1,)), ((), ()))
TN = (((0,), (0,)), ((), ()))
LN2 = 0.6931471805599453
ATTN_TQ = 1024
ATTN_LB = 512


def attn_fwd(Q, K, V, n_b, S, n_h, tq, lb):
    T = n_b * S
    nq = S // tq
    ratio = tq // lb
    tk = lb

    def body(q_ref, k_ref, v_ref, o_ref, lse_ref, m_s, l_s, acc_s):
        i = pl.program_id(2)
        m_s[...] = jnp.full(m_s.shape, NEG_INF, F32)
        l_s[...] = jnp.zeros(l_s.shape, F32)
        acc_s[...] = jnp.zeros(acc_s.shape, F32)

        def kv_step(j, diag_off, rsplit):
            start = pl.multiple_of(j * tk, tk)
            k = k_ref[pl.ds(start, tk), :]
            v = v_ref[pl.ds(start, tk), :]
            rc = tq // rsplit
            for r in range(rsplit):
                if diag_off is not None and diag_off > r * rc + rc - 1:
                    continue
                rows = pl.ds(r * rc, rc)
                s = lax.dot_general(q_ref[rows, :], k, NT, preferred_element_type=F32)
                if diag_off is not None and diag_off + tk - 1 > r * rc:
                    rr = lax.broadcasted_iota(jnp.int32, s.shape, 0) + r * rc
                    cc = lax.broadcasted_iota(jnp.int32, s.shape, 1) + diag_off
                    s = jnp.where(rr >= cc, s, NEG_INF)
                m_prev = m_s[rows, :]
                m_new = jnp.maximum(m_prev, jnp.max(s, axis=1, keepdims=True))
                alpha = jnp.exp2(m_prev - m_new)
                p = jnp.exp2(s - _lanes(m_new, tk))
                l_s[rows, :] = alpha * l_s[rows, :] + jnp.sum(p, axis=1, keepdims=True)
                acc_s[rows, :] = alpha * acc_s[rows, :] + jnp.dot(p.astype(BF16), v,
                                                                  preferred_element_type=F32)
                m_s[rows, :] = m_new

        def below_diagonal(j, carry):
            kv_step(j, None, 1)
            return carry
        lax.fori_loop(0, i * ratio, below_diagonal, 0)
        for d in range(ratio):
            kv_step(i * ratio + d, d * tk, ratio)
        l = l_s[...]
        o_ref[...] = (acc_s[...] / l).astype(o_ref.dtype)
        lse = m_s[...] + jnp.log(l) * (1.0 / LN2)
        for u in range(ratio):
            lse_ref[u] = jnp.transpose(lse[u * lb:(u + 1) * lb, :])[:SUB, :]

    qmap = lambda b, h, i: (b * nq + i, h)
    kmap = lambda b, h, i: (b, h)
    return pl.pallas_call(
        body, name="attn_fwd", grid=(n_b, n_h, nq),
        in_specs=[pl.BlockSpec((tq, SLOT), qmap), pl.BlockSpec((S, SLOT), kmap),
                  pl.BlockSpec((S, SLOT), kmap)],
        out_specs=[pl.BlockSpec((tq, SLOT), qmap),
                   pl.BlockSpec((None, ratio, SUB, lb), lambda b, h, i: (b * n_h + h, i, 0, 0))],
        out_shape=[jax.ShapeDtypeStruct((T, n_h * SLOT), BF16),
                   jax.ShapeDtypeStruct((n_b * n_h, S // lb, SUB, lb), F32)],
        scratch_shapes=[pltpu.VMEM((tq, SLOT), F32)] * 3,
        compiler_params=_params(("parallel", "parallel", "arbitrary")))(Q, K, V)


def attn_bwd(Q, K, V, O, dO, LSE, cosq, sinq, n_b, S, n_h, blk, scale):
    T = n_b * S
    nb = S // blk

    def body(q_ref, k_ref, v_ref, o_ref, do_ref, lse_ref, cos_ref, sin_ref,
             dk_ref, dv_ref, ga_ref, gb_ref, dq_s, delta_s, dk_s, dv_s):
        j = pl.program_id(2)

        @pl.when(j == 0)
        def _():
            dq_s[...] = jnp.zeros(dq_s.shape, F32)
            for i in range(nb):
                rows = pl.ds(i * blk, blk)
                d = jnp.sum(do_ref[rows, :].astype(F32) * o_ref[rows, :].astype(F32),
                            axis=1, keepdims=True)
                delta_s[i] = jnp.transpose(jnp.broadcast_to(d, (blk, SLOT)))[:SUB, :]

        k = k_ref[...]
        v = v_ref[...]
        dk_s[...] = jnp.zeros(dk_s.shape, F32)
        dv_s[...] = jnp.zeros(dv_s.shape, F32)

        def q_step(i, masked):
            rows = pl.ds(pl.multiple_of(i * blk, blk), blk)
            q = q_ref[rows, :]
            do = do_ref[rows, :]
            st = lax.dot_general(k, q, NT, preferred_element_type=F32)
            if masked:
                kv_i = lax.broadcasted_iota(jnp.int32, st.shape, 0)
                q_i = lax.broadcasted_iota(jnp.int32, st.shape, 1)
                st = jnp.where(q_i >= kv_i, st, NEG_INF)
            pt = jnp.exp2(st - lse_ref[i][:1, :])
            dpt = lax.dot_general(v, do, NT, preferred_element_type=F32)
            dst = (pt * (dpt - delta_s[i][:1, :])).astype(BF16)
            dv_s[...] += jnp.dot(pt.astype(BF16), do, preferred_element_type=F32)
            dk_s[...] += jnp.dot(dst, q, preferred_element_type=F32)
            dq_s[rows, :] += lax.dot_general(dst, k, TN, preferred_element_type=F32)

        q_step(j, True)

        def above_diagonal(i, carry):
            q_step(i, False)
            return carry
        lax.fori_loop(j + 1, nb, above_diagonal, 0)
        dk_ref[...] = dk_s[...] * LN2
        dv_ref[...] = dv_s[...].astype(dv_ref.dtype)

        @pl.when(j == nb - 1)
        def _():
            dq = dq_s[...] * scale
            ga_ref[...] = (dq * cos_ref[...]).astype(ga_ref.dtype)
            gb_ref[...] = (dq * sin_ref[...]).astype(gb_ref.dtype)

    full = pl.BlockSpec((S, SLOT), lambda b, h, j: (b, h))
    kv = pl.BlockSpec((blk, SLOT), lambda b, h, j: (b * nb + j, h))
    tab = pl.BlockSpec((S, SLOT), lambda b, h, j: (b, 0))
    stat = pl.BlockSpec((None, nb, SUB, blk), lambda b, h, j: (b * n_h + h, 0, 0, 0))
    return pl.pallas_call(
        body, name="attn_bwd", grid=(n_b, n_h, nb),
        in_specs=[full, kv, kv, full, full, stat, tab, tab], out_specs=[kv, kv, full, full],
        out_shape=[jax.ShapeDtypeStruct((T, n_h * SLOT), F32)]
        + [jax.ShapeDtypeStruct((T, n_h * SLOT), BF16)] * 3,
        scratch_shapes=[pltpu.VMEM((S, SLOT), F32), pltpu.VMEM((nb, SUB, blk), F32),
                        pltpu.VMEM((blk, SLOT), F32), pltpu.VMEM((blk, SLOT), F32)],
        compiler_params=_params(("parallel", "parallel", "arbitrary")))(
            Q, K, V, O, dO, LSE, cosq, sinq)


def _layer_norm_parts(u):
    xc = u - jnp.mean(u, axis=-1, keepdims=True)
    rstd = lax.rsqrt(jnp.mean(xc * xc, axis=-1, keepdims=True) + EPS)
    return xc * rstd, rstd


def conv_fwd(cu, w, b_dw, g_cn, b_cn, tm, S):
    T, C = cu.shape
    nex, hb = S // tm, tm // HALO

    def body(cur_ref, prev_ref, w_ref, b_ref, g_ref, bc_ref, u_ref, s_ref, ext):
        first = pl.program_id(0) % nex == 0
        ext[pl.ds(0, HALO), :] = jnp.where(first, 0.0, prev_ref[...])
        ext[pl.ds(HALO, tm), :] = cur_ref[...]
        acc = jnp.zeros((tm, C), F32)
        for j in range(CONV_W):
            acc = acc + w_ref[pl.ds(j, 1), :] * ext[pl.ds(HALO - CONV_W + 1 + j, tm), :]
        u = acc + b_ref[...]
        ln = _layer_norm_parts(u)[0] * g_ref[...] + bc_ref[...]
        u_ref[...] = u
        s_ref[...] = (ln * _sig(ln)).astype(s_ref.dtype)

    vec = pl.BlockSpec((1, C), lambda i: (0, 0))
    return pl.pallas_call(
        body, name="conv_fwd", grid=(T // tm,),
        in_specs=[pl.BlockSpec((tm, C), lambda i: (i, 0)),
                  pl.BlockSpec((HALO, C), lambda i: (jnp.maximum(i * hb - 1, 0), 0)),
                  pl.BlockSpec((HALO, C), lambda i: (0, 0)), vec, vec, vec],
        out_specs=[pl.BlockSpec((tm, C), lambda i: (i, 0))] * 2,
        out_shape=[jax.ShapeDtypeStruct((T, C), F32), jax.ShapeDtypeStruct((T, C), BF16)],
        scratch_shapes=[pltpu.VMEM((HALO + tm, C), F32)],
        compiler_params=_params(("arbitrary",)))(cu, cu, w, b_dw, g_cn, b_cn)


def conv_bwd(du, cu, glu_a, glu_b, w, tm, S):
    T, C = du.shape
    nex, hb = S // tm, tm // HALO
    last_blk = T // HALO - 1

    def body(du_ref, nxt_ref, cu_ref, prev_ref, a_ref, b_ref, w_ref,
             da_ref, db_ref, dw_ref, dbias_ref, extd, extc):
        i = pl.program_id(0)
        first = i % nex == 0
        last = i % nex == nex - 1
        du_cur = du_ref[...]
        extd[pl.ds(0, tm), :] = du_cur
        extd[pl.ds(tm, HALO), :] = jnp.where(last, 0.0, nxt_ref[...])
        extc[pl.ds(0, HALO), :] = jnp.where(first, 0.0, prev_ref[...])
        extc[pl.ds(HALO, tm), :] = cu_ref[...]

        @pl.when(i == 0)
        def _():
            dw_ref[...] = jnp.zeros(dw_ref.shape, F32)
            dbias_ref[...] = jnp.zeros(dbias_ref.shape, F32)

        dcu = jnp.zeros((tm, C), F32)
        for j in range(CONV_W):
            dcu = dcu + w_ref[pl.ds(j, 1), :] * extd[pl.ds(CONV_W - 1 - j, tm), :]
            dw_ref[pl.ds(SUB * j, SUB), :] += _sum8(
                du_cur * extc[pl.ds(HALO - CONV_W + 1 + j, tm), :])
        dbias_ref[...] += _sum8(du_cur)
        sb = _sig(b_ref[...])
        da_ref[...] = (dcu * sb).astype(da_ref.dtype)
        db_ref[...] = (dcu * a_ref[...] * sb * (1.0 - sb)).astype(db_ref.dtype)

    cur = pl.BlockSpec((tm, C), lambda i: (i, 0))
    return pl.pallas_call(
        body, name="conv_bwd", grid=(T // tm,),
        in_specs=[cur, pl.BlockSpec((HALO, C), lambda i: (jnp.minimum((i + 1) * hb, last_blk), 0)),
                  cur, pl.BlockSpec((HALO, C), lambda i: (jnp.maximum(i * hb - 1, 0), 0)),
                  cur, cur, pl.BlockSpec((HALO, C), lambda i: (0, 0))],
        out_specs=[cur, cur, pl.BlockSpec((HALO * SUB, C), lambda i: (0, 0)),
                   pl.BlockSpec((SUB, C), lambda i: (0, 0))],
        out_shape=[jax.ShapeDtypeStruct((T, C), BF16), jax.ShapeDtypeStruct((T, C), BF16),
                   jax.ShapeDtypeStruct((HALO * SUB, C), F32), jax.ShapeDtypeStruct((SUB, C), F32)],
        scratch_shapes=[pltpu.VMEM((tm + HALO, C), F32), pltpu.VMEM((HALO + tm, C), F32)],
        compiler_params=_params(("arbitrary",)))(du, du, cu, cu, glu_a, glu_b, w)


def silu_small(c_all):
    def body(c_ref, o_ref):
        v = c_ref[...]
        o_ref[...] = v * _sig(v)
    return pl.pallas_call(body, name="silu_c", out_shape=jax.ShapeDtypeStruct(c_all.shape, F32))(c_all)


def ada_fwd(c_act, w_ada, b_ada):
    n_l, D, ns = w_ada.shape
    B = c_act.shape[0]
    tn = _tile(ns, 512, LANES)

    def body(c_ref, w_ref, b_ref, o_ref):
        o_ref[...] = jnp.dot(c_ref[...], w_ref[...], preferred_element_type=F32,
                             precision=lax.Precision.HIGHEST) + b_ref[...]

    return pl.pallas_call(
        body, name="ada_fwd", grid=(n_l, ns // tn),
        in_specs=[pl.BlockSpec((B, D), lambda l, j: (0, 0)),
                  pl.BlockSpec((None, D, tn), lambda l, j: (l, 0, j)),
                  pl.BlockSpec((None, 1, tn), lambda l, j: (l, 0, j))],
        out_specs=pl.BlockSpec((None, B, tn), lambda l, j: (l, 0, j)),
        out_shape=jax.ShapeDtypeStruct((n_l, B, ns), F32),
        compiler_params=_params(("parallel", "parallel")))(c_act, w_ada, b_ada.reshape(n_l, 1, ns))


def ada_bwd_w(c_act_t, dmod):
    D, B = c_act_t.shape
    n_l, _, ns = dmod.shape
    tn = _tile(ns, 512, LANES)

    def body(c_ref, d_ref, o_ref):
        o_ref[...] = jnp.dot(c_ref[...], d_ref[...], preferred_element_type=F32,
                             precision=lax.Precision.HIGHEST)

    return pl.pallas_call(
        body, name="ada_bwd_w", grid=(n_l, ns // tn),
        in_specs=[pl.BlockSpec((D, B), lambda l, j: (0, 0)),
                  pl.BlockSpec((None, B, tn), lambda l, j: (l, 0, j))],
        out_specs=pl.BlockSpec((None, D, tn), lambda l, j: (l, 0, j)),
        out_shape=jax.ShapeDtypeStruct((n_l, D, ns), F32),
        compiler_params=_params(("parallel", "parallel")))(c_act_t, dmod)


def group_sum(name, v, group):
    rows, W = v.shape
    n = rows // group

    def body(v_ref, o_ref):
        o_ref[...] = jnp.sum(v_ref[...].reshape(n, group, W), axis=1)

    return pl.pallas_call(body, name=name, out_shape=jax.ShapeDtypeStruct((n, W), F32),
                          compiler_params=_params(None))(v)


def lane_total(name, v):
    def body(v_ref, o_ref):
        o_ref[...] = jnp.broadcast_to(jnp.sum(v_ref[...], axis=1, keepdims=True), o_ref.shape)
    return pl.pallas_call(body, name=name, out_shape=jax.ShapeDtypeStruct((1, LANES), F32))(v)


def adamw(name, w, g, m, v):
    rows, cols = w.shape
    tr = _tile(rows, max(SUB, (1 << 19) // cols // SUB * SUB), SUB)

    def body(w_ref, g_ref, m_ref, v_ref, d_ref, nm_ref, nv_ref):
        gg = g_ref[...]
        nm = ADAM_B1 * m_ref[...] + (1.0 - ADAM_B1) * gg
        nv = ADAM_B2 * v_ref[...] + (1.0 - ADAM_B2) * (gg * gg)
        m_hat = nm / (1.0 - ADAM_B1 ** ADAM_STEP)
        v_hat = nv / (1.0 - ADAM_B2 ** ADAM_STEP)
        d_ref[...] = -ADAM_LR * (m_hat / (jnp.sqrt(v_hat) + ADAM_EPS) + ADAM_WD * w_ref[...])
        nm_ref[...] = nm
        nv_ref[...] = nv

    spec = pl.BlockSpec((tr, cols), lambda i: (i, 0))
    return pl.pallas_call(
        body, name=name, grid=(rows // tr,), in_specs=[spec] * 4, out_specs=[spec] * 3,
        out_shape=[jax.ShapeDtypeStruct((rows, cols), F32)] * 3,
        compiler_params=_params(("parallel",)))(w, g, m, v)


def rope_tables(pos_col, invf, one_nope, rope_mask, tm):
    T = pos_col.shape[0]

    def body(p_ref, f_ref, o_ref, r_ref, cq_ref, sq_ref, ck_ref):
        ang = p_ref[...] * f_ref[...]
        cs = jnp.cos(ang) * r_ref[...]
        cq_ref[...] = o_ref[...] + cs
        sq_ref[...] = jnp.sin(ang) * r_ref[...]
        ck_ref[...] = cs

    vec = pl.BlockSpec((1, LANES), lambda i: (0, 0))
    out = pl.BlockSpec((tm, LANES), lambda i: (i, 0))
    return pl.pallas_call(
        body, name="rope_tables", grid=(T // tm,),
        in_specs=[pl.BlockSpec((tm, 1), lambda i: (i, 0)), vec, vec, vec], out_specs=[out] * 3,
        out_shape=[jax.ShapeDtypeStruct((T, LANES), F32)] * 3,
        compiler_params=_params(("parallel",)))(pos_col, invf, one_nope, rope_mask)


def _place():
    return lax.axis_index("x"), lax.axis_index("y"), lax.axis_index("c")


def allgather8(name, v):
    R, W = v.shape

    def body(x_ref, out_ref, send_sems, recv_sems, local_sem):
        x, y, c = _place()
        me = 4 * x + 2 * y + c
        mine = pltpu.make_async_copy(x_ref, out_ref.at[me], local_sem)
        mine.start()
        sends, peers = [], []
        for k in range(1, N_DEV):
            px, py, pc = x ^ ((k >> 2) & 1), y ^ ((k >> 1) & 1), c ^ (k & 1)
            peers.append((px, py, pc))
            cp = pltpu.make_async_remote_copy(
                src_ref=x_ref, dst_ref=out_ref.at[me], send_sem=send_sems.at[k - 1],
                recv_sem=recv_sems.at[k - 1], device_id=(px, py, pc), device_id_type=MESH)
            cp.start()
            sends.append(cp)
        for k, (px, py, pc) in enumerate(peers):
            pltpu.make_async_remote_copy(
                src_ref=x_ref, dst_ref=out_ref.at[4 * px + 2 * py + pc], send_sem=send_sems.at[k],
                recv_sem=recv_sems.at[k], device_id=(px, py, pc), device_id_type=MESH).wait_recv()
        for cp in sends:
            cp.wait_send()
        mine.wait()

    return pl.pallas_call(
        body, name=name, out_shape=jax.ShapeDtypeStruct((N_DEV, R, W), v.dtype),
        in_specs=[pl.BlockSpec(memory_space=pltpu.VMEM)],
        out_specs=pl.BlockSpec(memory_space=pltpu.VMEM),
        scratch_shapes=[pltpu.SemaphoreType.DMA((N_DEV - 1,)), pltpu.SemaphoreType.DMA((N_DEV - 1,)),
                        pltpu.SemaphoreType.DMA])(v)


def _other_chips(x, y):
    return [(1 - x, y), (x, 1 - y), (1 - x, 1 - y)]


def _hbm_exchange(name, body, ins, out_shapes, n_sems):
    n_in, n_out = len(ins), len(out_shapes)

    def wrapped(*refs):
        send_sems, recv_sems = refs[n_in + n_out:]

        def copy(k, src, dst, to):
            return pltpu.make_async_remote_copy(src_ref=src, dst_ref=dst, send_sem=send_sems.at[k],
                                                recv_sem=recv_sems.at[k], device_id=to,
                                                device_id_type=MESH)
        body(refs[:n_in], refs[n_in:n_in + n_out], copy)

    hbm = pl.BlockSpec(memory_space=pl.ANY)
    return pl.pallas_call(
        wrapped, name=name, out_shape=out_shapes, in_specs=[hbm] * n_in, out_specs=[hbm] * n_out,
        scratch_shapes=[pltpu.SemaphoreType.DMA((n_sems,)), pltpu.SemaphoreType.DMA((n_sems,))])(*ins)


def weight_allgather(ws):
    n = len(ws)

    def body(w_refs, o_refs, copy):
        x, y, c = _place()
        s_me = 2 * x + y
        chips = _other_chips(x, y)
        sends = []
        for t in range(n):
            layers = pl.ds(0, ws[t].shape[0])
            for j, (cx, cy) in enumerate(chips):
                cp = copy(6 * t + j, w_refs[t].at[layers, c], o_refs[t].at[s_me, layers, c], (cx, cy, c))
                cp.start()
                sends.append(cp)
        for t in range(n):
            layers = pl.ds(0, ws[t].shape[0])
            for j, (cx, cy) in enumerate(chips):
                landed = o_refs[t].at[2 * cx + cy, layers, c]
                copy(6 * t + j, w_refs[t].at[layers, c], landed, (cx, cy, c)).wait_recv()
                fw = copy(6 * t + 3 + j, landed, landed, (x, y, 1 - c))
                fw.start()
                sends.append(fw)
        for t in range(n):
            layers = pl.ds(0, ws[t].shape[0])
            for j, (cx, cy) in enumerate(chips):
                copy(6 * t + 3 + j, w_refs[t].at[layers, c],
                     o_refs[t].at[2 * cx + cy, layers, 1 - c], (x, y, 1 - c)).wait_recv()
        for cp in sends:
            cp.wait_send()

    return _hbm_exchange("weight_allgather", body, ws,
                         [jax.ShapeDtypeStruct((N_CHIPS,) + w.shape, w.dtype) for w in ws], 6 * n)


def sibling_swap(gs):
    n = len(gs)

    def body(g_refs, a_refs, copy):
        x, y, c = _place()
        cps = []
        for t in range(n):
            chips, layers = pl.ds(0, N_CHIPS), pl.ds(0, gs[t].shape[1])
            cp = copy(t, g_refs[t].at[chips, layers, 1 - c], a_refs[t], (x, y, 1 - c))
            cp.start()
            cps.append(cp)
        for cp in cps:
            cp.wait()

    return _hbm_exchange("grad_sibling_swap", body, gs,
                         [jax.ShapeDtypeStruct(g.shape[:2] + g.shape[3:], g.dtype) for g in gs], n)


def chip_partial(name, g, a, c_idx):
    n_s, n_l, _, kh, ns = g.shape
    tr = _tile(kh, max(16, (1 << 19) // ns // 16 * 16), 16)

    def body(c_ref, g_ref, a_ref, o_ref):
        o_ref[...] = (g_ref[...] + a_ref[...]).astype(o_ref.dtype)

    return pl.pallas_call(
        body, name=name,
        grid_spec=pltpu.PrefetchScalarGridSpec(
            num_scalar_prefetch=1, grid=(n_s * n_l, kh // tr),
            in_specs=[pl.BlockSpec((None, None, tr, ns), lambda i, r, cr: (i, cr[0], r, 0)),
                      pl.BlockSpec((None, tr, ns), lambda i, r, cr: (i, r, 0))],
            out_specs=pl.BlockSpec((None, tr, ns), lambda i, r, cr: (i, r, 0))),
        out_shape=jax.ShapeDtypeStruct((n_s * n_l, kh, ns), BF16),
        compiler_params=_params(("parallel", "parallel")))(
            c_idx, g.reshape(n_s * n_l, 2, kh, ns), a.reshape(n_s * n_l, kh, ns)).reshape(n_s, n_l, kh, ns)


def chip_exchange(ps):
    n = len(ps)

    def body(p_refs, b_refs, copy):
        x, y, c = _place()
        cps = []
        for t in range(n):
            for j, (cx, cy) in enumerate(_other_chips(x, y)):
                cp = copy(3 * t + j, p_refs[t].at[2 * cx + cy], b_refs[t].at[j], (cx, cy, c))
                cp.start()
                cps.append(cp)
        for cp in cps:
            cp.wait()

    return _hbm_exchange("grad_chip_exchange", body, ps,
                         [jax.ShapeDtypeStruct((3,) + p.shape[1:], p.dtype) for p in ps], 3 * n)


def shard_total(name, p, b, s_idx):
    _, n_l, kh, ns = p.shape
    rows = n_l * kh
    tr = _tile(rows, max(16, (1 << 19) // ns // 16 * 16), 16)

    def body(s_ref, p_ref, b0, b1, b2, o_ref):
        o_ref[...] = ((p_ref[...].astype(F32) + b0[...].astype(F32)) + b1[...].astype(F32)
                      ) + b2[...].astype(F32)

    def bspec(j):
        return pl.BlockSpec((None, tr, ns), lambda r, sr: (j, r, 0))

    b2d = b.reshape(3, rows, ns)
    return pl.pallas_call(
        body, name=name,
        grid_spec=pltpu.PrefetchScalarGridSpec(
            num_scalar_prefetch=1, grid=(rows // tr,),
            in_specs=[pl.BlockSpec((None, tr, ns), lambda r, sr: (sr[0], r, 0)),
                      bspec(0), bspec(1), bspec(2)],
            out_specs=pl.BlockSpec((tr, ns), lambda r, sr: (r, 0))),
        out_shape=jax.ShapeDtypeStruct((rows, ns), F32),
        compiler_params=_params(("parallel",)))(
            s_idx, p.reshape(N_CHIPS, rows, ns), b2d, b2d, b2d).reshape(n_l, kh, ns)


def sibling_share(fs):
    n = len(fs)

    def body(f_refs, o_refs, copy):
        x, y, c = _place()
        cps = []
        for t in range(n):
            cp = copy(t, f_refs[t], o_refs[t], (x, y, 1 - c))
            cp.start()
            cps.append(cp)
        for cp in cps:
            cp.wait()

    return _hbm_exchange("grad_sibling_share", body, fs,
                         [jax.ShapeDtypeStruct(f.shape, f.dtype) for f in fs], n)


def _rot_cols(w):
    h = w.shape[-1] // 2
    return jnp.concatenate([-w[..., h:], w[..., :h]], axis=-1)


def _slots(parts, lead, n_h):
    width = sum(p.shape[-1] for p in parts)
    pad = jnp.zeros(lead + (n_h, SLOT - width), parts[0].dtype)
    return jnp.concatenate(parts + [pad], axis=-1).reshape(lead + (n_h * SLOT,))


def layout_weights(w_in, w_uq, w_ukv, w_o_attn, dims):
    QL, KVL, C, D, n_h = dims
    o = 0
    w_ql, o = w_in[..., o:o + QL], o + QL
    w_kvl, o = w_in[..., o:o + KVL], o + KVL
    w_kr, o = w_in[..., o:o + ROPE], o + ROPE
    w_glu_a, o = w_in[..., o:o + C], o + C
    w_glu_b, o = w_in[..., o:o + C], o + C
    w_ga, o = w_in[..., o:o + D], o + D
    w_gb = w_in[..., o:o + D]
    z = lambda n: jnp.zeros(w_kr.shape[:-1] + (n,), w_kr.dtype)
    kr_a = jnp.concatenate([z(NOPE), w_kr, z(SLOT - QK_DIM)], axis=-1)
    kr_b = jnp.concatenate([z(NOPE), _rot_cols(w_kr), z(SLOT - QK_DIM)], axis=-1)
    lead = w_uq.shape[:-1]
    q = w_uq.reshape(lead + (n_h, QK_DIM))
    zq = jnp.zeros(lead + (n_h, NOPE), w_uq.dtype)
    wq_a = _slots([q[..., :NOPE], q[..., NOPE:]], lead, n_h)
    wq_b = _slots([zq, _rot_cols(q[..., NOPE:])], lead, n_h)
    lead = w_ukv.shape[:-1]
    kv = w_ukv.reshape(lead + (n_h, NOPE + VDIM))
    w_kn = _slots([kv[..., :NOPE]], lead, n_h)
    w_v = _slots([kv[..., NOPE:]], lead, n_h)
    lead = w_o_attn.shape[:-2]
    wo = w_o_attn.reshape(lead + (n_h, VDIM, D))
    wo = jnp.concatenate([wo, jnp.zeros(lead + (n_h, SLOT - VDIM, D), wo.dtype)], axis=-2)
    w_o = wo.reshape(lead + (n_h * SLOT, D))
    return dict(ql=w_ql, kvl=w_kvl, kr_a=kr_a, kr_b=kr_b, glu_a=w_glu_a, glu_b=w_glu_b,
                ga=w_ga, gb=w_gb, q_a=wq_a, q_b=wq_b, kn=w_kn, v=w_v, o=w_o)


GATHERED = ("w_in", "w_uq", "w_ukv", "w_o_attn", "w_pw2", "w_out", "w_gu", "w_down")
ROW_SHARDED = ("w_out", "w_down")


def halves(a):
    return a.reshape(a.shape[:-2] + (2, a.shape[-2] // 2, a.shape[-1]))


def full_layer(name, gathered, l):
    n_c, _, _, kh, ns = gathered.shape
    if name in ROW_SHARDED:
        return gathered[:, l].reshape(n_c * 2 * kh, ns)
    return jnp.concatenate([gathered[s, l].reshape(2 * kh, ns) for s in range(n_c)], axis=1)


def owed_pieces(name, g):
    if name in ROW_SHARDED:
        return halves(g.reshape(N_CHIPS, g.shape[0] // N_CHIPS, g.shape[1]))
    k, n = g.shape
    return halves(jnp.transpose(g.reshape(k, N_CHIPS, n // N_CHIPS), (1, 0, 2)))


def _pad_rows8(flat):
    n = flat.shape[0]
    w = -(-n // (SUB * LANES)) * LANES
    return jnp.concatenate([flat, jnp.zeros((SUB * w - n,), flat.dtype)]).reshape(SUB, w)


def kernel(x, c, positions, w_ada, b_ada, g_mix, w_in, g_q, w_uq, g_kv, w_ukv, w_o_attn, w_dw, b_dw, g_cn, b_cn, w_pw2, w_out, g_ffn, w_gu, w_down, g_final, loss_target, m_w_ada, m_b_ada, m_g_mix, m_w_in, m_g_q, m_w_uq, m_g_kv, m_w_ukv, m_w_o_attn, m_w_dw, m_b_dw, m_g_cn, m_b_cn, m_w_pw2, m_w_out, m_g_ffn, m_w_gu, m_w_down, m_g_final, v_w_ada, v_b_ada, v_g_mix, v_w_in, v_g_q, v_w_uq, v_g_kv, v_w_ukv, v_w_o_attn, v_w_dw, v_b_dw, v_g_cn, v_b_cn, v_w_pw2, v_w_out, v_g_ffn, v_w_gu, v_w_down, v_g_final):
    weights = dict(w_ada=w_ada, b_ada=b_ada, g_mix=g_mix, w_in=w_in, g_q=g_q, w_uq=w_uq, g_kv=g_kv,
                   w_ukv=w_ukv, w_o_attn=w_o_attn, w_dw=w_dw, b_dw=b_dw, g_cn=g_cn, b_cn=b_cn,
                   w_pw2=w_pw2, w_out=w_out, g_ffn=g_ffn, w_gu=w_gu, w_down=w_down, g_final=g_final)
    mom = dict(w_ada=m_w_ada, b_ada=m_b_ada, g_mix=m_g_mix, w_in=m_w_in, g_q=m_g_q, w_uq=m_w_uq,
               g_kv=m_g_kv, w_ukv=m_w_ukv, w_o_attn=m_w_o_attn, w_dw=m_w_dw, b_dw=m_b_dw,
               g_cn=m_g_cn, b_cn=m_b_cn, w_pw2=m_w_pw2, w_out=m_w_out, g_ffn=m_g_ffn, w_gu=m_w_gu,
               w_down=m_w_down, g_final=m_g_final)
    var = dict(w_ada=v_w_ada, b_ada=v_b_ada, g_mix=v_g_mix, w_in=v_w_in, g_q=v_g_q, w_uq=v_w_uq,
               g_kv=v_g_kv, w_ukv=v_w_ukv, w_o_attn=v_w_o_attn, w_dw=v_w_dw, b_dw=v_b_dw,
               g_cn=v_g_cn, b_cn=v_b_cn, w_pw2=v_w_pw2, w_out=v_w_out, g_ffn=v_g_ffn, w_gu=v_w_gu,
               w_down=v_w_down, g_final=v_g_final)
    order = list(weights)

    n_e, S, D = x.shape
    T = n_e * S
    n_l = w_in.shape[0]
    QL, KVL, C = g_q.shape[1], g_kv.shape[1], g_cn.shape[1]
    n_h = w_uq.shape[2] * N_CHIPS // QK_DIM
    F = w_gu.shape[2] * N_CHIPS // 2
    B = n_e * N_DEV
    dims = (QL, KVL, C, D, n_h)
    scale = QK_DIM ** -0.5
    tm = _tile(S, 512, HALO)
    blk = _tile(S, ATTN_LB, LANES)
    tq = _tile(S, ATTN_TQ, blk)
    q_scale = scale / LN2
    mx, my, mc = _place()
    dev = 4 * mx + 2 * my + mc
    chip = 2 * mx + my
    c_idx = jnp.reshape(mc, (1,)).astype(jnp.int32)
    s_idx = jnp.reshape(chip, (1,)).astype(jnp.int32)

    xt = x.reshape(T, D)
    tgt = loss_target.reshape(T, D)

    mine = [halves(weights[k].astype(BF16)) for k in GATHERED]
    zero = jnp.zeros((), jnp.int32)
    gathered = {k: lax.dynamic_update_slice(g, w[None], (chip.astype(jnp.int32),) + (zero,) * 4)
                for k, g, w in zip(GATHERED, weight_allgather(mine), mine)}
    full = {k: [full_layer(k, gathered[k], l) for l in range(n_l)] for k in GATHERED}

    ns_ada = w_ada.shape[2]
    n_c, n_wdw = n_e * D, int(np.prod(w_dw.shape))
    small_all = allgather8("gather_c_wdw", _pad_rows8(
        jnp.concatenate([c.reshape(-1), w_dw.reshape(-1)]))).reshape(N_DEV, -1)
    c_all = small_all[:, :n_c].reshape(B, D)
    wdw_parts = small_all[:, n_c:n_c + n_wdw].reshape((N_DEV,) + w_dw.shape)
    w_dw_full = jnp.concatenate([wdw_parts[2 * s] for s in range(N_CHIPS)], axis=2)
    w_dw_pad = jnp.concatenate([w_dw_full, jnp.zeros((n_l, HALO - CONV_W, C), F32)], axis=1)

    c_act = silu_small(c_all)
    b_ada_mine = lax.dynamic_slice_in_dim(b_ada, chip * ns_ada, ns_ada, axis=1)
    mod_part = ada_fwd(c_act, w_ada, b_ada_mine)
    mod_all = allgather8("gather_mod", mod_part.reshape(n_l * B, ns_ada)).reshape(
        N_DEV, n_l, B, ns_ada)
    mod_full = jnp.concatenate([mod_all[2 * s] for s in range(N_CHIPS)], axis=2)
    mod_mine = lax.dynamic_slice_in_dim(mod_full, dev * n_e, n_e, axis=1)
    mods = mod_mine.reshape(n_l, n_e, N_MOD, 1, D)

    lay = [layout_weights(full["w_in"][l], full["w_uq"][l], full["w_ukv"][l], full["w_o_attn"][l], dims)
           for l in range(n_l)]

    lane = np.arange(LANES)
    in_rope = (lane >= NOPE) & (lane < QK_DIM)
    inv_freq = ROPE_THETA ** (-np.arange(0, ROPE, 2, dtype=np.float32) / ROPE)
    invf = np.where(in_rope, inv_freq[(lane - NOPE) % (ROPE // 2)], 0.0).astype(np.float32)
    cos_q, sin_q, cos_k = rope_tables(
        positions.astype(F32).reshape(T, 1), jnp.asarray(invf).reshape(1, LANES),
        jnp.asarray((lane < NOPE).astype(np.float32)).reshape(1, LANES),
        jnp.asarray(in_rope.astype(np.float32)).reshape(1, LANES), tm)

    def rope_epi(accs, ex):
        n = accs[0].shape[1]
        return (accs[0] * _lanes(ex[0], n) + accs[1] * _lanes(ex[1], n),)

    def rms_epi(accs, ex):
        a = accs[0]
        return a, a * _rstd(a) * ex[0]

    def modnorm(name, xin, g, sc, sh):
        def fn(xv, gv, scv, shv):
            return (xv * _rstd(xv) * gv * (1.0 + scv) + shv,)
        return rowwise(name, [('tile', xin), ('vec', g), ('exvec', sc), ('exvec', sh)],
                       [('tile', D, BF16)], fn, tm, S)[0]

    def modnorm_bwd(name, dh, xin, dres, g, sc):
        def fn(dhv, xv, drv, gv, scv):
            rstd = _rstd(xv)
            xhat = xv * rstd
            dx = _norm_bwd(dhv * gv * (1.0 + scv), xhat, rstd) + drv
            return dx, dhv, dhv * xhat * gv, dhv * xhat * (1.0 + scv)
        return rowwise(name, [('tile', dh), ('tile', xin), ('tile', dres), ('vec', g), ('exvec', sc)],
                       [('tile', D, F32), ('exacc', D), ('exacc', D), ('acc', D)], fn, tm, S)

    def rms_bwd(name, dy, xin, g, width):
        def fn(dyv, xv, gv):
            rstd = _rstd(xv)
            xhat = xv * rstd
            return _norm_bwd(dyv * gv, xhat, rstd), dyv * xhat
        return rowwise(name, [('tile', dy), ('tile', xin), ('vec', g)],
                       [('tile', width, BF16), ('acc', width)], fn, tm, S)

    def gate_bwd(name, dxo, branch, gt):
        def fn(dv, bv, gv):
            return dv * gv, dv * bv
        return rowwise(name, [('tile', dxo), ('tile', branch), ('exvec', gt)],
                       [('tile', D, BF16), ('exacc', D)], fn, tm, S)

    tn_d = _tile(D, 512, LANES)
    tn_f = _tile(F, 1536, LANES)
    tn_s = _tile(n_h * SLOT, 512, LANES)
    row = lambda a: a.reshape(1, -1)

    saved = []
    xc = xt
    for l in range(n_l):
        W = lay[l]
        sh1, sc1, gt1, sh2, sc2, gt2 = [mods[l, :, k] for k in range(N_MOD)]
        h1 = modnorm("modnorm_mix", xc, row(g_mix[l]), sc1, sh1)
        ga, gb = mm_fused("proj_gates", [h1], [(0, W["ga"]), (0, W["gb"])], [], lambda a, e: a,
                          [F32, F32], tm, tn_d, S)
        cu, glu_a, glu_b = mm_fused(
            "proj_glu", [h1], [(0, W["glu_a"]), (0, W["glu_b"])], [],
            lambda a, e: (a[0] * _sig(a[1]), a[0], a[1]), [F32, F32, F32], tm, C, S)
        q_lat, qn = mm_fused("proj_q_lat", [h1], [(0, W["ql"])], [('vec', row(g_q[l]))], rms_epi,
                             [F32, BF16], tm, QL, S)
        kv_lat, kvn = mm_fused("proj_kv_lat", [h1], [(0, W["kvl"])], [('vec', row(g_kv[l]))],
                               rms_epi, [F32, BF16], tm, KVL, S)
        kr = mm_fused("proj_k_rope", [h1], [(0, W["kr_a"]), (0, W["kr_b"])],
                      [('row128', cos_k), ('row128', sin_q)], rope_epi, [F32], tm, SLOT, S)[0]
        q_all = mm_fused("q_up", [qn], [(0, W["q_a"]), (0, W["q_b"])],
                         [('row128', cos_q), ('row128', sin_q)],
                         lambda a, e: (rope_epi(a, e)[0] * q_scale,), [BF16], tm, tn_s, S)[0]
        k_all, v_all = mm_fused(
            "kv_up", [kvn], [(0, W["kn"]), (0, W["v"])], [('row128', kr)],
            lambda a, e: (a[0] + _lanes(e[0], a[0].shape[1]), a[1]), [BF16, BF16], tm, tn_s, S)
        o_all, lse = attn_fwd(q_all, k_all, v_all, n_e, S, n_h, tq, blk)
        u, s_act = conv_fwd(cu, w_dw_pad[l], row(b_dw[l]), row(g_cn[l]), row(b_cn[l]), tm, S)

        def merge_epi(a, e):
            return _sig(e[0]) * a[0] + _sig(e[1]) * a[1], a[0], a[1]
        y, ya, yc = mm_fused("merge", [o_all, s_act], [(0, W["o"]), (1, full["w_pw2"][l])],
                             [('tile', ga), ('tile', gb)], merge_epi, [BF16, F32, F32], tm, tn_d, S)
        x2, o_mix = mm_fused("mix_out", [y], [(0, full["w_out"][l])], [('tile', xc), ('exvec', gt1)],
                             lambda a, e: (e[0] + e[1] * a[0], a[0]), [F32, F32], tm, tn_d, S)
        h2 = modnorm("modnorm_ffn", x2, row(g_ffn[l]), sc2, sh2)

        def swiglu_epi(a, e):
            return a[0], a[1], a[0] * _sig(a[0]) * a[1]
        w_gu_l = full["w_gu"][l]
        g_act, up, act = mm_fused("ffn_up", [h2], [(0, w_gu_l), (0, w_gu_l, False, F // tn_f)], [],
                                  swiglu_epi, [BF16, BF16, BF16], tm, tn_f, S, N=F)
        x3, dn = mm_fused("ffn_down", [act], [(0, full["w_down"][l])], [('tile', x2), ('exvec', gt2)],
                          lambda a, e: (e[0] + e[1] * a[0], a[0]), [F32, F32], tm, tn_d, S)
        saved.append(dict(x=xc, h1=h1, ga=ga, gb=gb, cu=cu, glu_a=glu_a, glu_b=glu_b, q_lat=q_lat,
                          qn=qn, kv_lat=kv_lat, kvn=kvn, q_all=q_all, k_all=k_all, v_all=v_all,
                          o_all=o_all, lse=lse, u=u, s_act=s_act, y=y, ya=ya, yc=yc, x2=x2,
                          o_mix=o_mix, h2=h2, g_act=g_act, up=up, act=act, dn=dn))
        xc = x3

    def loss_fn(xv, tv, gv):
        rstd = _rstd(xv)
        xhat = xv * rstd
        err = xhat * gv - tv
        dy = err * (1.0 / D)
        return _norm_bwd(dy * gv, xhat, rstd), err * err * (0.5 / D), dy * xhat
    dxc, loss_acc, dg_final = rowwise("loss_head", [('tile', xc), ('tile', tgt), ('vec', row(g_final))],
                                      [('tile', D, F32), ('acc', D), ('acc', D)], loss_fn, tm, S)

    gfull = {k: [None] * n_l for k in GATHERED}
    g_wdw = [None] * n_l
    small_acc = {k: [None] * n_l for k in ("g_mix", "g_q", "g_kv", "b_dw", "g_cn", "b_cn", "g_ffn")}
    dmod_acc = [None] * n_l
    lay_T = jax.linear_transpose(
        lambda a, b, cc, d: layout_weights(a, b, cc, d, dims),
        *[jax.ShapeDtypeStruct(full[k][0].shape, F32) for k in ("w_in", "w_uq", "w_ukv", "w_o_attn")])

    for l in reversed(range(n_l)):
        sv = saved[l]
        W = lay[l]
        sh1, sc1, gt1, sh2, sc2, gt2 = [mods[l, :, k] for k in range(N_MOD)]
        w_gu_l = full["w_gu"][l]
        ddn, dgt2 = gate_bwd("ffn_gate_bwd", dxc, sv["dn"], gt2)

        def swiglu_bwd_epi(a, e):
            gv, uv = e[0].astype(F32), e[1].astype(F32)
            sg = _sig(gv)
            return a[0] * uv * sg * (1.0 + gv * (1.0 - sg)), a[0] * gv * sg
        dg_act, dup = mm_fused("ffn_down_bwd", [ddn], [(0, full["w_down"][l], True)],
                               [('tile', sv["g_act"]), ('tile', sv["up"])], swiglu_bwd_epi,
                               [BF16, BF16], tm, tn_f, S)
        gfull["w_down"][l] = mm_tn("ffn_down_dw", sv["act"], ddn)
        gfull["w_gu"][l] = jnp.concatenate([mm_tn("ffn_gate_dw", sv["h2"], dg_act),
                                            mm_tn("ffn_up_dw", sv["h2"], dup)], axis=1)
        dh2 = mm_fused("ffn_up_bwd", [dg_act, dup], [(0, w_gu_l, True, 0), (1, w_gu_l, True, 1)], [],
                       lambda a, e: (a[0] + a[1],), [F32], tm, tn_d, S)[0]
        dx2, dsh2, dsc2, dg_ffn = modnorm_bwd("modnorm_ffn_bwd", dh2, sv["x2"], dxc,
                                              row(g_ffn[l]), sc2)
        ddo, dgt1 = gate_bwd("mix_gate_bwd", dx2, sv["o_mix"], gt1)

        def merge_bwd_epi(a, e):
            sa, sb = _sig(e[0]), _sig(e[1])
            dy = a[0]
            return dy * sa, dy * sb, dy * e[2] * sa * (1.0 - sa), dy * e[3] * sb * (1.0 - sb)
        dya, dyc, dga, dgb = mm_fused(
            "mix_out_bwd", [ddo], [(0, full["w_out"][l], True)],
            [('tile', sv["ga"]), ('tile', sv["gb"]), ('tile', sv["ya"]), ('tile', sv["yc"])],
            merge_bwd_epi, [BF16] * 4, tm, tn_d, S)
        gfull["w_out"][l] = mm_tn("mix_out_dw", sv["y"], ddo)
        do_all = mm_fused("attn_out_bwd", [dya], [(0, W["o"], True)], [], lambda a, e: a, [BF16],
                          tm, tn_s, S)[0]
        d_wo = mm_tn("attn_out_dw", sv["o_all"], dya)
        ds_act = mm_fused("conv_out_bwd", [dyc], [(0, full["w_pw2"][l], True)], [], lambda a, e: a,
                          [F32], tm, C, S)[0]
        gfull["w_pw2"][l] = mm_tn("conv_out_dw", sv["s_act"], dyc)

        def ln_silu_bwd(dsv, uv, gv, bv):
            xhat, rstd = _layer_norm_parts(uv)
            ln = xhat * gv + bv
            sg = _sig(ln)
            dln = dsv * sg * (1.0 + ln * (1.0 - sg))
            dxhat = dln * gv
            du_ = rstd * (dxhat - jnp.mean(dxhat, axis=-1, keepdims=True)
                          - xhat * jnp.mean(dxhat * xhat, axis=-1, keepdims=True))
            return du_, dln * xhat, dln
        du, dg_cn, db_cn = rowwise(
            "conv_norm_bwd", [('tile', ds_act), ('tile', sv["u"]), ('vec', row(g_cn[l])),
                              ('vec', row(b_cn[l]))],
            [('tile', C, F32), ('acc', C), ('acc', C)], ln_silu_bwd, tm, S)
        dglu_a, dglu_b, dw_acc, db_dw = conv_bwd(du, sv["cu"], sv["glu_a"], sv["glu_b"],
                                                 w_dw_pad[l], tm, S)
        g_wdw[l] = group_sum("conv_dw_rows", dw_acc, SUB)[:CONV_W]
        dk_all, dv_all, gq_a, gq_b = attn_bwd(sv["q_all"], sv["k_all"], sv["v_all"], sv["o_all"],
                                              do_all, sv["lse"], cos_q, sin_q, n_e, S, n_h, blk, scale)
        dqn = mm_fused("q_up_bwd", [gq_a, gq_b], [(0, W["q_a"], True), (1, W["q_b"], True)], [],
                       lambda a, e: (a[0] + a[1],), [F32], tm, QL, S)[0]
        d_wqa = mm_tn("q_up_dw_a", sv["qn"], gq_a)
        d_wqb = mm_tn("q_up_dw_b", sv["qn"], gq_b)
        dq_lat, dg_q = rms_bwd("q_norm_bwd", dqn, sv["q_lat"], row(g_q[l]), QL)

        def k_split(dkv, ckv, skv):
            tot = dkv[:, :SLOT]
            for h in range(1, n_h):
                tot = tot + dkv[:, h * SLOT:(h + 1) * SLOT]
            return dkv, tot * ckv, tot * skv
        dk_b, dkr_a, dkr_b = rowwise("k_rope_bwd", [('tile', dk_all), ('tile', cos_k), ('tile', sin_q)],
                                     [('tile', n_h * SLOT, BF16), ('tile', SLOT, BF16),
                                      ('tile', SLOT, BF16)], k_split, tm, S)
        dkvn = mm_fused("kv_up_bwd", [dk_b, dv_all], [(0, W["kn"], True), (1, W["v"], True)], [],
                        lambda a, e: (a[0] + a[1],), [F32], tm, KVL, S)[0]
        d_wkn = mm_tn("kv_up_dw_k", sv["kvn"], dk_b)
        d_wv = mm_tn("kv_up_dw_v", sv["kvn"], dv_all)
        dkv_lat, dg_kv = rms_bwd("kv_norm_bwd", dkvn, sv["kv_lat"], row(g_kv[l]), KVL)
        segs = [("ga", dga), ("gb", dgb), ("glu_a", dglu_a), ("glu_b", dglu_b), ("ql", dq_lat),
                ("kvl", dkv_lat), ("kr_a", dkr_a), ("kr_b", dkr_b)]
        dh1 = mm_fused("proj_bwd", [g for _, g in segs],
                       [(k, W[nm], True) for k, (nm, _) in enumerate(segs)],
                       [], lambda a, e: (functools.reduce(lambda p, q: p + q, a),), [F32],
                       tm, tn_d, S)[0]
        d_lay = {nm: mm_tn("proj_dw_" + nm, sv["h1"], g) for nm, g in segs}
        d_lay.update(q_a=d_wqa, q_b=d_wqb, kn=d_wkn, v=d_wv, o=d_wo)
        (gfull["w_in"][l], gfull["w_uq"][l], gfull["w_ukv"][l],
         gfull["w_o_attn"][l]) = lay_T({k: d_lay[k] for k in lay[l]})
        dxc, dsh1, dsc1, dg_mix = modnorm_bwd("modnorm_mix_bwd", dh1, sv["x"], dx2,
                                              row(g_mix[l]), sc1)
        dmod_acc[l] = [dsh1, dsc1, dgt1, dsh2, dsc2, dgt2]
        for k, a in (("g_mix", dg_mix), ("g_q", dg_q), ("g_kv", dg_kv), ("b_dw", db_dw),
                     ("g_cn", dg_cn), ("b_cn", db_cn), ("g_ffn", dg_ffn)):
            small_acc[k][l] = a

    grad_x = dxc.reshape(n_e, S, D)

    dmod_rows = jnp.concatenate([a for l in range(n_l) for a in dmod_acc[l]], axis=0)
    dmod_own = group_sum("dmod_rows", dmod_rows, SUB).reshape(n_l, N_MOD, n_e, D)
    dmod_own = jnp.transpose(dmod_own, (0, 2, 1, 3)).reshape(n_l * n_e, N_MOD * D)
    dmod_all = allgather8("gather_dmod", dmod_own).reshape(N_DEV, n_l, n_e, N_MOD * D)
    dmod_all = jnp.transpose(dmod_all, (1, 0, 2, 3)).reshape(n_l, B, N_MOD * D)
    dmod_mine = lax.dynamic_slice_in_dim(dmod_all, chip * ns_ada, ns_ada, axis=2)
    grad_w_ada = ada_bwd_w(jnp.transpose(c_act), dmod_mine)
    grad_b_ada = group_sum("grad_b_ada", dmod_all.reshape(n_l * B, N_MOD * D), B)

    owed = [jnp.stack([owed_pieces(k, gfull[k][l]) for l in range(n_l)], axis=1)
            for k in GATHERED]
    parts = [chip_partial("grad_chip_partial_" + k, g, a, c_idx)
             for k, g, a in zip(GATHERED, owed, sibling_swap(owed))]
    mine_half = [shard_total("grad_shard_total_" + k, p, b, s_idx)
                 for k, p, b in zip(GATHERED, parts, chip_exchange(parts))]
    red = {}
    for k, half, other in zip(GATHERED, mine_half, sibling_share(mine_half)):
        n_lk, kh, ns = half.shape
        both = lax.dynamic_update_slice(
            jnp.broadcast_to(other[:, None], (n_lk, 2, kh, ns)), half[:, None],
            (zero, mc.astype(jnp.int32), zero, zero))
        red[k] = both.reshape(n_lk, 2 * kh, ns)

    wdw_full_g = jnp.stack(g_wdw)
    pieces = [loss_acc, dg_final] + [small_acc[k][l] for k in small_acc for l in range(n_l)]
    widths = [p.shape[1] for p in pieces]
    n_acc = sum(widths)
    wdw_blk = _pad_rows8(wdw_full_g.reshape(-1))
    wdw_w = wdw_blk.shape[1]
    gathered_small = allgather8("gather_small_grads", jnp.concatenate(pieces + [wdw_blk], axis=1))
    acc_sum = group_sum("small_total", gathered_small[:, :, :n_acc].reshape(N_DEV * SUB, n_acc),
                        N_DEV * SUB)
    wdw_sum = group_sum("wdw_total", gathered_small[:, :, n_acc:].reshape(N_DEV, SUB * wdw_w), N_DEV)
    offs = np.cumsum([0] + widths)
    take = lambda i: acc_sum[:, offs[i]:offs[i + 1]]
    loss = lane_total("loss_total", take(0))[0, 0]
    g_small = {"g_final": take(1).reshape(-1)}
    i = 2
    for k in small_acc:
        g_small[k] = jnp.concatenate([take(i + l) for l in range(n_l)], axis=0)
        i += n_l
    wdw_total = wdw_sum.reshape(-1)[:n_wdw * N_CHIPS].reshape(wdw_full_g.shape)
    grad_w_dw = lax.dynamic_slice_in_dim(wdw_total, chip * w_dw.shape[2], w_dw.shape[2], axis=2)

    grads = dict(w_ada=grad_w_ada, b_ada=grad_b_ada, w_dw=grad_w_dw, **g_small, **red)

    deltas, new_m, new_v = {}, {}, {}
    for k in order:
        shp = weights[k].shape
        two = (1, shp[0]) if len(shp) == 1 else (int(np.prod(shp[:-1])), shp[-1])
        d, nm, nv = adamw("adamw_" + k, weights[k].reshape(two), grads[k].reshape(two),
                          mom[k].reshape(two), var[k].reshape(two))
        deltas[k], new_m[k], new_v[k] = d.reshape(shp), nm.reshape(shp), nv.reshape(shp)
        grads[k] = grads[k].reshape(shp)

    return (loss, grad_x, *[grads[k] for k in order], *[deltas[k] for k in order],
            *[new_m[k] for k in order], *[new_v[k] for k in order])
```

```python
import functools

import numpy as np
import jax
import jax.numpy as jnp
from jax import lax
from jax.experimental import pallas as pl
from jax.experimental.pallas import tpu as pltpu

F32 = jnp.float32
BF16 = jnp.bfloat16
MESH = pl.DeviceIdType.MESH

EPS = 1e-6
NEG_INF = -1e30
NOPE, ROPE, VDIM = 64, 32, 64
QK_DIM = NOPE + ROPE
SLOT = 128
CONV_W = 31
HALO = 32
N_MOD = 6
ROPE_THETA = 10000.0
N_CHIPS = 4
N_DEV = 8
SUB = 8
LANES = 128
VMEM_LIMIT = 56 * 1024 * 1024

ADAM_LR, ADAM_B1, ADAM_B2, ADAM_EPS, ADAM_WD, ADAM_STEP = 0.001, 0.9, 0.999, 1e-08, 0.01, 10


def _tile(n, cap, mult):
    best = None
    for d in range(mult, min(n, cap) + 1, mult):
        if n % d == 0:
            best = d
    return best if best is not None else n


def _params(sem):
    return pltpu.CompilerParams(dimension_semantics=sem, vmem_limit_bytes=VMEM_LIMIT)


def _sig(x):
    return 1.0 / (1.0 + jnp.exp(-x))


def _sum8(x):
    r, w = x.shape
    return jnp.sum(x.reshape(r // SUB, SUB, w), axis=0)


def _lanes(v, n):
    return v if n == v.shape[1] else jnp.tile(v, (1, n // v.shape[1]))


def _rstd(x):
    return lax.rsqrt(jnp.mean(x * x, axis=-1, keepdims=True) + EPS)


def _norm_bwd(dxhat, xhat, rstd):
    return rstd * (dxhat - xhat * jnp.mean(dxhat * xhat, axis=-1, keepdims=True))


def mm_fused(name, As, pairs, extras, epilogue, out_dtypes, tm, tn, S, N=None):
    T = As[0].shape[0]
    pairs = [(p[0], p[1], p[2] if len(p) > 2 else False, p[3] if len(p) > 3 else 0) for p in pairs]
    if N is None:
        N = pairs[0][1].shape[0] if pairs[0][2] else pairs[0][1].shape[1]
    nex = S // tm
    in_specs, args = [], []
    for a in As:
        in_specs.append(pl.BlockSpec((tm, a.shape[1]), lambda i, j: (i, 0)))
        args.append(a)
    for ai, b, trans, off in pairs:
        kdim = As[ai].shape[1]
        if trans:
            in_specs.append(pl.BlockSpec((tn, kdim), lambda i, j, off=off: (j, off)))
        else:
            in_specs.append(pl.BlockSpec((kdim, tn), lambda i, j, off=off: (0, j + off)))
        args.append(b)
    for kind, arr in extras:
        if kind == 'tile':
            in_specs.append(pl.BlockSpec((tm, tn), lambda i, j: (i, j)))
        elif kind == 'row128':
            in_specs.append(pl.BlockSpec((tm, LANES), lambda i, j: (i, 0)))
        elif kind == 'vec':
            in_specs.append(pl.BlockSpec((1, tn), lambda i, j: (0, j)))
        else:
            in_specs.append(pl.BlockSpec((None, 1, tn), lambda i, j: (i // nex, 0, j)))
        args.append(arr)
    n_a, n_p, n_e = len(As), len(pairs), len(extras)

    def body(*refs):
        a_refs, b_refs = refs[:n_a], refs[n_a:n_a + n_p]
        e_refs, o_refs = refs[n_a + n_p:n_a + n_p + n_e], refs[n_a + n_p + n_e:]
        accs = [lax.dot_general(a_refs[ai][...], b_refs[k][...], NT if trans else NN,
                                preferred_element_type=F32)
                for k, (ai, _, trans, _) in enumerate(pairs)]
        outs = epilogue(accs, [r[...] for r in e_refs])
        for o_ref, o in zip(o_refs, outs):
            o_ref[...] = o.astype(o_ref.dtype)

    return pl.pallas_call(
        body, name=name, grid=(T // tm, N // tn), in_specs=in_specs,
        out_specs=[pl.BlockSpec((tm, tn), lambda i, j: (i, j)) for _ in out_dtypes],
        out_shape=[jax.ShapeDtypeStruct((T, N), dt) for dt in out_dtypes],
        compiler_params=_params(("parallel", "parallel")))(*args)


def mm_tn(name, A, G):
    T, K = A.shape
    N = G.shape[1]
    tt = _tile(T, 512, 16)
    tk = _tile(K, 1536, LANES)
    tn = _tile(N, 1536, LANES)

    def body(a_ref, g_ref, o_ref):
        part = lax.dot_general(a_ref[...], g_ref[...], (((0,), (0,)), ((), ())),
                               preferred_element_type=F32)

        @pl.when(pl.program_id(2) == 0)
        def _():
            o_ref[...] = part

        @pl.when(pl.program_id(2) > 0)
        def _():
            o_ref[...] += part

    return pl.pallas_call(
        body, name=name, grid=(K // tk, N // tn, T // tt),
        in_specs=[pl.BlockSpec((tt, tk), lambda k, n, t: (t, k)),
                  pl.BlockSpec((tt, tn), lambda k, n, t: (t, n))],
        out_specs=pl.BlockSpec((tk, tn), lambda k, n, t: (k, n)),
        out_shape=jax.ShapeDtypeStruct((K, N), F32),
        compiler_params=_params(("parallel", "parallel", "arbitrary")))(A, G)


def rowwise(name, ins, outs, fn, tm, S):
    T = next(a.shape[0] for k, a in ins if k == 'tile')
    nex = S // tm
    n_ex = T // S
    in_specs, args = [], []
    for kind, arr in ins:
        if kind == 'tile':
            in_specs.append(pl.BlockSpec((tm, arr.shape[1]), lambda i: (i, 0)))
        elif kind == 'vec':
            in_specs.append(pl.BlockSpec((1, arr.shape[1]), lambda i: (0, 0)))
        else:
            in_specs.append(pl.BlockSpec((None, 1, arr.shape[2]), lambda i: (i // nex, 0, 0)))
        args.append(arr)
    out_specs, out_shape = [], []
    for o in outs:
        if o[0] == 'tile':
            out_specs.append(pl.BlockSpec((tm, o[1]), lambda i: (i, 0)))
            out_shape.append(jax.ShapeDtypeStruct((T, o[1]), o[2]))
        elif o[0] == 'acc':
            out_specs.append(pl.BlockSpec((SUB, o[1]), lambda i: (0, 0)))
            out_shape.append(jax.ShapeDtypeStruct((SUB, o[1]), F32))
        else:
            out_specs.append(pl.BlockSpec((SUB, o[1]), lambda i: (i // nex, 0)))
            out_shape.append(jax.ShapeDtypeStruct((n_ex * SUB, o[1]), F32))
    n_in = len(ins)

    def body(*refs):
        i = pl.program_id(0)
        vals = fn(*[r[...] for r in refs[:n_in]])
        for o, o_ref, v in zip(outs, refs[n_in:], vals):
            if o[0] == 'tile':
                o_ref[...] = v.astype(o_ref.dtype)
            else:
                part = _sum8(v)
                first = (i == 0) if o[0] == 'acc' else (i % nex == 0)

                @pl.when(first)
                def _(o_ref=o_ref, part=part):
                    o_ref[...] = part

                @pl.when(jnp.logical_not(first))
                def _(o_ref=o_ref, part=part):
                    o_ref[...] += part

    return pl.pallas_call(
        body, name=name, grid=(T // tm,), in_specs=in_specs, out_specs=out_specs,
        out_shape=out_shape, compiler_params=_params(("arbitrary",)))(*args)


NN = (((1,), (0,)), ((), ()))
NT = (((1,), (1,)), ((), ()))
TN = (((0,), (0,)), ((), ()))
LN2 = 0.6931471805599453
ATTN_TQ = 1024
ATTN_LB = 512


def _hosted(job, n_in, n_out, grid):
    if job is None:
        return [], [], [], [], (lambda refs: None), (lambda refs: None), []
    ins, outs, n_sems, start, finish = job
    hbm = pl.BlockSpec(memory_space=pl.ANY)
    n_j = len(ins)

    def split(refs):
        j_in = refs[n_in:n_in + n_j]
        j_out = refs[n_in + n_j + n_out:n_in + n_j + n_out + n_j]
        send_sems, recv_sems = refs[-2:]

        def copy(k, src, dst, to):
            return pltpu.make_async_remote_copy(src_ref=src, dst_ref=dst, send_sem=send_sems.at[k],
                                                recv_sem=recv_sems.at[k], device_id=to,
                                                device_id_type=MESH)
        return j_in, j_out, copy

    def at(step_of):
        cond = None
        for axis, size in enumerate(grid):
            hit = pl.program_id(axis) == step_of(size)
            cond = hit if cond is None else jnp.logical_and(cond, hit)
        return cond

    def begin(refs):
        @pl.when(at(lambda size: 0))
        def _():
            start(*split(refs))

    def end(refs):
        @pl.when(at(lambda size: size - 1))
        def _():
            finish(*split(refs))

    sems = [pltpu.SemaphoreType.DMA((n_sems,)), pltpu.SemaphoreType.DMA((n_sems,))]
    return [hbm] * n_j, [hbm] * n_j, list(outs), sems, begin, end, list(ins)


def attn_fwd(Q, K, V, n_b, S, n_h, tq, lb, job=None):
    T = n_b * S
    nq = S // tq
    ratio = tq // lb
    tk = lb
    grid = (n_b, n_h, nq)
    j_in, j_out, j_shapes, j_scratch, begin, end, j_args = _hosted(job, 3, 2, grid)
    n_j = len(j_args)

    def body(*refs):
        q_ref, k_ref, v_ref = refs[:3]
        o_ref, lse_ref = refs[3 + n_j:5 + n_j]
        m_s, l_s, acc_s = refs[5 + 2 * n_j:8 + 2 * n_j]
        begin(refs)
        i = pl.program_id(2)
        m_s[...] = jnp.full(m_s.shape, NEG_INF, F32)
        l_s[...] = jnp.zeros(l_s.shape, F32)
        acc_s[...] = jnp.zeros(acc_s.shape, F32)

        def kv_step(j, diag_off, rsplit):
            start = pl.multiple_of(j * tk, tk)
            k = k_ref[pl.ds(start, tk), :]
            v = v_ref[pl.ds(start, tk), :]
            rc = tq // rsplit
            for r in range(rsplit):
                if diag_off is not None and diag_off > r * rc + rc - 1:
                    continue
                rows = pl.ds(r * rc, rc)
                s = lax.dot_general(q_ref[rows, :], k, NT, preferred_element_type=F32)
                if diag_off is not None and diag_off + tk - 1 > r * rc:
                    rr = lax.broadcasted_iota(jnp.int32, s.shape, 0) + r * rc
                    cc = lax.broadcasted_iota(jnp.int32, s.shape, 1) + diag_off
                    s = jnp.where(rr >= cc, s, NEG_INF)
                m_prev = m_s[rows, :]
                m_new = jnp.maximum(m_prev, jnp.max(s, axis=1, keepdims=True))
                alpha = jnp.exp2(m_prev - m_new)
                p = jnp.exp2(s - _lanes(m_new, tk))
                l_s[rows, :] = alpha * l_s[rows, :] + jnp.sum(p, axis=1, keepdims=True)
                acc_s[rows, :] = alpha * acc_s[rows, :] + jnp.dot(p.astype(BF16), v,
                                                                  preferred_element_type=F32)
                m_s[rows, :] = m_new

        def below_diagonal(j, carry):
            kv_step(j, None, 1)
            return carry
        lax.fori_loop(0, i * ratio, below_diagonal, 0)
        for d in range(ratio):
            kv_step(i * ratio + d, d * tk, ratio)
        l = l_s[...]
        o_ref[...] = (acc_s[...] / l).astype(o_ref.dtype)
        lse = m_s[...] + jnp.log(l) * (1.0 / LN2)
        for u in range(ratio):
            lse_ref[u] = jnp.transpose(lse[u * lb:(u + 1) * lb, :])[:SUB, :]
        end(refs)

    qmap = lambda b, h, i: (b * nq + i, h)
    kmap = lambda b, h, i: (b, h)
    return pl.pallas_call(
        body, name="attn_fwd", grid=grid,
        in_specs=[pl.BlockSpec((tq, SLOT), qmap), pl.BlockSpec((S, SLOT), kmap),
                  pl.BlockSpec((S, SLOT), kmap)] + j_in,
        out_specs=[pl.BlockSpec((tq, SLOT), qmap),
                   pl.BlockSpec((None, ratio, SUB, lb), lambda b, h, i: (b * n_h + h, i, 0, 0))] + j_out,
        out_shape=[jax.ShapeDtypeStruct((T, n_h * SLOT), BF16),
                   jax.ShapeDtypeStruct((n_b * n_h, S // lb, SUB, lb), F32)] + j_shapes,
        scratch_shapes=[pltpu.VMEM((tq, SLOT), F32)] * 3 + j_scratch,
        compiler_params=_params(("arbitrary", "arbitrary", "arbitrary")))(Q, K, V, *j_args)


def attn_bwd(Q, K, V, O, dO, LSE, cosq, sinq, n_b, S, n_h, blk, scale, job=None):
    T = n_b * S
    nb = S // blk
    grid = (n_b, n_h, nb)
    j_in, j_out, j_shapes, j_scratch, begin, end, j_args = _hosted(job, 8, 4, grid)
    n_j = len(j_args)

    def body(*refs):
        q_ref, k_ref, v_ref, o_ref, do_ref, lse_ref, cos_ref, sin_ref = refs[:8]
        dk_ref, dv_ref, ga_ref, gb_ref = refs[8 + n_j:12 + n_j]
        dq_s, delta_s, dk_s, dv_s = refs[12 + 2 * n_j:16 + 2 * n_j]
        begin(refs)
        j = pl.program_id(2)

        @pl.when(j == 0)
        def _():
            dq_s[...] = jnp.zeros(dq_s.shape, F32)
            for i in range(nb):
                rows = pl.ds(i * blk, blk)
                d = jnp.sum(do_ref[rows, :].astype(F32) * o_ref[rows, :].astype(F32),
                            axis=1, keepdims=True)
                delta_s[i] = jnp.transpose(jnp.broadcast_to(d, (blk, SLOT)))[:SUB, :]

        k = k_ref[...]
        v = v_ref[...]
        dk_s[...] = jnp.zeros(dk_s.shape, F32)
        dv_s[...] = jnp.zeros(dv_s.shape, F32)

        def q_step(i, masked):
            rows = pl.ds(pl.multiple_of(i * blk, blk), blk)
            q = q_ref[rows, :]
            do = do_ref[rows, :]
            st = lax.dot_general(k, q, NT, preferred_element_type=F32)
            if masked:
                kv_i = lax.broadcasted_iota(jnp.int32, st.shape, 0)
                q_i = lax.broadcasted_iota(jnp.int32, st.shape, 1)
                st = jnp.where(q_i >= kv_i, st, NEG_INF)
            pt = jnp.exp2(st - lse_ref[i][:1, :])
            dpt = lax.dot_general(v, do, NT, preferred_element_type=F32)
            dst = (pt * (dpt - delta_s[i][:1, :])).astype(BF16)
            dv_s[...] += jnp.dot(pt.astype(BF16), do, preferred_element_type=F32)
            dk_s[...] += jnp.dot(dst, q, preferred_element_type=F32)
            dq_s[rows, :] += lax.dot_general(dst, k, TN, preferred_element_type=F32)

        q_step(j, True)

        def above_diagonal(i, carry):
            q_step(i, False)
            return carry
        lax.fori_loop(j + 1, nb, above_diagonal, 0)
        dk_ref[...] = dk_s[...] * LN2
        dv_ref[...] = dv_s[...].astype(dv_ref.dtype)

        @pl.when(j == nb - 1)
        def _():
            dq = dq_s[...] * scale
            ga_ref[...] = (dq * cos_ref[...]).astype(ga_ref.dtype)
            gb_ref[...] = (dq * sin_ref[...]).astype(gb_ref.dtype)

        end(refs)

    full = pl.BlockSpec((S, SLOT), lambda b, h, j: (b, h))
    kv = pl.BlockSpec((blk, SLOT), lambda b, h, j: (b * nb + j, h))
    tab = pl.BlockSpec((S, SLOT), lambda b, h, j: (b, 0))
    stat = pl.BlockSpec((None, nb, SUB, blk), lambda b, h, j: (b * n_h + h, 0, 0, 0))
    return pl.pallas_call(
        body, name="attn_bwd", grid=grid,
        in_specs=[full, kv, kv, full, full, stat, tab, tab] + j_in,
        out_specs=[kv, kv, full, full] + j_out,
        out_shape=[jax.ShapeDtypeStruct((T, n_h * SLOT), F32)]
        + [jax.ShapeDtypeStruct((T, n_h * SLOT), BF16)] * 3 + j_shapes,
        scratch_shapes=[pltpu.VMEM((S, SLOT), F32), pltpu.VMEM((nb, SUB, blk), F32),
                        pltpu.VMEM((blk, SLOT), F32), pltpu.VMEM((blk, SLOT), F32)] + j_scratch,
        compiler_params=_params(("arbitrary", "arbitrary", "arbitrary")))(
            Q, K, V, O, dO, LSE, cosq, sinq, *j_args)


def _layer_norm_parts(u):
    xc = u - jnp.mean(u, axis=-1, keepdims=True)
    rstd = lax.rsqrt(jnp.mean(xc * xc, axis=-1, keepdims=True) + EPS)
    return xc * rstd, rstd


def conv_fwd(cu, w, b_dw, g_cn, b_cn, tm, S):
    T, C = cu.shape
    nex, hb = S // tm, tm // HALO

    def body(cur_ref, prev_ref, w_ref, b_ref, g_ref, bc_ref, u_ref, s_ref, ext):
        first = pl.program_id(0) % nex == 0
        ext[pl.ds(0, HALO), :] = jnp.where(first, 0.0, prev_ref[...])
        ext[pl.ds(HALO, tm), :] = cur_ref[...]
        acc = jnp.zeros((tm, C), F32)
        for j in range(CONV_W):
            acc = acc + w_ref[pl.ds(j, 1), :] * ext[pl.ds(HALO - CONV_W + 1 + j, tm), :]
        u = acc + b_ref[...]
        ln = _layer_norm_parts(u)[0] * g_ref[...] + bc_ref[...]
        u_ref[...] = u
        s_ref[...] = (ln * _sig(ln)).astype(s_ref.dtype)

    vec = pl.BlockSpec((1, C), lambda i: (0, 0))
    return pl.pallas_call(
        body, name="conv_fwd", grid=(T // tm,),
        in_specs=[pl.BlockSpec((tm, C), lambda i: (i, 0)),
                  pl.BlockSpec((HALO, C), lambda i: (jnp.maximum(i * hb - 1, 0), 0)),
                  pl.BlockSpec((HALO, C), lambda i: (0, 0)), vec, vec, vec],
        out_specs=[pl.BlockSpec((tm, C), lambda i: (i, 0))] * 2,
        out_shape=[jax.ShapeDtypeStruct((T, C), F32), jax.ShapeDtypeStruct((T, C), BF16)],
        scratch_shapes=[pltpu.VMEM((HALO + tm, C), F32)],
        compiler_params=_params(("arbitrary",)))(cu, cu, w, b_dw, g_cn, b_cn)


def conv_bwd(du, cu, glu_a, glu_b, w, tm, S):
    T, C = du.shape
    nex, hb = S // tm, tm // HALO
    last_blk = T // HALO - 1

    def body(du_ref, nxt_ref, cu_ref, prev_ref, a_ref, b_ref, w_ref,
             da_ref, db_ref, dw_ref, dbias_ref, extd, extc):
        i = pl.program_id(0)
        first = i % nex == 0
        last = i % nex == nex - 1
        du_cur = du_ref[...]
        extd[pl.ds(0, tm), :] = du_cur
        extd[pl.ds(tm, HALO), :] = jnp.where(last, 0.0, nxt_ref[...])
        extc[pl.ds(0, HALO), :] = jnp.where(first, 0.0, prev_ref[...])
        extc[pl.ds(HALO, tm), :] = cu_ref[...]

        @pl.when(i == 0)
        def _():
            dw_ref[...] = jnp.zeros(dw_ref.shape, F32)
            dbias_ref[...] = jnp.zeros(dbias_ref.shape, F32)

        dcu = jnp.zeros((tm, C), F32)
        for j in range(CONV_W):
            dcu = dcu + w_ref[pl.ds(j, 1), :] * extd[pl.ds(CONV_W - 1 - j, tm), :]
            dw_ref[pl.ds(SUB * j, SUB), :] += _sum8(
                du_cur * extc[pl.ds(HALO - CONV_W + 1 + j, tm), :])
        dbias_ref[...] += _sum8(du_cur)
        sb = _sig(b_ref[...])
        da_ref[...] = (dcu * sb).astype(da_ref.dtype)
        db_ref[...] = (dcu * a_ref[...] * sb * (1.0 - sb)).astype(db_ref.dtype)

    cur = pl.BlockSpec((tm, C), lambda i: (i, 0))
    return pl.pallas_call(
        body, name="conv_bwd", grid=(T // tm,),
        in_specs=[cur, pl.BlockSpec((HALO, C), lambda i: (jnp.minimum((i + 1) * hb, last_blk), 0)),
                  cur, pl.BlockSpec((HALO, C), lambda i: (jnp.maximum(i * hb - 1, 0), 0)),
                  cur, cur, pl.BlockSpec((HALO, C), lambda i: (0, 0))],
        out_specs=[cur, cur, pl.BlockSpec((HALO * SUB, C), lambda i: (0, 0)),
                   pl.BlockSpec((SUB, C), lambda i: (0, 0))],
        out_shape=[jax.ShapeDtypeStruct((T, C), BF16), jax.ShapeDtypeStruct((T, C), BF16),
                   jax.ShapeDtypeStruct((HALO * SUB, C), F32), jax.ShapeDtypeStruct((SUB, C), F32)],
        scratch_shapes=[pltpu.VMEM((tm + HALO, C), F32), pltpu.VMEM((HALO + tm, C), F32)],
        compiler_params=_params(("arbitrary",)))(du, du, cu, cu, glu_a, glu_b, w)


def silu_small(c_all):
    def body(c_ref, o_ref):
        v = c_ref[...]
        o_ref[...] = v * _sig(v)
    return pl.pallas_call(body, name="silu_c", out_shape=jax.ShapeDtypeStruct(c_all.shape, F32))(c_all)


def ada_fwd(c_act, w_ada, b_ada):
    n_l, D, ns = w_ada.shape
    B = c_act.shape[0]
    tn = _tile(ns, 512, LANES)

    def body(c_ref, w_ref, b_ref, o_ref):
        o_ref[...] = jnp.dot(c_ref[...], w_ref[...], preferred_element_type=F32,
                             precision=lax.Precision.HIGHEST) + b_ref[...]

    return pl.pallas_call(
        body, name="ada_fwd", grid=(n_l, ns // tn),
        in_specs=[pl.BlockSpec((B, D), lambda l, j: (0, 0)),
                  pl.BlockSpec((None, D, tn), lambda l, j: (l, 0, j)),
                  pl.BlockSpec((None, 1, tn), lambda l, j: (l, 0, j))],
        out_specs=pl.BlockSpec((None, B, tn), lambda l, j: (l, 0, j)),
        out_shape=jax.ShapeDtypeStruct((n_l, B, ns), F32),
        compiler_params=_params(("parallel", "parallel")))(c_act, w_ada, b_ada.reshape(n_l, 1, ns))


def ada_bwd_w(c_act_t, dmod):
    D, B = c_act_t.shape
    n_l, _, ns = dmod.shape
    tn = _tile(ns, 512, LANES)

    def body(c_ref, d_ref, o_ref):
        o_ref[...] = jnp.dot(c_ref[...], d_ref[...], preferred_element_type=F32,
                             precision=lax.Precision.HIGHEST)

    return pl.pallas_call(
        body, name="ada_bwd_w", grid=(n_l, ns // tn),
        in_specs=[pl.BlockSpec((D, B), lambda l, j: (0, 0)),
                  pl.BlockSpec((None, B, tn), lambda l, j: (l, 0, j))],
        out_specs=pl.BlockSpec((None, D, tn), lambda l, j: (l, 0, j)),
        out_shape=jax.ShapeDtypeStruct((n_l, D, ns), F32),
        compiler_params=_params(("parallel", "parallel")))(c_act_t, dmod)


def group_sum(name, v, group):
    rows, W = v.shape
    n = rows // group

    def body(v_ref, o_ref):
        o_ref[...] = jnp.sum(v_ref[...].reshape(n, group, W), axis=1)

    return pl.pallas_call(body, name=name, out_shape=jax.ShapeDtypeStruct((n, W), F32),
                          compiler_params=_params(None))(v)


def lane_total(name, v):
    def body(v_ref, o_ref):
        o_ref[...] = jnp.broadcast_to(jnp.sum(v_ref[...], axis=1, keepdims=True), o_ref.shape)
    return pl.pallas_call(body, name=name, out_shape=jax.ShapeDtypeStruct((1, LANES), F32))(v)


def adamw(name, w, g, m, v):
    rows, cols = w.shape
    tr = _tile(rows, max(SUB, (1 << 19) // cols // SUB * SUB), SUB)

    def body(w_ref, g_ref, m_ref, v_ref, d_ref, nm_ref, nv_ref):
        gg = g_ref[...]
        nm = ADAM_B1 * m_ref[...] + (1.0 - ADAM_B1) * gg
        nv = ADAM_B2 * v_ref[...] + (1.0 - ADAM_B2) * (gg * gg)
        m_hat = nm / (1.0 - ADAM_B1 ** ADAM_STEP)
        v_hat = nv / (1.0 - ADAM_B2 ** ADAM_STEP)
        d_ref[...] = -ADAM_LR * (m_hat / (jnp.sqrt(v_hat) + ADAM_EPS) + ADAM_WD * w_ref[...])
        nm_ref[...] = nm
        nv_ref[...] = nv

    spec = pl.BlockSpec((tr, cols), lambda i: (i, 0))
    return pl.pallas_call(
        body, name=name, grid=(rows // tr,), in_specs=[spec] * 4, out_specs=[spec] * 3,
        out_shape=[jax.ShapeDtypeStruct((rows, cols), F32)] * 3,
        compiler_params=_params(("parallel",)))(w, g, m, v)


def rope_tables(pos_col, invf, one_nope, rope_mask, tm):
    T = pos_col.shape[0]

    def body(p_ref, f_ref, o_ref, r_ref, cq_ref, sq_ref, ck_ref):
        ang = p_ref[...] * f_ref[...]
        cs = jnp.cos(ang) * r_ref[...]
        cq_ref[...] = o_ref[...] + cs
        sq_ref[...] = jnp.sin(ang) * r_ref[...]
        ck_ref[...] = cs

    vec = pl.BlockSpec((1, LANES), lambda i: (0, 0))
    out = pl.BlockSpec((tm, LANES), lambda i: (i, 0))
    return pl.pallas_call(
        body, name="rope_tables", grid=(T // tm,),
        in_specs=[pl.BlockSpec((tm, 1), lambda i: (i, 0)), vec, vec, vec], out_specs=[out] * 3,
        out_shape=[jax.ShapeDtypeStruct((T, LANES), F32)] * 3,
        compiler_params=_params(("parallel",)))(pos_col, invf, one_nope, rope_mask)


def _place():
    return lax.axis_index("x"), lax.axis_index("y"), lax.axis_index("c")


def allgather8(name, v):
    R, W = v.shape

    def body(x_ref, out_ref, send_sems, recv_sems, local_sem):
        x, y, c = _place()
        me = 4 * x + 2 * y + c
        mine = pltpu.make_async_copy(x_ref, out_ref.at[me], local_sem)
        mine.start()
        sends, peers = [], []
        for k in range(1, N_DEV):
            px, py, pc = x ^ ((k >> 2) & 1), y ^ ((k >> 1) & 1), c ^ (k & 1)
            peers.append((px, py, pc))
            cp = pltpu.make_async_remote_copy(
                src_ref=x_ref, dst_ref=out_ref.at[me], send_sem=send_sems.at[k - 1],
                recv_sem=recv_sems.at[k - 1], device_id=(px, py, pc), device_id_type=MESH)
            cp.start()
            sends.append(cp)
        for k, (px, py, pc) in enumerate(peers):
            pltpu.make_async_remote_copy(
                src_ref=x_ref, dst_ref=out_ref.at[4 * px + 2 * py + pc], send_sem=send_sems.at[k],
                recv_sem=recv_sems.at[k], device_id=(px, py, pc), device_id_type=MESH).wait_recv()
        for cp in sends:
            cp.wait_send()
        mine.wait()

    return pl.pallas_call(
        body, name=name, out_shape=jax.ShapeDtypeStruct((N_DEV, R, W), v.dtype),
        in_specs=[pl.BlockSpec(memory_space=pltpu.VMEM)],
        out_specs=pl.BlockSpec(memory_space=pltpu.VMEM),
        scratch_shapes=[pltpu.SemaphoreType.DMA((N_DEV - 1,)), pltpu.SemaphoreType.DMA((N_DEV - 1,)),
                        pltpu.SemaphoreType.DMA])(v)


def _other_chips(x, y):
    return [(1 - x, y), (x, 1 - y), (1 - x, 1 - y)]


def _hbm_exchange(name, body, ins, out_shapes, n_sems):
    n_in, n_out = len(ins), len(out_shapes)

    def wrapped(*refs):
        send_sems, recv_sems = refs[n_in + n_out:]

        def copy(k, src, dst, to):
            return pltpu.make_async_remote_copy(src_ref=src, dst_ref=dst, send_sem=send_sems.at[k],
                                                recv_sem=recv_sems.at[k], device_id=to,
                                                device_id_type=MESH)
        body(refs[:n_in], refs[n_in:n_in + n_out], copy)

    hbm = pl.BlockSpec(memory_space=pl.ANY)
    return pl.pallas_call(
        wrapped, name=name, out_shape=out_shapes, in_specs=[hbm] * n_in, out_specs=[hbm] * n_out,
        scratch_shapes=[pltpu.SemaphoreType.DMA((n_sems,)), pltpu.SemaphoreType.DMA((n_sems,))])(*ins)


def _gather_start(w_refs, o_refs, copy):
    x, y, c = _place()
    for t, (w, o) in enumerate(zip(w_refs, o_refs)):
        for j, (cx, cy) in enumerate(_other_chips(x, y)):
            copy(6 * t + j, w.at[c], o.at[2 * x + y, c], (cx, cy, c)).start()


def _gather_finish(w_refs, o_refs, copy):
    x, y, c = _place()
    chips = _other_chips(x, y)
    for t, (w, o) in enumerate(zip(w_refs, o_refs)):
        for j, (cx, cy) in enumerate(chips):
            landed = o.at[2 * cx + cy, c]
            copy(6 * t + j, w.at[c], landed, (cx, cy, c)).wait_recv()
            copy(6 * t + 3 + j, landed, landed, (x, y, 1 - c)).start()
    for t, (w, o) in enumerate(zip(w_refs, o_refs)):
        for j, (cx, cy) in enumerate(chips):
            copy(6 * t + 3 + j, w.at[c], o.at[2 * cx + cy, 1 - c], (x, y, 1 - c)).wait_recv()
    for t, (w, o) in enumerate(zip(w_refs, o_refs)):
        for j, (cx, cy) in enumerate(chips):
            copy(6 * t + j, w.at[c], o.at[2 * x + y, c], (cx, cy, c)).wait_send()
            landed = o.at[2 * cx + cy, c]
            copy(6 * t + 3 + j, landed, landed, (x, y, 1 - c)).wait_send()


def weight_gather_job(ws):
    return (ws, [jax.ShapeDtypeStruct((N_CHIPS,) + w.shape, w.dtype) for w in ws], 6 * len(ws),
            _gather_start, _gather_finish)


def _exchange_start(p_refs, b_refs, copy):
    x, y, c = _place()
    for t, (p, b) in enumerate(zip(p_refs, b_refs)):
        for j, (cx, cy) in enumerate(_other_chips(x, y)):
            copy(3 * t + j, p.at[2 * cx + cy], b.at[j], (cx, cy, c)).start()


def _exchange_finish(p_refs, b_refs, copy):
    x, y, c = _place()
    for t, (p, b) in enumerate(zip(p_refs, b_refs)):
        for j, (cx, cy) in enumerate(_other_chips(x, y)):
            copy(3 * t + j, p.at[2 * cx + cy], b.at[j], (cx, cy, c)).wait()


def chip_exchange_job(ps):
    return (ps, [jax.ShapeDtypeStruct((3,) + p.shape[1:], p.dtype) for p in ps], 3 * len(ps),
            _exchange_start, _exchange_finish)


def run_job(name, job):
    ins, outs, n_sems, start, finish = job

    def body(in_refs, out_refs, copy):
        start(in_refs, out_refs, copy)
        finish(in_refs, out_refs, copy)
    return _hbm_exchange(name, body, ins, outs, n_sems)


def sibling_swap(gs):
    n = len(gs)

    def body(g_refs, a_refs, copy):
        x, y, c = _place()
        cps = []
        for t in range(n):
            cp = copy(t, g_refs[t].at[pl.ds(0, N_CHIPS), 1 - c], a_refs[t], (x, y, 1 - c))
            cp.start()
            cps.append(cp)
        for cp in cps:
            cp.wait()

    return _hbm_exchange("grad_sibling_swap", body, gs,
                         [jax.ShapeDtypeStruct(g.shape[:1] + g.shape[2:], g.dtype) for g in gs], n)


def chip_partial(name, g, a, c_idx):
    n_s, _, kh, ns = g.shape
    tr = _tile(kh, max(16, (1 << 19) // ns // 16 * 16), 16)

    def body(c_ref, g_ref, a_ref, o_ref):
        o_ref[...] = (g_ref[...] + a_ref[...]).astype(o_ref.dtype)

    return pl.pallas_call(
        body, name=name,
        grid_spec=pltpu.PrefetchScalarGridSpec(
            num_scalar_prefetch=1, grid=(n_s, kh // tr),
            in_specs=[pl.BlockSpec((None, None, tr, ns), lambda i, r, cr: (i, cr[0], r, 0)),
                      pl.BlockSpec((None, tr, ns), lambda i, r, cr: (i, r, 0))],
            out_specs=pl.BlockSpec((None, tr, ns), lambda i, r, cr: (i, r, 0))),
        out_shape=jax.ShapeDtypeStruct((n_s, kh, ns), BF16),
        compiler_params=_params(("parallel", "parallel")))(c_idx, g, a)


def shard_total(name, p, b, s_idx):
    _, rows, ns = p.shape
    tr = _tile(rows, max(16, (1 << 19) // ns // 16 * 16), 16)

    def body(s_ref, p_ref, b0, b1, b2, o_ref):
        o_ref[...] = ((p_ref[...].astype(F32) + b0[...].astype(F32)) + b1[...].astype(F32)
                      ) + b2[...].astype(F32)

    def bspec(j):
        return pl.BlockSpec((None, tr, ns), lambda r, sr: (j, r, 0))

    return pl.pallas_call(
        body, name=name,
        grid_spec=pltpu.PrefetchScalarGridSpec(
            num_scalar_prefetch=1, grid=(rows // tr,),
            in_specs=[pl.BlockSpec((None, tr, ns), lambda r, sr: (sr[0], r, 0)),
                      bspec(0), bspec(1), bspec(2)],
            out_specs=pl.BlockSpec((tr, ns), lambda r, sr: (r, 0))),
        out_shape=jax.ShapeDtypeStruct((rows, ns), F32),
        compiler_params=_params(("parallel",)))(s_idx, p, b, b, b)


def sibling_share(fs):
    n = len(fs)

    def body(f_refs, o_refs, copy):
        x, y, c = _place()
        cps = []
        for t in range(n):
            cp = copy(t, f_refs[t], o_refs[t], (x, y, 1 - c))
            cp.start()
            cps.append(cp)
        for cp in cps:
            cp.wait()

    return _hbm_exchange("grad_sibling_share", body, fs,
                         [jax.ShapeDtypeStruct(f.shape, f.dtype) for f in fs], n)


def _rot_cols(w):
    h = w.shape[-1] // 2
    return jnp.concatenate([-w[..., h:], w[..., :h]], axis=-1)


def _slots(parts, lead, n_h):
    width = sum(p.shape[-1] for p in parts)
    pad = jnp.zeros(lead + (n_h, SLOT - width), parts[0].dtype)
    return jnp.concatenate(parts + [pad], axis=-1).reshape(lead + (n_h * SLOT,))


def layout_weights(w_in, w_uq, w_ukv, w_o_attn, dims):
    QL, KVL, C, D, n_h = dims
    o = 0
    w_ql, o = w_in[..., o:o + QL], o + QL
    w_kvl, o = w_in[..., o:o + KVL], o + KVL
    w_kr, o = w_in[..., o:o + ROPE], o + ROPE
    w_glu_a, o = w_in[..., o:o + C], o + C
    w_glu_b, o = w_in[..., o:o + C], o + C
    w_ga, o = w_in[..., o:o + D], o + D
    w_gb = w_in[..., o:o + D]
    z = lambda n: jnp.zeros(w_kr.shape[:-1] + (n,), w_kr.dtype)
    kr_a = jnp.concatenate([z(NOPE), w_kr, z(SLOT - QK_DIM)], axis=-1)
    kr_b = jnp.concatenate([z(NOPE), _rot_cols(w_kr), z(SLOT - QK_DIM)], axis=-1)
    lead = w_uq.shape[:-1]
    q = w_uq.reshape(lead + (n_h, QK_DIM))
    zq = jnp.zeros(lead + (n_h, NOPE), w_uq.dtype)
    wq_a = _slots([q[..., :NOPE], q[..., NOPE:]], lead, n_h)
    wq_b = _slots([zq, _rot_cols(q[..., NOPE:])], lead, n_h)
    lead = w_ukv.shape[:-1]
    kv = w_ukv.reshape(lead + (n_h, NOPE + VDIM))
    w_kn = _slots([kv[..., :NOPE]], lead, n_h)
    w_v = _slots([kv[..., NOPE:]], lead, n_h)
    lead = w_o_attn.shape[:-2]
    wo = w_o_attn.reshape(lead + (n_h, VDIM, D))
    wo = jnp.concatenate([wo, jnp.zeros(lead + (n_h, SLOT - VDIM, D), wo.dtype)], axis=-2)
    w_o = wo.reshape(lead + (n_h * SLOT, D))
    return dict(ql=w_ql, kvl=w_kvl, kr_a=kr_a, kr_b=kr_b, glu_a=w_glu_a, glu_b=w_glu_b,
                ga=w_ga, gb=w_gb, q_a=wq_a, q_b=wq_b, kn=w_kn, v=w_v, o=w_o)


GATHERED = ("w_in", "w_uq", "w_ukv", "w_o_attn", "w_pw2", "w_out", "w_gu", "w_down")
ROW_SHARDED = ("w_out", "w_down")


def halves(a):
    return a.reshape(a.shape[:-2] + (2, a.shape[-2] // 2, a.shape[-1]))


def full_layer(name, gathered):
    n_c, _, kh, ns = gathered.shape
    if name in ROW_SHARDED:
        return gathered.reshape(n_c * 2 * kh, ns)
    return jnp.concatenate([gathered[s].reshape(2 * kh, ns) for s in range(n_c)], axis=1)


def owed_pieces(name, g):
    if name in ROW_SHARDED:
        return halves(g.reshape(N_CHIPS, g.shape[0] // N_CHIPS, g.shape[1]))
    k, n = g.shape
    return halves(jnp.transpose(g.reshape(k, N_CHIPS, n // N_CHIPS), (1, 0, 2)))


def _pad_rows8(flat):
    n = flat.shape[0]
    w = -(-n // (SUB * LANES)) * LANES
    return jnp.concatenate([flat, jnp.zeros((SUB * w - n,), flat.dtype)]).reshape(SUB, w)


def kernel(x, c, positions, w_ada, b_ada, g_mix, w_in, g_q, w_uq, g_kv, w_ukv, w_o_attn, w_dw, b_dw, g_cn, b_cn, w_pw2, w_out, g_ffn, w_gu, w_down, g_final, loss_target, m_w_ada, m_b_ada, m_g_mix, m_w_in, m_g_q, m_w_uq, m_g_kv, m_w_ukv, m_w_o_attn, m_w_dw, m_b_dw, m_g_cn, m_b_cn, m_w_pw2, m_w_out, m_g_ffn, m_w_gu, m_w_down, m_g_final, v_w_ada, v_b_ada, v_g_mix, v_w_in, v_g_q, v_w_uq, v_g_kv, v_w_ukv, v_w_o_attn, v_w_dw, v_b_dw, v_g_cn, v_b_cn, v_w_pw2, v_w_out, v_g_ffn, v_w_gu, v_w_down, v_g_final):
    weights = dict(w_ada=w_ada, b_ada=b_ada, g_mix=g_mix, w_in=w_in, g_q=g_q, w_uq=w_uq, g_kv=g_kv,
                   w_ukv=w_ukv, w_o_attn=w_o_attn, w_dw=w_dw, b_dw=b_dw, g_cn=g_cn, b_cn=b_cn,
                   w_pw2=w_pw2, w_out=w_out, g_ffn=g_ffn, w_gu=w_gu, w_down=w_down, g_final=g_final)
    mom = dict(w_ada=m_w_ada, b_ada=m_b_ada, g_mix=m_g_mix, w_in=m_w_in, g_q=m_g_q, w_uq=m_w_uq,
               g_kv=m_g_kv, w_ukv=m_w_ukv, w_o_attn=m_w_o_attn, w_dw=m_w_dw, b_dw=m_b_dw,
               g_cn=m_g_cn, b_cn=m_b_cn, w_pw2=m_w_pw2, w_out=m_w_out, g_ffn=m_g_ffn, w_gu=m_w_gu,
               w_down=m_w_down, g_final=m_g_final)
    var = dict(w_ada=v_w_ada, b_ada=v_b_ada, g_mix=v_g_mix, w_in=v_w_in, g_q=v_g_q, w_uq=v_w_uq,
               g_kv=v_g_kv, w_ukv=v_w_ukv, w_o_attn=v_w_o_attn, w_dw=v_w_dw, b_dw=v_b_dw,
               g_cn=v_g_cn, b_cn=v_b_cn, w_pw2=v_w_pw2, w_out=v_w_out, g_ffn=v_g_ffn, w_gu=v_w_gu,
               w_down=v_w_down, g_final=v_g_final)
    order = list(weights)

    n_e, S, D = x.shape
    T = n_e * S
    n_l = w_in.shape[0]
    QL, KVL, C = g_q.shape[1], g_kv.shape[1], g_cn.shape[1]
    n_h = w_uq.shape[2] * N_CHIPS // QK_DIM
    F = w_gu.shape[2] * N_CHIPS // 2
    B = n_e * N_DEV
    dims = (QL, KVL, C, D, n_h)
    scale = QK_DIM ** -0.5
    tm = _tile(S, 512, HALO)
    tm_mm = _tile(S, 1024, HALO)
    blk = _tile(S, ATTN_LB, LANES)
    tq = _tile(S, ATTN_TQ, blk)
    q_scale = scale / LN2
    mx, my, mc = _place()
    dev = 4 * mx + 2 * my + mc
    chip = 2 * mx + my
    c_idx = jnp.reshape(mc, (1,)).astype(jnp.int32)
    s_idx = jnp.reshape(chip, (1,)).astype(jnp.int32)

    xt = x.reshape(T, D)
    tgt = loss_target.reshape(T, D)

    mine = [[halves(weights[k][l].astype(BF16)) for k in GATHERED] for l in range(n_l)]
    zero = jnp.zeros((), jnp.int32)
    full = {k: [None] * n_l for k in GATHERED}
    lay = [None] * n_l

    def take_weights(l, got):
        for k, g, w in zip(GATHERED, got, mine[l]):
            g = lax.dynamic_update_slice(g, w[None], (chip.astype(jnp.int32), zero, zero, zero))
            full[k][l] = full_layer(k, g)
        lay[l] = layout_weights(full["w_in"][l], full["w_uq"][l], full["w_ukv"][l],
                                full["w_o_attn"][l], dims)

    take_weights(0, run_job("weight_allgather", weight_gather_job(mine[0])))

    ns_ada = w_ada.shape[2]
    n_c, n_wdw = n_e * D, int(np.prod(w_dw.shape))
    small_all = allgather8("gather_c_wdw", _pad_rows8(
        jnp.concatenate([c.reshape(-1), w_dw.reshape(-1)]))).reshape(N_DEV, -1)
    c_all = small_all[:, :n_c].reshape(B, D)
    wdw_parts = small_all[:, n_c:n_c + n_wdw].reshape((N_DEV,) + w_dw.shape)
    w_dw_full = jnp.concatenate([wdw_parts[2 * s] for s in range(N_CHIPS)], axis=2)
    w_dw_pad = jnp.concatenate([w_dw_full, jnp.zeros((n_l, HALO - CONV_W, C), F32)], axis=1)

    c_act = silu_small(c_all)
    b_ada_mine = lax.dynamic_slice_in_dim(b_ada, chip * ns_ada, ns_ada, axis=1)
    mod_part = ada_fwd(c_act, w_ada, b_ada_mine)
    mod_all = allgather8("gather_mod", mod_part.reshape(n_l * B, ns_ada)).reshape(
        N_DEV, n_l, B, ns_ada)
    mod_full = jnp.concatenate([mod_all[2 * s] for s in range(N_CHIPS)], axis=2)
    mod_mine = lax.dynamic_slice_in_dim(mod_full, dev * n_e, n_e, axis=1)
    mods = mod_mine.reshape(n_l, n_e, N_MOD, 1, D)


    lane = np.arange(LANES)
    in_rope = (lane >= NOPE) & (lane < QK_DIM)
    inv_freq = ROPE_THETA ** (-np.arange(0, ROPE, 2, dtype=np.float32) / ROPE)
    invf = np.where(in_rope, inv_freq[(lane - NOPE) % (ROPE // 2)], 0.0).astype(np.float32)
    cos_q, sin_q, cos_k = rope_tables(
        positions.astype(F32).reshape(T, 1), jnp.asarray(invf).reshape(1, LANES),
        jnp.asarray((lane < NOPE).astype(np.float32)).reshape(1, LANES),
        jnp.asarray(in_rope.astype(np.float32)).reshape(1, LANES), tm)

    def rope_epi(accs, ex):
        n = accs[0].shape[1]
        return (accs[0] * _lanes(ex[0], n) + accs[1] * _lanes(ex[1], n),)

    def rms_epi(accs, ex):
        a = accs[0]
        return a, a * _rstd(a) * ex[0]

    def modnorm(name, xin, g, sc, sh):
        def fn(xv, gv, scv, shv):
            return (xv * _rstd(xv) * gv * (1.0 + scv) + shv,)
        return rowwise(name, [('tile', xin), ('vec', g), ('exvec', sc), ('exvec', sh)],
                       [('tile', D, BF16)], fn, tm, S)[0]

    def modnorm_bwd(name, dh, xin, dres, g, sc):
        def fn(dhv, xv, drv, gv, scv):
            rstd = _rstd(xv)
            xhat = xv * rstd
            dx = _norm_bwd(dhv * gv * (1.0 + scv), xhat, rstd) + drv
            return dx, dhv, dhv * xhat * gv, dhv * xhat * (1.0 + scv)
        return rowwise(name, [('tile', dh), ('tile', xin), ('tile', dres), ('vec', g), ('exvec', sc)],
                       [('tile', D, F32), ('exacc', D), ('exacc', D), ('acc', D)], fn, tm, S)

    def rms_bwd(name, dy, xin, g, width):
        def fn(dyv, xv, gv):
            rstd = _rstd(xv)
            xhat = xv * rstd
            return _norm_bwd(dyv * gv, xhat, rstd), dyv * xhat
        return rowwise(name, [('tile', dy), ('tile', xin), ('vec', g)],
                       [('tile', width, BF16), ('acc', width)], fn, tm, S)

    def gate_bwd(name, dxo, branch, gt):
        def fn(dv, bv, gv):
            return dv * gv, dv * bv
        return rowwise(name, [('tile', dxo), ('tile', branch), ('exvec', gt)],
                       [('tile', D, BF16), ('exacc', D)], fn, tm, S)

    tn_d = _tile(D, 512, LANES)
    tn_f = _tile(F, 1536, LANES)
    tn_s = _tile(n_h * SLOT, 512, LANES)
    row = lambda a: a.reshape(1, -1)

    saved = []
    xc = xt
    for l in range(n_l):
        W = lay[l]
        sh1, sc1, gt1, sh2, sc2, gt2 = [mods[l, :, k] for k in range(N_MOD)]
        h1 = modnorm("modnorm_mix", xc, row(g_mix[l]), sc1, sh1)
        ga, gb = mm_fused("proj_gates", [h1], [(0, W["ga"]), (0, W["gb"])], [], lambda a, e: a,
                          [F32, F32], tm_mm, tn_d, S)
        cu, glu_a, glu_b = mm_fused(
            "proj_glu", [h1], [(0, W["glu_a"]), (0, W["glu_b"])], [],
            lambda a, e: (a[0] * _sig(a[1]), a[0], a[1]), [F32, F32, F32], tm, C, S)
        q_lat, qn = mm_fused("proj_q_lat", [h1], [(0, W["ql"])], [('vec', row(g_q[l]))], rms_epi,
                             [F32, BF16], tm, QL, S)
        kv_lat, kvn = mm_fused("proj_kv_lat", [h1], [(0, W["kvl"])], [('vec', row(g_kv[l]))],
                               rms_epi, [F32, BF16], tm, KVL, S)
        kr = mm_fused("proj_k_rope", [h1], [(0, W["kr_a"]), (0, W["kr_b"])],
                      [('row128', cos_k), ('row128', sin_q)], rope_epi, [F32], tm, SLOT, S)[0]
        q_all = mm_fused("q_up", [qn], [(0, W["q_a"]), (0, W["q_b"])],
                         [('row128', cos_q), ('row128', sin_q)],
                         lambda a, e: (rope_epi(a, e)[0] * q_scale,), [BF16], tm, tn_s, S)[0]
        k_all, v_all = mm_fused(
            "kv_up", [kvn], [(0, W["kn"]), (0, W["v"])], [('row128', kr)],
            lambda a, e: (a[0] + _lanes(e[0], a[0].shape[1]), a[1]), [BF16, BF16], tm, tn_s, S)
        o_all, lse, *got = attn_fwd(q_all, k_all, v_all, n_e, S, n_h, tq, blk,
                                    job=weight_gather_job(mine[l + 1]) if l + 1 < n_l else None)
        if l + 1 < n_l:
            take_weights(l + 1, got)
        u, s_act = conv_fwd(cu, w_dw_pad[l], row(b_dw[l]), row(g_cn[l]), row(b_cn[l]), tm, S)

        def merge_epi(a, e):
            return _sig(e[0]) * a[0] + _sig(e[1]) * a[1], a[0], a[1]
        y, ya, yc = mm_fused("merge", [o_all, s_act], [(0, W["o"]), (1, full["w_pw2"][l])],
                             [('tile', ga), ('tile', gb)], merge_epi, [BF16, F32, F32], tm_mm, tn_d, S)
        x2, o_mix = mm_fused("mix_out", [y], [(0, full["w_out"][l])], [('tile', xc), ('exvec', gt1)],
                             lambda a, e: (e[0] + e[1] * a[0], a[0]), [F32, F32], tm_mm, tn_d, S)
        h2 = modnorm("modnorm_ffn", x2, row(g_ffn[l]), sc2, sh2)

        def swiglu_epi(a, e):
            return a[0], a[1], a[0] * _sig(a[0]) * a[1]
        w_gu_l = full["w_gu"][l]
        g_act, up, act = mm_fused("ffn_up", [h2], [(0, w_gu_l), (0, w_gu_l, False, F // tn_f)], [],
                                  swiglu_epi, [BF16, BF16, BF16], tm, tn_f, S, N=F)
        x3, dn = mm_fused("ffn_down", [act], [(0, full["w_down"][l])], [('tile', x2), ('exvec', gt2)],
                          lambda a, e: (e[0] + e[1] * a[0], a[0]), [F32, F32], tm_mm, tn_d, S)
        saved.append(dict(x=xc, h1=h1, ga=ga, gb=gb, cu=cu, glu_a=glu_a, glu_b=glu_b, q_lat=q_lat,
                          qn=qn, kv_lat=kv_lat, kvn=kvn, q_all=q_all, k_all=k_all, v_all=v_all,
                          o_all=o_all, lse=lse, u=u, s_act=s_act, y=y, ya=ya, yc=yc, x2=x2,
                          o_mix=o_mix, h2=h2, g_act=g_act, up=up, act=act, dn=dn))
        xc = x3

    def loss_fn(xv, tv, gv):
        rstd = _rstd(xv)
        xhat = xv * rstd
        err = xhat * gv - tv
        dy = err * (1.0 / D)
        return _norm_bwd(dy * gv, xhat, rstd), err * err * (0.5 / D), dy * xhat
    dxc, loss_acc, dg_final = rowwise("loss_head", [('tile', xc), ('tile', tgt), ('vec', row(g_final))],
                                      [('tile', D, F32), ('acc', D), ('acc', D)], loss_fn, tm, S)

    gfull = {k: [None] * n_l for k in GATHERED}
    g_wdw = [None] * n_l
    small_acc = {k: [None] * n_l for k in ("g_mix", "g_q", "g_kv", "b_dw", "g_cn", "b_cn", "g_ffn")}
    dmod_acc = [None] * n_l
    parts, received = [None] * n_l, [None] * n_l
    lay_T = jax.linear_transpose(
        lambda a, b, cc, d: layout_weights(a, b, cc, d, dims),
        *[jax.ShapeDtypeStruct(full[k][0].shape, F32) for k in ("w_in", "w_uq", "w_ukv", "w_o_attn")])

    for l in reversed(range(n_l)):
        sv = saved[l]
        W = lay[l]
        sh1, sc1, gt1, sh2, sc2, gt2 = [mods[l, :, k] for k in range(N_MOD)]
        w_gu_l = full["w_gu"][l]
        ddn, dgt2 = gate_bwd("ffn_gate_bwd", dxc, sv["dn"], gt2)

        def swiglu_bwd_epi(a, e):
            gv, uv = e[0].astype(F32), e[1].astype(F32)
            sg = _sig(gv)
            return a[0] * uv * sg * (1.0 + gv * (1.0 - sg)), a[0] * gv * sg
        dg_act, dup = mm_fused("ffn_down_bwd", [ddn], [(0, full["w_down"][l], True)],
                               [('tile', sv["g_act"]), ('tile', sv["up"])], swiglu_bwd_epi,
                               [BF16, BF16], tm, tn_f, S)
        gfull["w_down"][l] = mm_tn("ffn_down_dw", sv["act"], ddn)
        gfull["w_gu"][l] = jnp.concatenate([mm_tn("ffn_gate_dw", sv["h2"], dg_act),
                                            mm_tn("ffn_up_dw", sv["h2"], dup)], axis=1)
        dh2 = mm_fused("ffn_up_bwd", [dg_act, dup], [(0, w_gu_l, True, 0), (1, w_gu_l, True, 1)], [],
                       lambda a, e: (a[0] + a[1],), [F32], tm_mm, tn_d, S)[0]
        dx2, dsh2, dsc2, dg_ffn = modnorm_bwd("modnorm_ffn_bwd", dh2, sv["x2"], dxc,
                                              row(g_ffn[l]), sc2)
        ddo, dgt1 = gate_bwd("mix_gate_bwd", dx2, sv["o_mix"], gt1)

        def merge_bwd_epi(a, e):
            sa, sb = _sig(e[0]), _sig(e[1])
            dy = a[0]
            return dy * sa, dy * sb, dy * e[2] * sa * (1.0 - sa), dy * e[3] * sb * (1.0 - sb)
        dya, dyc, dga, dgb = mm_fused(
            "mix_out_bwd", [ddo], [(0, full["w_out"][l], True)],
            [('tile', sv["ga"]), ('tile', sv["gb"]), ('tile', sv["ya"]), ('tile', sv["yc"])],
            merge_bwd_epi, [BF16] * 4, tm_mm, tn_d, S)
        gfull["w_out"][l] = mm_tn("mix_out_dw", sv["y"], ddo)
        do_all = mm_fused("attn_out_bwd", [dya], [(0, W["o"], True)], [], lambda a, e: a, [BF16],
                          tm, tn_s, S)[0]
        d_wo = mm_tn("attn_out_dw", sv["o_all"], dya)
        ds_act = mm_fused("conv_out_bwd", [dyc], [(0, full["w_pw2"][l], True)], [], lambda a, e: a,
                          [F32], tm, C, S)[0]
        gfull["w_pw2"][l] = mm_tn("conv_out_dw", sv["s_act"], dyc)

        def ln_silu_bwd(dsv, uv, gv, bv):
            xhat, rstd = _layer_norm_parts(uv)
            ln = xhat * gv + bv
            sg = _sig(ln)
            dln = dsv * sg * (1.0 + ln * (1.0 - sg))
            dxhat = dln * gv
            du_ = rstd * (dxhat - jnp.mean(dxhat, axis=-1, keepdims=True)
                          - xhat * jnp.mean(dxhat * xhat, axis=-1, keepdims=True))
            return du_, dln * xhat, dln
        du, dg_cn, db_cn = rowwise(
            "conv_norm_bwd", [('tile', ds_act), ('tile', sv["u"]), ('vec', row(g_cn[l])),
                              ('vec', row(b_cn[l]))],
            [('tile', C, F32), ('acc', C), ('acc', C)], ln_silu_bwd, tm, S)
        dglu_a, dglu_b, dw_acc, db_dw = conv_bwd(du, sv["cu"], sv["glu_a"], sv["glu_b"],
                                                 w_dw_pad[l], tm, S)
        g_wdw[l] = group_sum("conv_dw_rows", dw_acc, SUB)[:CONV_W]
        dk_all, dv_all, gq_a, gq_b, *got = attn_bwd(
            sv["q_all"], sv["k_all"], sv["v_all"], sv["o_all"], do_all, sv["lse"], cos_q, sin_q,
            n_e, S, n_h, blk, scale, job=chip_exchange_job(parts[l + 1]) if l + 1 < n_l else None)
        if l + 1 < n_l:
            received[l + 1] = got
        dqn = mm_fused("q_up_bwd", [gq_a, gq_b], [(0, W["q_a"], True), (1, W["q_b"], True)], [],
                       lambda a, e: (a[0] + a[1],), [F32], tm, QL, S)[0]
        d_wqa = mm_tn("q_up_dw_a", sv["qn"], gq_a)
        d_wqb = mm_tn("q_up_dw_b", sv["qn"], gq_b)
        dq_lat, dg_q = rms_bwd("q_norm_bwd", dqn, sv["q_lat"], row(g_q[l]), QL)

        def k_split(dkv, ckv, skv):
            tot = dkv[:, :SLOT]
            for h in range(1, n_h):
                tot = tot + dkv[:, h * SLOT:(h + 1) * SLOT]
            return dkv, tot * ckv, tot * skv
        dk_b, dkr_a, dkr_b = rowwise("k_rope_bwd", [('tile', dk_all), ('tile', cos_k), ('tile', sin_q)],
                                     [('tile', n_h * SLOT, BF16), ('tile', SLOT, BF16),
                                      ('tile', SLOT, BF16)], k_split, tm, S)
        dkvn = mm_fused("kv_up_bwd", [dk_b, dv_all], [(0, W["kn"], True), (1, W["v"], True)], [],
                        lambda a, e: (a[0] + a[1],), [F32], tm, KVL, S)[0]
        d_wkn = mm_tn("kv_up_dw_k", sv["kvn"], dk_b)
        d_wv = mm_tn("kv_up_dw_v", sv["kvn"], dv_all)
        dkv_lat, dg_kv = rms_bwd("kv_norm_bwd", dkvn, sv["kv_lat"], row(g_kv[l]), KVL)
        segs = [("ga", dga), ("gb", dgb), ("glu_a", dglu_a), ("glu_b", dglu_b), ("ql", dq_lat),
                ("kvl", dkv_lat), ("kr_a", dkr_a), ("kr_b", dkr_b)]
        dh1 = mm_fused("proj_bwd", [g for _, g in segs],
                       [(k, W[nm], True) for k, (nm, _) in enumerate(segs)],
                       [], lambda a, e: (functools.reduce(lambda p, q: p + q, a),), [F32],
                       tm_mm, tn_d, S)[0]
        d_lay = {nm: mm_tn("proj_dw_" + nm, sv["h1"], g) for nm, g in segs}
        d_lay.update(q_a=d_wqa, q_b=d_wqb, kn=d_wkn, v=d_wv, o=d_wo)
        (gfull["w_in"][l], gfull["w_uq"][l], gfull["w_ukv"][l],
         gfull["w_o_attn"][l]) = lay_T({k: d_lay[k] for k in lay[l]})
        dxc, dsh1, dsc1, dg_mix = modnorm_bwd("modnorm_mix_bwd", dh1, sv["x"], dx2,
                                              row(g_mix[l]), sc1)
        dmod_acc[l] = [dsh1, dsc1, dgt1, dsh2, dsc2, dgt2]
        for k, a in (("g_mix", dg_mix), ("g_q", dg_q), ("g_kv", dg_kv), ("b_dw", db_dw),
                     ("g_cn", dg_cn), ("b_cn", db_cn), ("g_ffn", dg_ffn)):
            small_acc[k][l] = a
        owed = [owed_pieces(k, gfull[k][l]) for k in GATHERED]
        parts[l] = [chip_partial("grad_chip_partial_" + k, g, a, c_idx)
                    for k, g, a in zip(GATHERED, owed, sibling_swap(owed))]

    received[0] = run_job("grad_chip_exchange", chip_exchange_job(parts[0]))
    grad_x = dxc.reshape(n_e, S, D)

    dmod_rows = jnp.concatenate([a for l in range(n_l) for a in dmod_acc[l]], axis=0)
    dmod_own = group_sum("dmod_rows", dmod_rows, SUB).reshape(n_l, N_MOD, n_e, D)
    dmod_own = jnp.transpose(dmod_own, (0, 2, 1, 3)).reshape(n_l * n_e, N_MOD * D)
    dmod_all = allgather8("gather_dmod", dmod_own).reshape(N_DEV, n_l, n_e, N_MOD * D)
    dmod_all = jnp.transpose(dmod_all, (1, 0, 2, 3)).reshape(n_l, B, N_MOD * D)
    dmod_mine = lax.dynamic_slice_in_dim(dmod_all, chip * ns_ada, ns_ada, axis=2)
    grad_w_ada = ada_bwd_w(jnp.transpose(c_act), dmod_mine)
    grad_b_ada = group_sum("grad_b_ada", dmod_all.reshape(n_l * B, N_MOD * D), B)

    mine_half = [shard_total("grad_shard_total_" + k, p, b, s_idx)
                 for l in range(n_l) for k, p, b in zip(GATHERED, parts[l], received[l])]
    both = []
    for half, other in zip(mine_half, sibling_share(mine_half)):
        kh, ns = half.shape
        both.append(lax.dynamic_update_slice(
            jnp.broadcast_to(other[None], (2, kh, ns)), half[None],
            (mc.astype(jnp.int32), zero, zero)).reshape(2 * kh, ns))
    n_g = len(GATHERED)
    red = {k: jnp.stack([both[l * n_g + t] for l in range(n_l)]) for t, k in enumerate(GATHERED)}

    wdw_full_g = jnp.stack(g_wdw)
    pieces = [loss_acc, dg_final] + [small_acc[k][l] for k in small_acc for l in range(n_l)]
    widths = [p.shape[1] for p in pieces]
    n_acc = sum(widths)
    wdw_blk = _pad_rows8(wdw_full_g.reshape(-1))
    wdw_w = wdw_blk.shape[1]
    gathered_small = allgather8("gather_small_grads", jnp.concatenate(pieces + [wdw_blk], axis=1))
    acc_sum = group_sum("small_total", gathered_small[:, :, :n_acc].reshape(N_DEV * SUB, n_acc),
                        N_DEV * SUB)
    wdw_sum = group_sum("wdw_total", gathered_small[:, :, n_acc:].reshape(N_DEV, SUB * wdw_w), N_DEV)
    offs = np.cumsum([0] + widths)
    take = lambda i: acc_sum[:, offs[i]:offs[i + 1]]
    loss = lane_total("loss_total", take(0))[0, 0]
    g_small = {"g_final": take(1).reshape(-1)}
    i = 2
    for k in small_acc:
        g_small[k] = jnp.concatenate([take(i + l) for l in range(n_l)], axis=0)
        i += n_l
    wdw_total = wdw_sum.reshape(-1)[:n_wdw * N_CHIPS].reshape(wdw_full_g.shape)
    grad_w_dw = lax.dynamic_slice_in_dim(wdw_total, chip * w_dw.shape[2], w_dw.shape[2], axis=2)

    grads = dict(w_ada=grad_w_ada, b_ada=grad_b_ada, w_dw=grad_w_dw, **g_small, **red)

    deltas, new_m, new_v = {}, {}, {}
    for k in order:
        shp = weights[k].shape
        two = (1, shp[0]) if len(shp) == 1 else (int(np.prod(shp[:-1])), shp[-1])
        d, nm, nv = adamw("adamw_" + k, weights[k].reshape(two), grads[k].reshape(two),
                          mom[k].reshape(two), var[k].reshape(two))
        deltas[k], new_m[k], new_v[k] = d.reshape(shp), nm.reshape(shp), nv.reshape(shp)
        grads[k] = grads[k].reshape(shp)

    return (loss, grad_x, *[grads[k] for k in order], *[deltas[k] for k in order],
            *[new_m[k] for k in order], *[new_v[k] for k in order])
```

```python
import functools

import numpy as np
import jax
import jax.numpy as jnp
from jax import lax
from jax.experimental import pallas as pl
from jax.experimental.pallas import tpu as pltpu

F32 = jnp.float32
BF16 = jnp.bfloat16
MESH = pl.DeviceIdType.MESH

EPS = 1e-6
NEG_INF = -1e30
NOPE, ROPE, VDIM = 64, 32, 64
QK_DIM = NOPE + ROPE
SLOT = 128
CONV_W = 31
HALO = 32
N_MOD = 6
ROPE_THETA = 10000.0
N_CHIPS = 4
N_DEV = 8
SUB = 8
LANES = 128
VMEM_LIMIT = 56 * 1024 * 1024

ADAM_LR, ADAM_B1, ADAM_B2, ADAM_EPS, ADAM_WD, ADAM_STEP = 0.001, 0.9, 0.999, 1e-08, 0.01, 10


def _tile(n, cap, mult):
    best = None
    for d in range(mult, min(n, cap) + 1, mult):
        if n % d == 0:
            best = d
    return best if best is not None else n


def _params(sem):
    return pltpu.CompilerParams(dimension_semantics=sem, vmem_limit_bytes=VMEM_LIMIT)


def _sig(x):
    return 1.0 / (1.0 + jnp.exp(-x))


def _sum8(x):
    r, w = x.shape
    return jnp.sum(x.reshape(r // SUB, SUB, w), axis=0)


def _lanes(v, n):
    return v if n == v.shape[1] else jnp.tile(v, (1, n // v.shape[1]))


def _rstd(x):
    return lax.rsqrt(jnp.mean(x * x, axis=-1, keepdims=True) + EPS)


def _norm_bwd(dxhat, xhat, rstd):
    return rstd * (dxhat - xhat * jnp.mean(dxhat * xhat, axis=-1, keepdims=True))


def mm_fused(name, As, pairs, extras, epilogue, out_dtypes, tm, tn, S, N=None):
    T = As[0].shape[0]
    pairs = [(p[0], p[1], p[2] if len(p) > 2 else False, p[3] if len(p) > 3 else 0) for p in pairs]
    if N is None:
        N = pairs[0][1].shape[0] if pairs[0][2] else pairs[0][1].shape[1]
    nex = S // tm
    in_specs, args = [], []
    for a in As:
        in_specs.append(pl.BlockSpec((tm, a.shape[1]), lambda i, j: (i, 0)))
        args.append(a)
    for ai, b, trans, off in pairs:
        kdim = As[ai].shape[1]
        if trans:
            in_specs.append(pl.BlockSpec((tn, kdim), lambda i, j, off=off: (j, off)))
        else:
            in_specs.append(pl.BlockSpec((kdim, tn), lambda i, j, off=off: (0, j + off)))
        args.append(b)
    for kind, arr in extras:
        if kind == 'tile':
            in_specs.append(pl.BlockSpec((tm, tn), lambda i, j: (i, j)))
        elif kind == 'row128':
            in_specs.append(pl.BlockSpec((tm, LANES), lambda i, j: (i, 0)))
        elif kind == 'vec':
            in_specs.append(pl.BlockSpec((1, tn), lambda i, j: (0, j)))
        else:
            in_specs.append(pl.BlockSpec((None, 1, tn), lambda i, j: (i // nex, 0, j)))
        args.append(arr)
    n_a, n_p, n_e = len(As), len(pairs), len(extras)

    def body(*refs):
        a_refs, b_refs = refs[:n_a], refs[n_a:n_a + n_p]
        e_refs, o_refs = refs[n_a + n_p:n_a + n_p + n_e], refs[n_a + n_p + n_e:]
        accs = [lax.dot_general(a_refs[ai][...], b_refs[k][...], NT if trans else NN,
                                preferred_element_type=F32)
                for k, (ai, _, trans, _) in enumerate(pairs)]
        outs = epilogue(accs, [r[...] for r in e_refs])
        for o_ref, o in zip(o_refs, outs):
            o_ref[...] = o.astype(o_ref.dtype)

    return pl.pallas_call(
        body, name=name, grid=(T // tm, N // tn), in_specs=in_specs,
        out_specs=[pl.BlockSpec((tm, tn), lambda i, j: (i, j)) for _ in out_dtypes],
        out_shape=[jax.ShapeDtypeStruct((T, N), dt) for dt in out_dtypes],
        compiler_params=_params(("parallel", "parallel")))(*args)


def mm_tn(name, A, G):
    T, K = A.shape
    N = G.shape[1]
    tt = _tile(T, 1024, 16)
    tk = _tile(K, 1536, LANES)
    tn = _tile(N, 1536, LANES)

    def body(a_ref, g_ref, o_ref):
        part = lax.dot_general(a_ref[...], g_ref[...], (((0,), (0,)), ((), ())),
                               preferred_element_type=F32)

        @pl.when(pl.program_id(2) == 0)
        def _():
            o_ref[...] = part

        @pl.when(pl.program_id(2) > 0)
        def _():
            o_ref[...] += part

    return pl.pallas_call(
        body, name=name, grid=(K // tk, N // tn, T // tt),
        in_specs=[pl.BlockSpec((tt, tk), lambda k, n, t: (t, k)),
                  pl.BlockSpec((tt, tn), lambda k, n, t: (t, n))],
        out_specs=pl.BlockSpec((tk, tn), lambda k, n, t: (k, n)),
        out_shape=jax.ShapeDtypeStruct((K, N), F32),
        compiler_params=_params(("parallel", "parallel", "arbitrary")))(A, G)


def rowwise(name, ins, outs, fn, tm, S):
    T = next(a.shape[0] for k, a in ins if k == 'tile')
    nex = S // tm
    n_ex = T // S
    in_specs, args = [], []
    for kind, arr in ins:
        if kind == 'tile':
            in_specs.append(pl.BlockSpec((tm, arr.shape[1]), lambda i: (i, 0)))
        elif kind == 'vec':
            in_specs.append(pl.BlockSpec((1, arr.shape[1]), lambda i: (0, 0)))
        else:
            in_specs.append(pl.BlockSpec((None, 1, arr.shape[2]), lambda i: (i // nex, 0, 0)))
        args.append(arr)
    out_specs, out_shape = [], []
    for o in outs:
        if o[0] == 'tile':
            out_specs.append(pl.BlockSpec((tm, o[1]), lambda i: (i, 0)))
            out_shape.append(jax.ShapeDtypeStruct((T, o[1]), o[2]))
        elif o[0] == 'acc':
            out_specs.append(pl.BlockSpec((SUB, o[1]), lambda i: (0, 0)))
            out_shape.append(jax.ShapeDtypeStruct((SUB, o[1]), F32))
        else:
            out_specs.append(pl.BlockSpec((SUB, o[1]), lambda i: (i // nex, 0)))
            out_shape.append(jax.ShapeDtypeStruct((n_ex * SUB, o[1]), F32))
    n_in = len(ins)

    def body(*refs):
        i = pl.program_id(0)
        vals = fn(*[r[...] for r in refs[:n_in]])
        for o, o_ref, v in zip(outs, refs[n_in:], vals):
            if o[0] == 'tile':
                o_ref[...] = v.astype(o_ref.dtype)
            else:
                part = _sum8(v)
                first = (i == 0) if o[0] == 'acc' else (i % nex == 0)

                @pl.when(first)
                def _(o_ref=o_ref, part=part):
                    o_ref[...] = part

                @pl.when(jnp.logical_not(first))
                def _(o_ref=o_ref, part=part):
                    o_ref[...] += part

    return pl.pallas_call(
        body, name=name, grid=(T // tm,), in_specs=in_specs, out_specs=out_specs,
        out_shape=out_shape, compiler_params=_params(("arbitrary",)))(*args)


NN = (((1,), (0,)), ((), ()))
NT = (((1,), (1,)), ((), ()))
TN = (((0,), (0,)), ((), ()))
LN2 = 0.6931471805599453
ATTN_TQ = 1024
ATTN_LB = 512


def _hosted(job, n_in, n_out, grid):
    if job is None:
        return [], [], [], [], (lambda refs: None), (lambda refs: None), []
    ins, outs, n_sems, start, finish = job
    hbm = pl.BlockSpec(memory_space=pl.ANY)
    n_j = len(ins)

    def split(refs):
        j_in = refs[n_in:n_in + n_j]
        j_out = refs[n_in + n_j + n_out:n_in + n_j + n_out + n_j]
        send_sems, recv_sems = refs[-2:]

        def copy(k, src, dst, to):
            return pltpu.make_async_remote_copy(src_ref=src, dst_ref=dst, send_sem=send_sems.at[k],
                                                recv_sem=recv_sems.at[k], device_id=to,
                                                device_id_type=MESH)
        return j_in, j_out, copy

    def at(step_of):
        cond = None
        for axis, size in enumerate(grid):
            hit = pl.program_id(axis) == step_of(size)
            cond = hit if cond is None else jnp.logical_and(cond, hit)
        return cond

    def begin(refs):
        @pl.when(at(lambda size: 0))
        def _():
            start(*split(refs))

    def end(refs):
        @pl.when(at(lambda size: size - 1))
        def _():
            finish(*split(refs))

    sems = [pltpu.SemaphoreType.DMA((n_sems,)), pltpu.SemaphoreType.DMA((n_sems,))]
    return [hbm] * n_j, [hbm] * n_j, list(outs), sems, begin, end, list(ins)


def attn_fwd(Q, K, V, n_b, S, n_h, tq, lb, job=None):
    T = n_b * S
    nq = S // tq
    ratio = tq // lb
    tk = lb
    grid = (n_b, n_h, nq)
    j_in, j_out, j_shapes, j_scratch, begin, end, j_args = _hosted(job, 3, 2, grid)
    n_j = len(j_args)

    def body(*refs):
        q_ref, k_ref, v_ref = refs[:3]
        o_ref, lse_ref = refs[3 + n_j:5 + n_j]
        m_s, l_s, acc_s = refs[5 + 2 * n_j:8 + 2 * n_j]
        begin(refs)
        i = pl.program_id(2)
        m_s[...] = jnp.full(m_s.shape, NEG_INF, F32)
        l_s[...] = jnp.zeros(l_s.shape, F32)
        acc_s[...] = jnp.zeros(acc_s.shape, F32)

        def kv_step(j, diag_off, rsplit):
            start = pl.multiple_of(j * tk, tk)
            k = k_ref[pl.ds(start, tk), :]
            v = v_ref[pl.ds(start, tk), :]
            rc = tq // rsplit
            for r in range(rsplit):
                if diag_off is not None and diag_off > r * rc + rc - 1:
                    continue
                rows = pl.ds(r * rc, rc)
                s = lax.dot_general(q_ref[rows, :], k, NT, preferred_element_type=F32)
                if diag_off is not None and diag_off + tk - 1 > r * rc:
                    rr = lax.broadcasted_iota(jnp.int32, s.shape, 0) + r * rc
                    cc = lax.broadcasted_iota(jnp.int32, s.shape, 1) + diag_off
                    s = jnp.where(rr >= cc, s, NEG_INF)
                m_prev = m_s[rows, :]
                m_new = jnp.maximum(m_prev, jnp.max(s, axis=1, keepdims=True))
                alpha = jnp.exp2(m_prev - m_new)
                p = jnp.exp2(s - _lanes(m_new, tk))
                l_s[rows, :] = alpha * l_s[rows, :] + jnp.sum(p, axis=1, keepdims=True)
                acc_s[rows, :] = alpha * acc_s[rows, :] + jnp.dot(p.astype(BF16), v,
                                                                  preferred_element_type=F32)
                m_s[rows, :] = m_new

        def below_diagonal(j, carry):
            kv_step(j, None, 1)
            return carry
        lax.fori_loop(0, i * ratio, below_diagonal, 0)
        for d in range(ratio):
            kv_step(i * ratio + d, d * tk, ratio)
        l = l_s[...]
        o_ref[...] = (acc_s[...] / l).astype(o_ref.dtype)
        lse = m_s[...] + jnp.log(l) * (1.0 / LN2)
        for u in range(ratio):
            lse_ref[u] = jnp.transpose(lse[u * lb:(u + 1) * lb, :])[:SUB, :]
        end(refs)

    qmap = lambda b, h, i: (b * nq + i, h)
    kmap = lambda b, h, i: (b, h)
    return pl.pallas_call(
        body, name="attn_fwd", grid=grid,
        in_specs=[pl.BlockSpec((tq, SLOT), qmap), pl.BlockSpec((S, SLOT), kmap),
                  pl.BlockSpec((S, SLOT), kmap)] + j_in,
        out_specs=[pl.BlockSpec((tq, SLOT), qmap),
                   pl.BlockSpec((None, ratio, SUB, lb), lambda b, h, i: (b * n_h + h, i, 0, 0))] + j_out,
        out_shape=[jax.ShapeDtypeStruct((T, n_h * SLOT), BF16),
                   jax.ShapeDtypeStruct((n_b * n_h, S // lb, SUB, lb), F32)] + j_shapes,
        scratch_shapes=[pltpu.VMEM((tq, SLOT), F32)] * 3 + j_scratch,
        compiler_params=_params(("arbitrary", "arbitrary", "arbitrary")))(Q, K, V, *j_args)


def attn_bwd(Q, K, V, O, dO, LSE, cosq, sinq, n_b, S, n_h, blk, scale, job=None):
    T = n_b * S
    nb = S // blk
    grid = (n_b, n_h, nb)
    j_in, j_out, j_shapes, j_scratch, begin, end, j_args = _hosted(job, 8, 4, grid)
    n_j = len(j_args)

    def body(*refs):
        q_ref, k_ref, v_ref, o_ref, do_ref, lse_ref, cos_ref, sin_ref = refs[:8]
        dk_ref, dv_ref, ga_ref, gb_ref = refs[8 + n_j:12 + n_j]
        dq_s, delta_s, dk_s, dv_s = refs[12 + 2 * n_j:16 + 2 * n_j]
        begin(refs)
        j = pl.program_id(2)

        @pl.when(j == 0)
        def _():
            dq_s[...] = jnp.zeros(dq_s.shape, F32)
            for i in range(nb):
                rows = pl.ds(i * blk, blk)
                d = jnp.sum(do_ref[rows, :].astype(F32) * o_ref[rows, :].astype(F32),
                            axis=1, keepdims=True)
                delta_s[i] = jnp.transpose(jnp.broadcast_to(d, (blk, SLOT)))[:SUB, :]

        k = k_ref[...]
        v = v_ref[...]
        dk_s[...] = jnp.zeros(dk_s.shape, F32)
        dv_s[...] = jnp.zeros(dv_s.shape, F32)

        def q_step(i, masked):
            rows = pl.ds(pl.multiple_of(i * blk, blk), blk)
            q = q_ref[rows, :]
            do = do_ref[rows, :]
            st = lax.dot_general(k, q, NT, preferred_element_type=F32)
            if masked:
                kv_i = lax.broadcasted_iota(jnp.int32, st.shape, 0)
                q_i = lax.broadcasted_iota(jnp.int32, st.shape, 1)
                st = jnp.where(q_i >= kv_i, st, NEG_INF)
            pt = jnp.exp2(st - lse_ref[i][:1, :])
            dpt = lax.dot_general(v, do, NT, preferred_element_type=F32)
            dst = (pt * (dpt - delta_s[i][:1, :])).astype(BF16)
            dv_s[...] += jnp.dot(pt.astype(BF16), do, preferred_element_type=F32)
            dk_s[...] += jnp.dot(dst, q, preferred_element_type=F32)
            dq_s[rows, :] += lax.dot_general(dst, k, TN, preferred_element_type=F32)

        q_step(j, True)

        def above_diagonal(i, carry):
            q_step(i, False)
            return carry
        lax.fori_loop(j + 1, nb, above_diagonal, 0)
        dk_ref[...] = dk_s[...] * LN2
        dv_ref[...] = dv_s[...].astype(dv_ref.dtype)

        @pl.when(j == nb - 1)
        def _():
            dq = dq_s[...] * scale
            ga_ref[...] = (dq * cos_ref[...]).astype(ga_ref.dtype)
            gb_ref[...] = (dq * sin_ref[...]).astype(gb_ref.dtype)

        end(refs)

    full = pl.BlockSpec((S, SLOT), lambda b, h, j: (b, h))
    kv = pl.BlockSpec((blk, SLOT), lambda b, h, j: (b * nb + j, h))
    tab = pl.BlockSpec((S, SLOT), lambda b, h, j: (b, 0))
    stat = pl.BlockSpec((None, nb, SUB, blk), lambda b, h, j: (b * n_h + h, 0, 0, 0))
    return pl.pallas_call(
        body, name="attn_bwd", grid=grid,
        in_specs=[full, kv, kv, full, full, stat, tab, tab] + j_in,
        out_specs=[kv, kv, full, full] + j_out,
        out_shape=[jax.ShapeDtypeStruct((T, n_h * SLOT), F32)]
        + [jax.ShapeDtypeStruct((T, n_h * SLOT), BF16)] * 3 + j_shapes,
        scratch_shapes=[pltpu.VMEM((S, SLOT), F32), pltpu.VMEM((nb, SUB, blk), F32),
                        pltpu.VMEM((blk, SLOT), F32), pltpu.VMEM((blk, SLOT), F32)] + j_scratch,
        compiler_params=_params(("arbitrary", "arbitrary", "arbitrary")))(
            Q, K, V, O, dO, LSE, cosq, sinq, *j_args)


def _layer_norm_parts(u):
    xc = u - jnp.mean(u, axis=-1, keepdims=True)
    rstd = lax.rsqrt(jnp.mean(xc * xc, axis=-1, keepdims=True) + EPS)
    return xc * rstd, rstd


def _shift_scratch(tm, C):
    return pltpu.VMEM((SUB - 1, tm + HALO - SUB, C), F32)


def _preshift(ext, sh, tm):
    for r in range(1, SUB):
        sh[r - 1] = ext[pl.ds(r, tm + HALO - SUB), :]


def _shifted(ext, sh, off, base, n):
    q, r = divmod(off, SUB)
    src = ext if r == 0 else sh.at[r - 1]
    return src[pl.ds(base + SUB * q, n), :]


CONV_ROWS = 32


def conv_fwd(cu, w, b_dw, g_cn, b_cn, tm, S):
    T, C = cu.shape
    nex, hb = S // tm, tm // HALO

    def body(cur_ref, prev_ref, w_ref, b_ref, g_ref, bc_ref, u_ref, s_ref, ext, sh):
        first = pl.program_id(0) % nex == 0
        ext[pl.ds(0, HALO), :] = jnp.where(first, 0.0, prev_ref[...])
        ext[pl.ds(HALO, tm), :] = cur_ref[...]
        _preshift(ext, sh, tm)

        acc = jnp.zeros((tm, C), F32)
        for j in range(CONV_W):
            acc = acc + w_ref[pl.ds(j, 1), :] * _shifted(ext, sh, HALO - CONV_W + 1 + j, 0, tm)
        u = acc + b_ref[...]
        ln = _layer_norm_parts(u)[0] * g_ref[...] + bc_ref[...]
        u_ref[...] = u
        s_ref[...] = (ln * _sig(ln)).astype(s_ref.dtype)

    vec = pl.BlockSpec((1, C), lambda i: (0, 0))
    return pl.pallas_call(
        body, name="conv_fwd", grid=(T // tm,),
        in_specs=[pl.BlockSpec((tm, C), lambda i: (i, 0)),
                  pl.BlockSpec((HALO, C), lambda i: (jnp.maximum(i * hb - 1, 0), 0)),
                  pl.BlockSpec((HALO, C), lambda i: (0, 0)), vec, vec, vec],
        out_specs=[pl.BlockSpec((tm, C), lambda i: (i, 0))] * 2,
        out_shape=[jax.ShapeDtypeStruct((T, C), F32), jax.ShapeDtypeStruct((T, C), BF16)],
        scratch_shapes=[pltpu.VMEM((HALO + tm, C), F32), _shift_scratch(tm, C)],
        compiler_params=_params(("arbitrary",)))(cu, cu, w, b_dw, g_cn, b_cn)


def conv_bwd(du, cu, glu_a, glu_b, w, tm, S):
    T, C = du.shape
    nex, hb = S // tm, tm // HALO
    last_blk = T // HALO - 1

    def body(du_ref, nxt_ref, cu_ref, prev_ref, a_ref, b_ref, w_ref,
             da_ref, db_ref, dw_ref, dbias_ref, extd, extc, shd, shc):
        i = pl.program_id(0)
        first = i % nex == 0
        last = i % nex == nex - 1
        extd[pl.ds(0, tm), :] = du_ref[...]
        extd[pl.ds(tm, HALO), :] = jnp.where(last, 0.0, nxt_ref[...])
        extc[pl.ds(0, HALO), :] = jnp.where(first, 0.0, prev_ref[...])
        extc[pl.ds(HALO, tm), :] = cu_ref[...]
        _preshift(extd, shd, tm)
        _preshift(extc, shc, tm)

        @pl.when(i == 0)
        def _():
            dw_ref[...] = jnp.zeros(dw_ref.shape, F32)
            dbias_ref[...] = jnp.zeros(dbias_ref.shape, F32)

        def rows_step(ci, carry):
            base = pl.multiple_of(ci * CONV_ROWS, CONV_ROWS)
            rows = pl.ds(base, CONV_ROWS)
            du_rows = du_ref[rows, :]
            dcu = jnp.zeros((CONV_ROWS, C), F32)
            for j in range(CONV_W):
                dcu = dcu + w_ref[pl.ds(j, 1), :] * _shifted(extd, shd, CONV_W - 1 - j,
                                                             base, CONV_ROWS)
                dw_ref[pl.ds(SUB * j, SUB), :] += _sum8(
                    du_rows * _shifted(extc, shc, HALO - CONV_W + 1 + j, base, CONV_ROWS))
            dbias_ref[...] += _sum8(du_rows)
            sb = _sig(b_ref[rows, :])
            da_ref[rows, :] = (dcu * sb).astype(da_ref.dtype)
            db_ref[rows, :] = (dcu * a_ref[rows, :] * sb * (1.0 - sb)).astype(db_ref.dtype)
            return carry
        lax.fori_loop(0, tm // CONV_ROWS, rows_step, 0)

    cur = pl.BlockSpec((tm, C), lambda i: (i, 0))
    return pl.pallas_call(
        body, name="conv_bwd", grid=(T // tm,),
        in_specs=[cur, pl.BlockSpec((HALO, C), lambda i: (jnp.minimum((i + 1) * hb, last_blk), 0)),
                  cur, pl.BlockSpec((HALO, C), lambda i: (jnp.maximum(i * hb - 1, 0), 0)),
                  cur, cur, pl.BlockSpec((HALO, C), lambda i: (0, 0))],
        out_specs=[cur, cur, pl.BlockSpec((HALO * SUB, C), lambda i: (0, 0)),
                   pl.BlockSpec((SUB, C), lambda i: (0, 0))],
        out_shape=[jax.ShapeDtypeStruct((T, C), BF16), jax.ShapeDtypeStruct((T, C), BF16),
                   jax.ShapeDtypeStruct((HALO * SUB, C), F32), jax.ShapeDtypeStruct((SUB, C), F32)],
        scratch_shapes=[pltpu.VMEM((tm + HALO, C), F32), pltpu.VMEM((HALO + tm, C), F32),
                        _shift_scratch(tm, C), _shift_scratch(tm, C)],
        compiler_params=_params(("arbitrary",)))(du, du, cu, cu, glu_a, glu_b, w)


def silu_small(c_all):
    def body(c_ref, o_ref):
        v = c_ref[...]
        o_ref[...] = v * _sig(v)
    return pl.pallas_call(body, name="silu_c", out_shape=jax.ShapeDtypeStruct(c_all.shape, F32))(c_all)


def ada_fwd(c_act, w_ada, b_ada):
    n_l, D, ns = w_ada.shape
    B = c_act.shape[0]
    tn = _tile(ns, 512, LANES)

    def body(c_ref, w_ref, b_ref, o_ref):
        o_ref[...] = jnp.dot(c_ref[...], w_ref[...], preferred_element_type=F32,
                             precision=lax.Precision.HIGHEST) + b_ref[...]

    return pl.pallas_call(
        body, name="ada_fwd", grid=(n_l, ns // tn),
        in_specs=[pl.BlockSpec((B, D), lambda l, j: (0, 0)),
                  pl.BlockSpec((None, D, tn), lambda l, j: (l, 0, j)),
                  pl.BlockSpec((None, 1, tn), lambda l, j: (l, 0, j))],
        out_specs=pl.BlockSpec((None, B, tn), lambda l, j: (l, 0, j)),
        out_shape=jax.ShapeDtypeStruct((n_l, B, ns), F32),
        compiler_params=_params(("parallel", "parallel")))(c_act, w_ada, b_ada.reshape(n_l, 1, ns))


def ada_bwd_w(c_act_t, dmod):
    D, B = c_act_t.shape
    n_l, _, ns = dmod.shape
    tn = _tile(ns, 512, LANES)

    def body(c_ref, d_ref, o_ref):
        o_ref[...] = jnp.dot(c_ref[...], d_ref[...], preferred_element_type=F32,
                             precision=lax.Precision.HIGHEST)

    return pl.pallas_call(
        body, name="ada_bwd_w", grid=(n_l, ns // tn),
        in_specs=[pl.BlockSpec((D, B), lambda l, j: (0, 0)),
                  pl.BlockSpec((None, B, tn), lambda l, j: (l, 0, j))],
        out_specs=pl.BlockSpec((None, D, tn), lambda l, j: (l, 0, j)),
        out_shape=jax.ShapeDtypeStruct((n_l, D, ns), F32),
        compiler_params=_params(("parallel", "parallel")))(c_act_t, dmod)


def group_sum(name, v, group):
    rows, W = v.shape
    n = rows // group

    def body(v_ref, o_ref):
        o_ref[...] = jnp.sum(v_ref[...].reshape(n, group, W), axis=1)

    return pl.pallas_call(body, name=name, out_shape=jax.ShapeDtypeStruct((n, W), F32),
                          compiler_params=_params(None))(v)


def lane_total(name, v):
    def body(v_ref, o_ref):
        o_ref[...] = jnp.broadcast_to(jnp.sum(v_ref[...], axis=1, keepdims=True), o_ref.shape)
    return pl.pallas_call(body, name=name, out_shape=jax.ShapeDtypeStruct((1, LANES), F32))(v)


def adamw(name, w, g, m, v):
    rows, cols = w.shape
    tr = _tile(rows, max(SUB, (1 << 19) // cols // SUB * SUB), SUB)

    def body(w_ref, g_ref, m_ref, v_ref, d_ref, nm_ref, nv_ref):
        gg = g_ref[...]
        nm = ADAM_B1 * m_ref[...] + (1.0 - ADAM_B1) * gg
        nv = ADAM_B2 * v_ref[...] + (1.0 - ADAM_B2) * (gg * gg)
        m_hat = nm / (1.0 - ADAM_B1 ** ADAM_STEP)
        v_hat = nv / (1.0 - ADAM_B2 ** ADAM_STEP)
        d_ref[...] = -ADAM_LR * (m_hat / (jnp.sqrt(v_hat) + ADAM_EPS) + ADAM_WD * w_ref[...])
        nm_ref[...] = nm
        nv_ref[...] = nv

    spec = pl.BlockSpec((tr, cols), lambda i: (i, 0))
    return pl.pallas_call(
        body, name=name, grid=(rows // tr,), in_specs=[spec] * 4, out_specs=[spec] * 3,
        out_shape=[jax.ShapeDtypeStruct((rows, cols), F32)] * 3,
        compiler_params=_params(("parallel",)))(w, g, m, v)


def rope_tables(pos_col, invf, one_nope, rope_mask, tm):
    T = pos_col.shape[0]

    def body(p_ref, f_ref, o_ref, r_ref, cq_ref, sq_ref, ck_ref):
        ang = p_ref[...] * f_ref[...]
        cs = jnp.cos(ang) * r_ref[...]
        cq_ref[...] = o_ref[...] + cs
        sq_ref[...] = jnp.sin(ang) * r_ref[...]
        ck_ref[...] = cs

    vec = pl.BlockSpec((1, LANES), lambda i: (0, 0))
    out = pl.BlockSpec((tm, LANES), lambda i: (i, 0))
    return pl.pallas_call(
        body, name="rope_tables", grid=(T // tm,),
        in_specs=[pl.BlockSpec((tm, 1), lambda i: (i, 0)), vec, vec, vec], out_specs=[out] * 3,
        out_shape=[jax.ShapeDtypeStruct((T, LANES), F32)] * 3,
        compiler_params=_params(("parallel",)))(pos_col, invf, one_nope, rope_mask)


def _place():
    return lax.axis_index("x"), lax.axis_index("y"), lax.axis_index("c")


def allgather8(name, v):
    R, W = v.shape

    def body(x_ref, out_ref, send_sems, recv_sems, local_sem):
        x, y, c = _place()
        me = 4 * x + 2 * y + c
        mine = pltpu.make_async_copy(x_ref, out_ref.at[me], local_sem)
        mine.start()
        sends, peers = [], []
        for k in range(1, N_DEV):
            px, py, pc = x ^ ((k >> 2) & 1), y ^ ((k >> 1) & 1), c ^ (k & 1)
            peers.append((px, py, pc))
            cp = pltpu.make_async_remote_copy(
                src_ref=x_ref, dst_ref=out_ref.at[me], send_sem=send_sems.at[k - 1],
                recv_sem=recv_sems.at[k - 1], device_id=(px, py, pc), device_id_type=MESH)
            cp.start()
            sends.append(cp)
        for k, (px, py, pc) in enumerate(peers):
            pltpu.make_async_remote_copy(
                src_ref=x_ref, dst_ref=out_ref.at[4 * px + 2 * py + pc], send_sem=send_sems.at[k],
                recv_sem=recv_sems.at[k], device_id=(px, py, pc), device_id_type=MESH).wait_recv()
        for cp in sends:
            cp.wait_send()
        mine.wait()

    return pl.pallas_call(
        body, name=name, out_shape=jax.ShapeDtypeStruct((N_DEV, R, W), v.dtype),
        in_specs=[pl.BlockSpec(memory_space=pltpu.VMEM)],
        out_specs=pl.BlockSpec(memory_space=pltpu.VMEM),
        scratch_shapes=[pltpu.SemaphoreType.DMA((N_DEV - 1,)), pltpu.SemaphoreType.DMA((N_DEV - 1,)),
                        pltpu.SemaphoreType.DMA])(v)


def _other_chips(x, y):
    return [(1 - x, y), (x, 1 - y), (1 - x, 1 - y)]


def _hbm_exchange(name, body, ins, out_shapes, n_sems):
    n_in, n_out = len(ins), len(out_shapes)

    def wrapped(*refs):
        send_sems, recv_sems = refs[n_in + n_out:]

        def copy(k, src, dst, to):
            return pltpu.make_async_remote_copy(src_ref=src, dst_ref=dst, send_sem=send_sems.at[k],
                                                recv_sem=recv_sems.at[k], device_id=to,
                                                device_id_type=MESH)
        body(refs[:n_in], refs[n_in:n_in + n_out], copy)

    hbm = pl.BlockSpec(memory_space=pl.ANY)
    return pl.pallas_call(
        wrapped, name=name, out_shape=out_shapes, in_specs=[hbm] * n_in, out_specs=[hbm] * n_out,
        scratch_shapes=[pltpu.SemaphoreType.DMA((n_sems,)), pltpu.SemaphoreType.DMA((n_sems,))])(*ins)


def _gather_start(w_refs, o_refs, copy):
    x, y, c = _place()
    for t, (w, o) in enumerate(zip(w_refs, o_refs)):
        for j, (cx, cy) in enumerate(_other_chips(x, y)):
            copy(6 * t + j, w.at[c], o.at[2 * x + y, c], (cx, cy, c)).start()


def _gather_finish(w_refs, o_refs, copy):
    x, y, c = _place()
    chips = _other_chips(x, y)
    for t, (w, o) in enumerate(zip(w_refs, o_refs)):
        for j, (cx, cy) in enumerate(chips):
            landed = o.at[2 * cx + cy, c]
            copy(6 * t + j, w.at[c], landed, (cx, cy, c)).wait_recv()
            copy(6 * t + 3 + j, landed, landed, (x, y, 1 - c)).start()
    for t, (w, o) in enumerate(zip(w_refs, o_refs)):
        for j, (cx, cy) in enumerate(chips):
            copy(6 * t + 3 + j, w.at[c], o.at[2 * cx + cy, 1 - c], (x, y, 1 - c)).wait_recv()
    for t, (w, o) in enumerate(zip(w_refs, o_refs)):
        for j, (cx, cy) in enumerate(chips):
            copy(6 * t + j, w.at[c], o.at[2 * x + y, c], (cx, cy, c)).wait_send()
            landed = o.at[2 * cx + cy, c]
            copy(6 * t + 3 + j, landed, landed, (x, y, 1 - c)).wait_send()


def weight_gather_job(ws):
    return (ws, [jax.ShapeDtypeStruct((N_CHIPS,) + w.shape, w.dtype) for w in ws], 6 * len(ws),
            _gather_start, _gather_finish)


def _exchange_start(p_refs, b_refs, copy):
    x, y, c = _place()
    for t, (p, b) in enumerate(zip(p_refs, b_refs)):
        for j, (cx, cy) in enumerate(_other_chips(x, y)):
            copy(3 * t + j, p.at[2 * cx + cy], b.at[j], (cx, cy, c)).start()


def _exchange_finish(p_refs, b_refs, copy):
    x, y, c = _place()
    for t, (p, b) in enumerate(zip(p_refs, b_refs)):
        for j, (cx, cy) in enumerate(_other_chips(x, y)):
            copy(3 * t + j, p.at[2 * cx + cy], b.at[j], (cx, cy, c)).wait()


def chip_exchange_job(ps):
    return (ps, [jax.ShapeDtypeStruct((3,) + p.shape[1:], p.dtype) for p in ps], 3 * len(ps),
            _exchange_start, _exchange_finish)


def run_job(name, job):
    ins, outs, n_sems, start, finish = job

    def body(in_refs, out_refs, copy):
        start(in_refs, out_refs, copy)
        finish(in_refs, out_refs, copy)
    return _hbm_exchange(name, body, ins, outs, n_sems)


def sibling_swap(gs):
    n = len(gs)

    def body(g_refs, a_refs, copy):
        x, y, c = _place()
        cps = []
        for t in range(n):
            cp = copy(t, g_refs[t].at[pl.ds(0, N_CHIPS), 1 - c], a_refs[t], (x, y, 1 - c))
            cp.start()
            cps.append(cp)
        for cp in cps:
            cp.wait()

    return _hbm_exchange("grad_sibling_swap", body, gs,
                         [jax.ShapeDtypeStruct(g.shape[:1] + g.shape[2:], g.dtype) for g in gs], n)


def chip_partial(name, g, a, c_idx):
    n_s, _, kh, ns = g.shape
    tr = _tile(kh, max(16, (1 << 19) // ns // 16 * 16), 16)

    def body(c_ref, g_ref, a_ref, o_ref):
        o_ref[...] = (g_ref[...] + a_ref[...]).astype(o_ref.dtype)

    return pl.pallas_call(
        body, name=name,
        grid_spec=pltpu.PrefetchScalarGridSpec(
            num_scalar_prefetch=1, grid=(n_s, kh // tr),
            in_specs=[pl.BlockSpec((None, None, tr, ns), lambda i, r, cr: (i, cr[0], r, 0)),
                      pl.BlockSpec((None, tr, ns), lambda i, r, cr: (i, r, 0))],
            out_specs=pl.BlockSpec((None, tr, ns), lambda i, r, cr: (i, r, 0))),
        out_shape=jax.ShapeDtypeStruct((n_s, kh, ns), BF16),
        compiler_params=_params(("parallel", "parallel")))(c_idx, g, a)


def shard_total(name, p, b, s_idx):
    _, rows, ns = p.shape
    tr = _tile(rows, max(16, (1 << 19) // ns // 16 * 16), 16)

    def body(s_ref, p_ref, b0, b1, b2, o_ref):
        o_ref[...] = ((p_ref[...].astype(F32) + b0[...].astype(F32)) + b1[...].astype(F32)
                      ) + b2[...].astype(F32)

    def bspec(j):
        return pl.BlockSpec((None, tr, ns), lambda r, sr: (j, r, 0))

    return pl.pallas_call(
        body, name=name,
        grid_spec=pltpu.PrefetchScalarGridSpec(
            num_scalar_prefetch=1, grid=(rows // tr,),
            in_specs=[pl.BlockSpec((None, tr, ns), lambda r, sr: (sr[0], r, 0)),
                      bspec(0), bspec(1), bspec(2)],
            out_specs=pl.BlockSpec((tr, ns), lambda r, sr: (r, 0))),
        out_shape=jax.ShapeDtypeStruct((rows, ns), F32),
        compiler_params=_params(("parallel",)))(s_idx, p, b, b, b)


def sibling_share(fs):
    n = len(fs)

    def body(f_refs, o_refs, copy):
        x, y, c = _place()
        cps = []
        for t in range(n):
            cp = copy(t, f_refs[t], o_refs[t], (x, y, 1 - c))
            cp.start()
            cps.append(cp)
        for cp in cps:
            cp.wait()

    return _hbm_exchange("grad_sibling_share", body, fs,
                         [jax.ShapeDtypeStruct(f.shape, f.dtype) for f in fs], n)


def _rot_cols(w):
    h = w.shape[-1] // 2
    return jnp.concatenate([-w[..., h:], w[..., :h]], axis=-1)


def _slots(parts, lead, n_h):
    width = sum(p.shape[-1] for p in parts)
    pad = jnp.zeros(lead + (n_h, SLOT - width), parts[0].dtype)
    return jnp.concatenate(parts + [pad], axis=-1).reshape(lead + (n_h * SLOT,))


def layout_weights(w_in, w_uq, w_ukv, w_o_attn, dims):
    QL, KVL, C, D, n_h = dims
    o = 0
    w_ql, o = w_in[..., o:o + QL], o + QL
    w_kvl, o = w_in[..., o:o + KVL], o + KVL
    w_kr, o = w_in[..., o:o + ROPE], o + ROPE
    w_glu_a, o = w_in[..., o:o + C], o + C
    w_glu_b, o = w_in[..., o:o + C], o + C
    w_ga, o = w_in[..., o:o + D], o + D
    w_gb = w_in[..., o:o + D]
    z = lambda n: jnp.zeros(w_kr.shape[:-1] + (n,), w_kr.dtype)
    kr_a = jnp.concatenate([z(NOPE), w_kr, z(SLOT - QK_DIM)], axis=-1)
    kr_b = jnp.concatenate([z(NOPE), _rot_cols(w_kr), z(SLOT - QK_DIM)], axis=-1)
    lead = w_uq.shape[:-1]
    q = w_uq.reshape(lead + (n_h, QK_DIM))
    zq = jnp.zeros(lead + (n_h, NOPE), w_uq.dtype)
    wq_a = _slots([q[..., :NOPE], q[..., NOPE:]], lead, n_h)
    wq_b = _slots([zq, _rot_cols(q[..., NOPE:])], lead, n_h)
    lead = w_ukv.shape[:-1]
    kv = w_ukv.reshape(lead + (n_h, NOPE + VDIM))
    w_kn = _slots([kv[..., :NOPE]], lead, n_h)
    w_v = _slots([kv[..., NOPE:]], lead, n_h)
    lead = w_o_attn.shape[:-2]
    wo = w_o_attn.reshape(lead + (n_h, VDIM, D))
    wo = jnp.concatenate([wo, jnp.zeros(lead + (n_h, SLOT - VDIM, D), wo.dtype)], axis=-2)
    w_o = wo.reshape(lead + (n_h * SLOT, D))
    return dict(ql=w_ql, kvl=w_kvl, kr_a=kr_a, kr_b=kr_b, glu_a=w_glu_a, glu_b=w_glu_b,
                ga=w_ga, gb=w_gb, q_a=wq_a, q_b=wq_b, kn=w_kn, v=w_v, o=w_o)


GATHERED = ("w_in", "w_uq", "w_ukv", "w_o_attn", "w_pw2", "w_out", "w_gu", "w_down")
ROW_SHARDED = ("w_out", "w_down")


def halves(a):
    return a.reshape(a.shape[:-2] + (2, a.shape[-2] // 2, a.shape[-1]))


def full_layer(name, gathered):
    n_c, _, kh, ns = gathered.shape
    if name in ROW_SHARDED:
        return gathered.reshape(n_c * 2 * kh, ns)
    return jnp.concatenate([gathered[s].reshape(2 * kh, ns) for s in range(n_c)], axis=1)


def owed_pieces(name, g):
    if name in ROW_SHARDED:
        return halves(g.reshape(N_CHIPS, g.shape[0] // N_CHIPS, g.shape[1]))
    k, n = g.shape
    return halves(jnp.transpose(g.reshape(k, N_CHIPS, n // N_CHIPS), (1, 0, 2)))


def _pad_rows8(flat):
    n = flat.shape[0]
    w = -(-n // (SUB * LANES)) * LANES
    return jnp.concatenate([flat, jnp.zeros((SUB * w - n,), flat.dtype)]).reshape(SUB, w)


def kernel(x, c, positions, w_ada, b_ada, g_mix, w_in, g_q, w_uq, g_kv, w_ukv, w_o_attn, w_dw, b_dw, g_cn, b_cn, w_pw2, w_out, g_ffn, w_gu, w_down, g_final, loss_target, m_w_ada, m_b_ada, m_g_mix, m_w_in, m_g_q, m_w_uq, m_g_kv, m_w_ukv, m_w_o_attn, m_w_dw, m_b_dw, m_g_cn, m_b_cn, m_w_pw2, m_w_out, m_g_ffn, m_w_gu, m_w_down, m_g_final, v_w_ada, v_b_ada, v_g_mix, v_w_in, v_g_q, v_w_uq, v_g_kv, v_w_ukv, v_w_o_attn, v_w_dw, v_b_dw, v_g_cn, v_b_cn, v_w_pw2, v_w_out, v_g_ffn, v_w_gu, v_w_down, v_g_final):
    weights = dict(w_ada=w_ada, b_ada=b_ada, g_mix=g_mix, w_in=w_in, g_q=g_q, w_uq=w_uq, g_kv=g_kv,
                   w_ukv=w_ukv, w_o_attn=w_o_attn, w_dw=w_dw, b_dw=b_dw, g_cn=g_cn, b_cn=b_cn,
                   w_pw2=w_pw2, w_out=w_out, g_ffn=g_ffn, w_gu=w_gu, w_down=w_down, g_final=g_final)
    mom = dict(w_ada=m_w_ada, b_ada=m_b_ada, g_mix=m_g_mix, w_in=m_w_in, g_q=m_g_q, w_uq=m_w_uq,
               g_kv=m_g_kv, w_ukv=m_w_ukv, w_o_attn=m_w_o_attn, w_dw=m_w_dw, b_dw=m_b_dw,
               g_cn=m_g_cn, b_cn=m_b_cn, w_pw2=m_w_pw2, w_out=m_w_out, g_ffn=m_g_ffn, w_gu=m_w_gu,
               w_down=m_w_down, g_final=m_g_final)
    var = dict(w_ada=v_w_ada, b_ada=v_b_ada, g_mix=v_g_mix, w_in=v_w_in, g_q=v_g_q, w_uq=v_w_uq,
               g_kv=v_g_kv, w_ukv=v_w_ukv, w_o_attn=v_w_o_attn, w_dw=v_w_dw, b_dw=v_b_dw,
               g_cn=v_g_cn, b_cn=v_b_cn, w_pw2=v_w_pw2, w_out=v_w_out, g_ffn=v_g_ffn, w_gu=v_w_gu,
               w_down=v_w_down, g_final=v_g_final)
    order = list(weights)

    n_e, S, D = x.shape
    T = n_e * S
    n_l = w_in.shape[0]
    QL, KVL, C = g_q.shape[1], g_kv.shape[1], g_cn.shape[1]
    n_h = w_uq.shape[2] * N_CHIPS // QK_DIM
    F = w_gu.shape[2] * N_CHIPS // 2
    B = n_e * N_DEV
    dims = (QL, KVL, C, D, n_h)
    scale = QK_DIM ** -0.5
    tm = _tile(S, 512, HALO)
    tm_mm = _tile(S, 1024, HALO)
    blk = _tile(S, ATTN_LB, LANES)
    tq = _tile(S, ATTN_TQ, blk)
    q_scale = scale / LN2
    mx, my, mc = _place()
    dev = 4 * mx + 2 * my + mc
    chip = 2 * mx + my
    c_idx = jnp.reshape(mc, (1,)).astype(jnp.int32)
    s_idx = jnp.reshape(chip, (1,)).astype(jnp.int32)

    xt = x.reshape(T, D)
    tgt = loss_target.reshape(T, D)

    mine = [[halves(weights[k][l].astype(BF16)) for k in GATHERED] for l in range(n_l)]
    zero = jnp.zeros((), jnp.int32)
    full = {k: [None] * n_l for k in GATHERED}
    lay = [None] * n_l

    def take_weights(l, got):
        for k, g, w in zip(GATHERED, got, mine[l]):
            g = lax.dynamic_update_slice(g, w[None], (chip.astype(jnp.int32), zero, zero, zero))
            full[k][l] = full_layer(k, g)
        lay[l] = layout_weights(full["w_in"][l], full["w_uq"][l], full["w_ukv"][l],
                                full["w_o_attn"][l], dims)

    take_weights(0, run_job("weight_allgather", weight_gather_job(mine[0])))

    ns_ada = w_ada.shape[2]
    n_c, n_wdw = n_e * D, int(np.prod(w_dw.shape))
    small_all = allgather8("gather_c_wdw", _pad_rows8(
        jnp.concatenate([c.reshape(-1), w_dw.reshape(-1)]))).reshape(N_DEV, -1)
    c_all = small_all[:, :n_c].reshape(B, D)
    wdw_parts = small_all[:, n_c:n_c + n_wdw].reshape((N_DEV,) + w_dw.shape)
    w_dw_full = jnp.concatenate([wdw_parts[2 * s] for s in range(N_CHIPS)], axis=2)
    w_dw_pad = jnp.concatenate([w_dw_full, jnp.zeros((n_l, HALO - CONV_W, C), F32)], axis=1)

    c_act = silu_small(c_all)
    b_ada_mine = lax.dynamic_slice_in_dim(b_ada, chip * ns_ada, ns_ada, axis=1)
    mod_part = ada_fwd(c_act, w_ada, b_ada_mine)
    mod_all = allgather8("gather_mod", mod_part.reshape(n_l * B, ns_ada)).reshape(
        N_DEV, n_l, B, ns_ada)
    mod_full = jnp.concatenate([mod_all[2 * s] for s in range(N_CHIPS)], axis=2)
    mod_mine = lax.dynamic_slice_in_dim(mod_full, dev * n_e, n_e, axis=1)
    mods = mod_mine.reshape(n_l, n_e, N_MOD, 1, D)


    lane = np.arange(LANES)
    in_rope = (lane >= NOPE) & (lane < QK_DIM)
    inv_freq = ROPE_THETA ** (-np.arange(0, ROPE, 2, dtype=np.float32) / ROPE)
    invf = np.where(in_rope, inv_freq[(lane - NOPE) % (ROPE // 2)], 0.0).astype(np.float32)
    cos_q, sin_q, cos_k = rope_tables(
        positions.astype(F32).reshape(T, 1), jnp.asarray(invf).reshape(1, LANES),
        jnp.asarray((lane < NOPE).astype(np.float32)).reshape(1, LANES),
        jnp.asarray(in_rope.astype(np.float32)).reshape(1, LANES), tm)

    def rope_epi(accs, ex):
        n = accs[0].shape[1]
        return (accs[0] * _lanes(ex[0], n) + accs[1] * _lanes(ex[1], n),)

    def rms_epi(accs, ex):
        a = accs[0]
        return a, a * _rstd(a) * ex[0]

    def modnorm(name, xin, g, sc, sh):
        def fn(xv, gv, scv, shv):
            return (xv * _rstd(xv) * gv * (1.0 + scv) + shv,)
        return rowwise(name, [('tile', xin), ('vec', g), ('exvec', sc), ('exvec', sh)],
                       [('tile', D, BF16)], fn, tm, S)[0]

    def modnorm_bwd(name, dh, xin, dres, g, sc):
        def fn(dhv, xv, drv, gv, scv):
            rstd = _rstd(xv)
            xhat = xv * rstd
            dx = _norm_bwd(dhv * gv * (1.0 + scv), xhat, rstd) + drv
            return dx, dhv, dhv * xhat * gv, dhv * xhat * (1.0 + scv)
        return rowwise(name, [('tile', dh), ('tile', xin), ('tile', dres), ('vec', g), ('exvec', sc)],
                       [('tile', D, F32), ('exacc', D), ('exacc', D), ('acc', D)], fn, tm, S)

    def rms_bwd(name, dy, xin, g, width):
        def fn(dyv, xv, gv):
            rstd = _rstd(xv)
            xhat = xv * rstd
            return _norm_bwd(dyv * gv, xhat, rstd), dyv * xhat
        return rowwise(name, [('tile', dy), ('tile', xin), ('vec', g)],
                       [('tile', width, BF16), ('acc', width)], fn, tm, S)

    def gate_bwd(name, dxo, branch, gt):
        def fn(dv, bv, gv):
            return dv * gv, dv * bv
        return rowwise(name, [('tile', dxo), ('tile', branch), ('exvec', gt)],
                       [('tile', D, BF16), ('exacc', D)], fn, tm, S)

    tn_d = _tile(D, 512, LANES)
    tn_f = _tile(F, 1536, LANES)
    tn_s = _tile(n_h * SLOT, 512, LANES)
    row = lambda a: a.reshape(1, -1)

    saved = []
    xc = xt
    for l in range(n_l):
        W = lay[l]
        sh1, sc1, gt1, sh2, sc2, gt2 = [mods[l, :, k] for k in range(N_MOD)]
        h1 = modnorm("modnorm_mix", xc, row(g_mix[l]), sc1, sh1)
        ga, gb = mm_fused("proj_gates", [h1], [(0, W["ga"]), (0, W["gb"])], [], lambda a, e: a,
                          [F32, F32], tm_mm, tn_d, S)
        cu, glu_a, glu_b = mm_fused(
            "proj_glu", [h1], [(0, W["glu_a"]), (0, W["glu_b"])], [],
            lambda a, e: (a[0] * _sig(a[1]), a[0], a[1]), [F32, F32, F32], tm, C, S)
        q_lat, qn = mm_fused("proj_q_lat", [h1], [(0, W["ql"])], [('vec', row(g_q[l]))], rms_epi,
                             [F32, BF16], tm, QL, S)
        kv_lat, kvn = mm_fused("proj_kv_lat", [h1], [(0, W["kvl"])], [('vec', row(g_kv[l]))],
                               rms_epi, [F32, BF16], tm, KVL, S)
        kr = mm_fused("proj_k_rope", [h1], [(0, W["kr_a"]), (0, W["kr_b"])],
                      [('row128', cos_k), ('row128', sin_q)], rope_epi, [F32], tm, SLOT, S)[0]
        q_all = mm_fused("q_up", [qn], [(0, W["q_a"]), (0, W["q_b"])],
                         [('row128', cos_q), ('row128', sin_q)],
                         lambda a, e: (rope_epi(a, e)[0] * q_scale,), [BF16], tm, tn_s, S)[0]
        k_all, v_all = mm_fused(
            "kv_up", [kvn], [(0, W["kn"]), (0, W["v"])], [('row128', kr)],
            lambda a, e: (a[0] + _lanes(e[0], a[0].shape[1]), a[1]), [BF16, BF16], tm, tn_s, S)
        o_all, lse, *got = attn_fwd(q_all, k_all, v_all, n_e, S, n_h, tq, blk,
                                    job=weight_gather_job(mine[l + 1]) if l + 1 < n_l else None)
        if l + 1 < n_l:
            take_weights(l + 1, got)
        u, s_act = conv_fwd(cu, w_dw_pad[l], row(b_dw[l]), row(g_cn[l]), row(b_cn[l]), tm, S)

        def merge_epi(a, e):
            return _sig(e[0]) * a[0] + _sig(e[1]) * a[1], a[0], a[1]
        y, ya, yc = mm_fused("merge", [o_all, s_act], [(0, W["o"]), (1, full["w_pw2"][l])],
                             [('tile', ga), ('tile', gb)], merge_epi, [BF16, F32, F32], tm_mm, tn_d, S)
        x2, o_mix = mm_fused("mix_out", [y], [(0, full["w_out"][l])], [('tile', xc), ('exvec', gt1)],
                             lambda a, e: (e[0] + e[1] * a[0], a[0]), [F32, F32], tm_mm, tn_d, S)
        h2 = modnorm("modnorm_ffn", x2, row(g_ffn[l]), sc2, sh2)

        def swiglu_epi(a, e):
            return a[0], a[1], a[0] * _sig(a[0]) * a[1]
        w_gu_l = full["w_gu"][l]
        g_act, up, act = mm_fused("ffn_up", [h2], [(0, w_gu_l), (0, w_gu_l, False, F // tn_f)], [],
                                  swiglu_epi, [BF16, BF16, BF16], tm_mm, tn_f, S, N=F)
        x3, dn = mm_fused("ffn_down", [act], [(0, full["w_down"][l])], [('tile', x2), ('exvec', gt2)],
                          lambda a, e: (e[0] + e[1] * a[0], a[0]), [F32, F32], tm_mm, tn_d, S)
        saved.append(dict(x=xc, h1=h1, ga=ga, gb=gb, cu=cu, glu_a=glu_a, glu_b=glu_b, q_lat=q_lat,
                          qn=qn, kv_lat=kv_lat, kvn=kvn, q_all=q_all, k_all=k_all, v_all=v_all,
                          o_all=o_all, lse=lse, u=u, s_act=s_act, y=y, ya=ya, yc=yc, x2=x2,
                          o_mix=o_mix, h2=h2, g_act=g_act, up=up, act=act, dn=dn))
        xc = x3

    def loss_fn(xv, tv, gv):
        rstd = _rstd(xv)
        xhat = xv * rstd
        err = xhat * gv - tv
        dy = err * (1.0 / D)
        return _norm_bwd(dy * gv, xhat, rstd), err * err * (0.5 / D), dy * xhat
    dxc, loss_acc, dg_final = rowwise("loss_head", [('tile', xc), ('tile', tgt), ('vec', row(g_final))],
                                      [('tile', D, F32), ('acc', D), ('acc', D)], loss_fn, tm, S)

    gfull = {k: [None] * n_l for k in GATHERED}
    g_wdw = [None] * n_l
    small_acc = {k: [None] * n_l for k in ("g_mix", "g_q", "g_kv", "b_dw", "g_cn", "b_cn", "g_ffn")}
    dmod_acc = [None] * n_l
    parts, received = [None] * n_l, [None] * n_l
    lay_T = jax.linear_transpose(
        lambda a, b, cc, d: layout_weights(a, b, cc, d, dims),
        *[jax.ShapeDtypeStruct(full[k][0].shape, F32) for k in ("w_in", "w_uq", "w_ukv", "w_o_attn")])

    for l in reversed(range(n_l)):
        sv = saved[l]
        W = lay[l]
        sh1, sc1, gt1, sh2, sc2, gt2 = [mods[l, :, k] for k in range(N_MOD)]
        w_gu_l = full["w_gu"][l]
        ddn, dgt2 = gate_bwd("ffn_gate_bwd", dxc, sv["dn"], gt2)

        def swiglu_bwd_epi(a, e):
            gv, uv = e[0].astype(F32), e[1].astype(F32)
            sg = _sig(gv)
            return a[0] * uv * sg * (1.0 + gv * (1.0 - sg)), a[0] * gv * sg
        dg_act, dup = mm_fused("ffn_down_bwd", [ddn], [(0, full["w_down"][l], True)],
                               [('tile', sv["g_act"]), ('tile', sv["up"])], swiglu_bwd_epi,
                               [BF16, BF16], tm_mm, tn_f, S)
        gfull["w_down"][l] = mm_tn("ffn_down_dw", sv["act"], ddn)
        gfull["w_gu"][l] = jnp.concatenate([mm_tn("ffn_gate_dw", sv["h2"], dg_act),
                                            mm_tn("ffn_up_dw", sv["h2"], dup)], axis=1)
        dh2 = mm_fused("ffn_up_bwd", [dg_act, dup], [(0, w_gu_l, True, 0), (1, w_gu_l, True, 1)], [],
                       lambda a, e: (a[0] + a[1],), [F32], tm_mm, tn_d, S)[0]
        dx2, dsh2, dsc2, dg_ffn = modnorm_bwd("modnorm_ffn_bwd", dh2, sv["x2"], dxc,
                                              row(g_ffn[l]), sc2)
        ddo, dgt1 = gate_bwd("mix_gate_bwd", dx2, sv["o_mix"], gt1)

        def merge_bwd_epi(a, e):
            sa, sb = _sig(e[0]), _sig(e[1])
            dy = a[0]
            return dy * sa, dy * sb, dy * e[2] * sa * (1.0 - sa), dy * e[3] * sb * (1.0 - sb)
        dya, dyc, dga, dgb = mm_fused(
            "mix_out_bwd", [ddo], [(0, full["w_out"][l], True)],
            [('tile', sv["ga"]), ('tile', sv["gb"]), ('tile', sv["ya"]), ('tile', sv["yc"])],
            merge_bwd_epi, [BF16] * 4, tm_mm, tn_d, S)
        gfull["w_out"][l] = mm_tn("mix_out_dw", sv["y"], ddo)
        do_all = mm_fused("attn_out_bwd", [dya], [(0, W["o"], True)], [], lambda a, e: a, [BF16],
                          tm, tn_s, S)[0]
        d_wo = mm_tn("attn_out_dw", sv["o_all"], dya)
        ds_act = mm_fused("conv_out_bwd", [dyc], [(0, full["w_pw2"][l], True)], [], lambda a, e: a,
                          [F32], tm, C, S)[0]
        gfull["w_pw2"][l] = mm_tn("conv_out_dw", sv["s_act"], dyc)

        def ln_silu_bwd(dsv, uv, gv, bv):
            xhat, rstd = _layer_norm_parts(uv)
            ln = xhat * gv + bv
            sg = _sig(ln)
            dln = dsv * sg * (1.0 + ln * (1.0 - sg))
            dxhat = dln * gv
            du_ = rstd * (dxhat - jnp.mean(dxhat, axis=-1, keepdims=True)
                          - xhat * jnp.mean(dxhat * xhat, axis=-1, keepdims=True))
            return du_, dln * xhat, dln
        du, dg_cn, db_cn = rowwise(
            "conv_norm_bwd", [('tile', ds_act), ('tile', sv["u"]), ('vec', row(g_cn[l])),
                              ('vec', row(b_cn[l]))],
            [('tile', C, F32), ('acc', C), ('acc', C)], ln_silu_bwd, tm, S)
        dglu_a, dglu_b, dw_acc, db_dw = conv_bwd(du, sv["cu"], sv["glu_a"], sv["glu_b"],
                                                 w_dw_pad[l], tm, S)
        g_wdw[l] = group_sum("conv_dw_rows", dw_acc, SUB)[:CONV_W]
        dk_all, dv_all, gq_a, gq_b, *got = attn_bwd(
            sv["q_all"], sv["k_all"], sv["v_all"], sv["o_all"], do_all, sv["lse"], cos_q, sin_q,
            n_e, S, n_h, blk, scale, job=chip_exchange_job(parts[l + 1]) if l + 1 < n_l else None)
        if l + 1 < n_l:
            received[l + 1] = got
        dqn = mm_fused("q_up_bwd", [gq_a, gq_b], [(0, W["q_a"], True), (1, W["q_b"], True)], [],
                       lambda a, e: (a[0] + a[1],), [F32], tm, QL, S)[0]
        d_wqa = mm_tn("q_up_dw_a", sv["qn"], gq_a)
        d_wqb = mm_tn("q_up_dw_b", sv["qn"], gq_b)
        dq_lat, dg_q = rms_bwd("q_norm_bwd", dqn, sv["q_lat"], row(g_q[l]), QL)

        def k_split(dkv, ckv, skv):
            tot = dkv[:, :SLOT]
            for h in range(1, n_h):
                tot = tot + dkv[:, h * SLOT:(h + 1) * SLOT]
            return dkv, tot * ckv, tot * skv
        dk_b, dkr_a, dkr_b = rowwise("k_rope_bwd", [('tile', dk_all), ('tile', cos_k), ('tile', sin_q)],
                                     [('tile', n_h * SLOT, BF16), ('tile', SLOT, BF16),
                                      ('tile', SLOT, BF16)], k_split, tm, S)
        dkvn = mm_fused("kv_up_bwd", [dk_b, dv_all], [(0, W["kn"], True), (1, W["v"], True)], [],
                        lambda a, e: (a[0] + a[1],), [F32], tm, KVL, S)[0]
        d_wkn = mm_tn("kv_up_dw_k", sv["kvn"], dk_b)
        d_wv = mm_tn("kv_up_dw_v", sv["kvn"], dv_all)
        dkv_lat, dg_kv = rms_bwd("kv_norm_bwd", dkvn, sv["kv_lat"], row(g_kv[l]), KVL)
        segs = [("ga", dga), ("gb", dgb), ("glu_a", dglu_a), ("glu_b", dglu_b), ("ql", dq_lat),
                ("kvl", dkv_lat), ("kr_a", dkr_a), ("kr_b", dkr_b)]
        dh1 = mm_fused("proj_bwd", [g for _, g in segs],
                       [(k, W[nm], True) for k, (nm, _) in enumerate(segs)],
                       [], lambda a, e: (functools.reduce(lambda p, q: p + q, a),), [F32],
                       tm_mm, tn_d, S)[0]
        d_lay = {nm: mm_tn("proj_dw_" + nm, sv["h1"], g) for nm, g in segs}
        d_lay.update(q_a=d_wqa, q_b=d_wqb, kn=d_wkn, v=d_wv, o=d_wo)
        (gfull["w_in"][l], gfull["w_uq"][l], gfull["w_ukv"][l],
         gfull["w_o_attn"][l]) = lay_T({k: d_lay[k] for k in lay[l]})
        dxc, dsh1, dsc1, dg_mix = modnorm_bwd("modnorm_mix_bwd", dh1, sv["x"], dx2,
                                              row(g_mix[l]), sc1)
        dmod_acc[l] = [dsh1, dsc1, dgt1, dsh2, dsc2, dgt2]
        for k, a in (("g_mix", dg_mix), ("g_q", dg_q), ("g_kv", dg_kv), ("b_dw", db_dw),
                     ("g_cn", dg_cn), ("b_cn", db_cn), ("g_ffn", dg_ffn)):
            small_acc[k][l] = a
        owed = [owed_pieces(k, gfull[k][l]) for k in GATHERED]
        parts[l] = [chip_partial("grad_chip_partial_" + k, g, a, c_idx)
                    for k, g, a in zip(GATHERED, owed, sibling_swap(owed))]

    received[0] = run_job("grad_chip_exchange", chip_exchange_job(parts[0]))
    grad_x = dxc.reshape(n_e, S, D)

    dmod_rows = jnp.concatenate([a for l in range(n_l) for a in dmod_acc[l]], axis=0)
    dmod_own = group_sum("dmod_rows", dmod_rows, SUB).reshape(n_l, N_MOD, n_e, D)
    dmod_own = jnp.transpose(dmod_own, (0, 2, 1, 3)).reshape(n_l * n_e, N_MOD * D)
    dmod_all = allgather8("gather_dmod", dmod_own).reshape(N_DEV, n_l, n_e, N_MOD * D)
    dmod_all = jnp.transpose(dmod_all, (1, 0, 2, 3)).reshape(n_l, B, N_MOD * D)
    dmod_mine = lax.dynamic_slice_in_dim(dmod_all, chip * ns_ada, ns_ada, axis=2)
    grad_w_ada = ada_bwd_w(jnp.transpose(c_act), dmod_mine)
    grad_b_ada = group_sum("grad_b_ada", dmod_all.reshape(n_l * B, N_MOD * D), B)

    mine_half = [shard_total("grad_shard_total_" + k, p, b, s_idx)
                 for l in range(n_l) for k, p, b in zip(GATHERED, parts[l], received[l])]
    both = []
    for half, other in zip(mine_half, sibling_share(mine_half)):
        kh, ns = half.shape
        both.append(lax.dynamic_update_slice(
            jnp.broadcast_to(other[None], (2, kh, ns)), half[None],
            (mc.astype(jnp.int32), zero, zero)).reshape(2 * kh, ns))
    n_g = len(GATHERED)
    red = {k: jnp.stack([both[l * n_g + t] for l in range(n_l)]) for t, k in enumerate(GATHERED)}

    wdw_full_g = jnp.stack(g_wdw)
    pieces = [loss_acc, dg_final] + [small_acc[k][l] for k in small_acc for l in range(n_l)]
    widths = [p.shape[1] for p in pieces]
    n_acc = sum(widths)
    wdw_blk = _pad_rows8(wdw_full_g.reshape(-1))
    wdw_w = wdw_blk.shape[1]
    gathered_small = allgather8("gather_small_grads", jnp.concatenate(pieces + [wdw_blk], axis=1))
    acc_sum = group_sum("small_total", gathered_small[:, :, :n_acc].reshape(N_DEV * SUB, n_acc),
                        N_DEV * SUB)
    wdw_sum = group_sum("wdw_total", gathered_small[:, :, n_acc:].reshape(N_DEV, SUB * wdw_w), N_DEV)
    offs = np.cumsum([0] + widths)
    take = lambda i: acc_sum[:, offs[i]:offs[i + 1]]
    loss = lane_total("loss_total", take(0))[0, 0]
    g_small = {"g_final": take(1).reshape(-1)}
    i = 2
    for k in small_acc:
        g_small[k] = jnp.concatenate([take(i + l) for l in range(n_l)], axis=0)
        i += n_l
    wdw_total = wdw_sum.reshape(-1)[:n_wdw * N_CHIPS].reshape(wdw_full_g.shape)
    grad_w_dw = lax.dynamic_slice_in_dim(wdw_total, chip * w_dw.shape[2], w_dw.shape[2], axis=2)

    grads = dict(w_ada=grad_w_ada, b_ada=grad_b_ada, w_dw=grad_w_dw, **g_small, **red)

    deltas, new_m, new_v = {}, {}, {}
    for k in order:
        shp = weights[k].shape
        two = (1, shp[0]) if len(shp) == 1 else (int(np.prod(shp[:-1])), shp[-1])
        d, nm, nv = adamw("adamw_" + k, weights[k].reshape(two), grads[k].reshape(two),
                          mom[k].reshape(two), var[k].reshape(two))
        deltas[k], new_m[k], new_v[k] = d.reshape(shp), nm.reshape(shp), nv.reshape(shp)
        grads[k] = grads[k].reshape(shp)

    return (loss, grad_x, *[grads[k] for k in order], *[deltas[k] for k in order],
            *[new_m[k] for k in order], *[new_v[k] for k in order])
```

```python
import functools

import numpy as np
import jax
import jax.numpy as jnp
from jax import lax
from jax.experimental import pallas as pl
from jax.experimental.pallas import tpu as pltpu

F32 = jnp.float32
BF16 = jnp.bfloat16
MESH = pl.DeviceIdType.MESH

EPS = 1e-6
NEG_INF = -1e30
NOPE, ROPE, VDIM = 64, 32, 64
QK_DIM = NOPE + ROPE
SLOT = 128
CONV_W = 31
HALO = 32
N_MOD = 6
ROPE_THETA = 10000.0
N_CHIPS = 4
N_DEV = 8
SUB = 8
LANES = 128
VMEM_LIMIT = 56 * 1024 * 1024

ADAM_LR, ADAM_B1, ADAM_B2, ADAM_EPS, ADAM_WD, ADAM_STEP = 0.001, 0.9, 0.999, 1e-08, 0.01, 10


def _tile(n, cap, mult):
    best = None
    for d in range(mult, min(n, cap) + 1, mult):
        if n % d == 0:
            best = d
    return best if best is not None else n


def _params(sem):
    return pltpu.CompilerParams(dimension_semantics=sem, vmem_limit_bytes=VMEM_LIMIT)


def _sig(x):
    return 1.0 / (1.0 + jnp.exp(-x))


def _sum8(x):
    r, w = x.shape
    return jnp.sum(x.reshape(r // SUB, SUB, w), axis=0)


def _lanes(v, n):
    return v if n == v.shape[1] else jnp.tile(v, (1, n // v.shape[1]))


def _rstd(x):
    return lax.rsqrt(jnp.mean(x * x, axis=-1, keepdims=True) + EPS)


def _norm_bwd(dxhat, xhat, rstd):
    return rstd * (dxhat - xhat * jnp.mean(dxhat * xhat, axis=-1, keepdims=True))


def mm_fused(name, As, pairs, extras, epilogue, out_dtypes, tm, tn, S, N=None):
    T = As[0].shape[0]
    pairs = [(p[0], p[1], p[2] if len(p) > 2 else False, p[3] if len(p) > 3 else 0) for p in pairs]
    if N is None:
        N = pairs[0][1].shape[0] if pairs[0][2] else pairs[0][1].shape[1]
    nex = S // tm
    in_specs, args = [], []
    for a in As:
        in_specs.append(pl.BlockSpec((tm, a.shape[1]), lambda i, j: (i, 0)))
        args.append(a)
    for ai, b, trans, off in pairs:
        kdim = As[ai].shape[1]
        if trans:
            in_specs.append(pl.BlockSpec((tn, kdim), lambda i, j, off=off: (j, off)))
        else:
            in_specs.append(pl.BlockSpec((kdim, tn), lambda i, j, off=off: (0, j + off)))
        args.append(b)
    for kind, arr in extras:
        if kind == 'tile':
            in_specs.append(pl.BlockSpec((tm, tn), lambda i, j: (i, j)))
        elif kind == 'row128':
            in_specs.append(pl.BlockSpec((tm, LANES), lambda i, j: (i, 0)))
        elif kind == 'vec':
            in_specs.append(pl.BlockSpec((1, tn), lambda i, j: (0, j)))
        else:
            in_specs.append(pl.BlockSpec((None, 1, tn), lambda i, j: (i // nex, 0, j)))
        args.append(arr)
    n_a, n_p, n_e = len(As), len(pairs), len(extras)

    def body(*refs):
        a_refs, b_refs = refs[:n_a], refs[n_a:n_a + n_p]
        e_refs, o_refs = refs[n_a + n_p:n_a + n_p + n_e], refs[n_a + n_p + n_e:]
        accs = [lax.dot_general(a_refs[ai][...], b_refs[k][...], NT if trans else NN,
                                preferred_element_type=F32)
                for k, (ai, _, trans, _) in enumerate(pairs)]
        outs = epilogue(accs, [r[...] for r in e_refs])
        for o_ref, o in zip(o_refs, outs):
            o_ref[...] = o.astype(o_ref.dtype)

    return pl.pallas_call(
        body, name=name, grid=(T // tm, N // tn), in_specs=in_specs,
        out_specs=[pl.BlockSpec((tm, tn), lambda i, j: (i, j)) for _ in out_dtypes],
        out_shape=[jax.ShapeDtypeStruct((T, N), dt) for dt in out_dtypes],
        compiler_params=_params(("parallel", "parallel")))(*args)


def mm_tn(name, A, G):
    T, K = A.shape
    N = G.shape[1]
    tt = _tile(T, 1024, 16)
    tk = _tile(K, 1536, LANES)
    tn = _tile(N, 1536, LANES)

    def body(a_ref, g_ref, o_ref):
        part = lax.dot_general(a_ref[...], g_ref[...], (((0,), (0,)), ((), ())),
                               preferred_element_type=F32)

        @pl.when(pl.program_id(2) == 0)
        def _():
            o_ref[...] = part

        @pl.when(pl.program_id(2) > 0)
        def _():
            o_ref[...] += part

    return pl.pallas_call(
        body, name=name, grid=(K // tk, N // tn, T // tt),
        in_specs=[pl.BlockSpec((tt, tk), lambda k, n, t: (t, k)),
                  pl.BlockSpec((tt, tn), lambda k, n, t: (t, n))],
        out_specs=pl.BlockSpec((tk, tn), lambda k, n, t: (k, n)),
        out_shape=jax.ShapeDtypeStruct((K, N), F32),
        compiler_params=_params(("parallel", "parallel", "arbitrary")))(A, G)


def rowwise(name, ins, outs, fn, tm, S):
    T = next(a.shape[0] for k, a in ins if k == 'tile')
    nex = S // tm
    n_ex = T // S
    in_specs, args = [], []
    for kind, arr in ins:
        if kind == 'tile':
            in_specs.append(pl.BlockSpec((tm, arr.shape[1]), lambda i: (i, 0)))
        elif kind == 'vec':
            in_specs.append(pl.BlockSpec((1, arr.shape[1]), lambda i: (0, 0)))
        else:
            in_specs.append(pl.BlockSpec((None, 1, arr.shape[2]), lambda i: (i // nex, 0, 0)))
        args.append(arr)
    out_specs, out_shape = [], []
    for o in outs:
        if o[0] == 'tile':
            out_specs.append(pl.BlockSpec((tm, o[1]), lambda i: (i, 0)))
            out_shape.append(jax.ShapeDtypeStruct((T, o[1]), o[2]))
        elif o[0] == 'acc':
            out_specs.append(pl.BlockSpec((SUB, o[1]), lambda i: (0, 0)))
            out_shape.append(jax.ShapeDtypeStruct((SUB, o[1]), F32))
        else:
            out_specs.append(pl.BlockSpec((SUB, o[1]), lambda i: (i // nex, 0)))
            out_shape.append(jax.ShapeDtypeStruct((n_ex * SUB, o[1]), F32))
    n_in = len(ins)

    def body(*refs):
        i = pl.program_id(0)
        vals = fn(*[r[...] for r in refs[:n_in]])
        for o, o_ref, v in zip(outs, refs[n_in:], vals):
            if o[0] == 'tile':
                o_ref[...] = v.astype(o_ref.dtype)
            else:
                part = _sum8(v)
                first = (i == 0) if o[0] == 'acc' else (i % nex == 0)

                @pl.when(first)
                def _(o_ref=o_ref, part=part):
                    o_ref[...] = part

                @pl.when(jnp.logical_not(first))
                def _(o_ref=o_ref, part=part):
                    o_ref[...] += part

    return pl.pallas_call(
        body, name=name, grid=(T // tm,), in_specs=in_specs, out_specs=out_specs,
        out_shape=out_shape, compiler_params=_params(("arbitrary",)))(*args)


NN = (((1,), (0,)), ((), ()))
NT = (((1,), (1,)), ((), ()))
TN = (((0,), (0,)), ((), ()))
LN2 = 0.6931471805599453
ATTN_TQ = 1024
ATTN_LB = 512


def _hosted(job, n_in, n_out, grid):
    if job is None:
        return [], [], [], [], (lambda refs: None), (lambda refs: None), []
    ins, outs, n_sems, start, finish = job
    hbm = pl.BlockSpec(memory_space=pl.ANY)
    n_j = len(ins)

    def split(refs):
        j_in = refs[n_in:n_in + n_j]
        j_out = refs[n_in + n_j + n_out:n_in + n_j + n_out + n_j]
        send_sems, recv_sems = refs[-2:]

        def copy(k, src, dst, to):
            return pltpu.make_async_remote_copy(src_ref=src, dst_ref=dst, send_sem=send_sems.at[k],
                                                recv_sem=recv_sems.at[k], device_id=to,
                                                device_id_type=MESH)
        return j_in, j_out, copy

    def at(step_of):
        cond = None
        for axis, size in enumerate(grid):
            hit = pl.program_id(axis) == step_of(size)
            cond = hit if cond is None else jnp.logical_and(cond, hit)
        return cond

    def begin(refs):
        @pl.when(at(lambda size: 0))
        def _():
            start(*split(refs))

    def end(refs):
        @pl.when(at(lambda size: size - 1))
        def _():
            finish(*split(refs))

    sems = [pltpu.SemaphoreType.DMA((n_sems,)), pltpu.SemaphoreType.DMA((n_sems,))]
    return [hbm] * n_j, [hbm] * n_j, list(outs), sems, begin, end, list(ins)


def attn_fwd(Q, K, V, n_b, S, n_h, tq, lb, job=None):
    T = n_b * S
    nq = S // tq
    ratio = tq // lb
    tk = lb
    grid = (n_b, n_h, nq)
    j_in, j_out, j_shapes, j_scratch, begin, end, j_args = _hosted(job, 3, 2, grid)
    n_j = len(j_args)

    def body(*refs):
        q_ref, k_ref, v_ref = refs[:3]
        o_ref, lse_ref = refs[3 + n_j:5 + n_j]
        m_s, l_s, acc_s = refs[5 + 2 * n_j:8 + 2 * n_j]
        begin(refs)
        i = pl.program_id(2)
        m_s[...] = jnp.full(m_s.shape, NEG_INF, F32)
        l_s[...] = jnp.zeros(l_s.shape, F32)
        acc_s[...] = jnp.zeros(acc_s.shape, F32)

        def kv_step(j, diag_off, rsplit):
            start = pl.multiple_of(j * tk, tk)
            k = k_ref[pl.ds(start, tk), :]
            v = v_ref[pl.ds(start, tk), :]
            rc = tq // rsplit
            for r in range(rsplit):
                if diag_off is not None and diag_off > r * rc + rc - 1:
                    continue
                rows = pl.ds(r * rc, rc)
                s = lax.dot_general(q_ref[rows, :], k, NT, preferred_element_type=F32)
                if diag_off is not None and diag_off + tk - 1 > r * rc:
                    rr = lax.broadcasted_iota(jnp.int32, s.shape, 0) + r * rc
                    cc = lax.broadcasted_iota(jnp.int32, s.shape, 1) + diag_off
                    s = jnp.where(rr >= cc, s, NEG_INF)
                m_prev = m_s[rows, :]
                m_new = jnp.maximum(m_prev, jnp.max(s, axis=1, keepdims=True))
                alpha = jnp.exp2(m_prev - m_new)
                p = jnp.exp2(s - _lanes(m_new, tk))
                l_s[rows, :] = alpha * l_s[rows, :] + jnp.sum(p, axis=1, keepdims=True)
                acc_s[rows, :] = alpha * acc_s[rows, :] + jnp.dot(p.astype(BF16), v,
                                                                  preferred_element_type=F32)
                m_s[rows, :] = m_new

        def below_diagonal(j, carry):
            kv_step(j, None, 1)
            return carry
        lax.fori_loop(0, i * ratio, below_diagonal, 0)
        for d in range(ratio):
            kv_step(i * ratio + d, d * tk, ratio)
        l = l_s[...]
        o_ref[...] = (acc_s[...] / l).astype(o_ref.dtype)
        lse = m_s[...] + jnp.log(l) * (1.0 / LN2)
        for u in range(ratio):
            lse_ref[u] = jnp.transpose(lse[u * lb:(u + 1) * lb, :])[:SUB, :]
        end(refs)

    qmap = lambda b, h, i: (b * nq + i, h)
    kmap = lambda b, h, i: (b, h)
    return pl.pallas_call(
        body, name="attn_fwd", grid=grid,
        in_specs=[pl.BlockSpec((tq, SLOT), qmap), pl.BlockSpec((S, SLOT), kmap),
                  pl.BlockSpec((S, SLOT), kmap)] + j_in,
        out_specs=[pl.BlockSpec((tq, SLOT), qmap),
                   pl.BlockSpec((None, ratio, SUB, lb), lambda b, h, i: (b * n_h + h, i, 0, 0))] + j_out,
        out_shape=[jax.ShapeDtypeStruct((T, n_h * SLOT), BF16),
                   jax.ShapeDtypeStruct((n_b * n_h, S // lb, SUB, lb), F32)] + j_shapes,
        scratch_shapes=[pltpu.VMEM((tq, SLOT), F32)] * 3 + j_scratch,
        compiler_params=_params(("arbitrary", "arbitrary", "arbitrary")))(Q, K, V, *j_args)


def attn_bwd(Q, K, V, O, dO, LSE, cosq, sinq, n_b, S, n_h, blk, scale, job=None):
    T = n_b * S
    nb = S // blk
    grid = (n_b, n_h, nb)
    j_in, j_out, j_shapes, j_scratch, begin, end, j_args = _hosted(job, 8, 4, grid)
    n_j = len(j_args)

    def body(*refs):
        q_ref, k_ref, v_ref, o_ref, do_ref, lse_ref, cos_ref, sin_ref = refs[:8]
        dk_ref, dv_ref, ga_ref, gb_ref = refs[8 + n_j:12 + n_j]
        dq_s, delta_s, dk_s, dv_s = refs[12 + 2 * n_j:16 + 2 * n_j]
        begin(refs)
        j = pl.program_id(2)

        @pl.when(j == 0)
        def _():
            dq_s[...] = jnp.zeros(dq_s.shape, F32)
            for i in range(nb):
                rows = pl.ds(i * blk, blk)
                d = jnp.sum(do_ref[rows, :].astype(F32) * o_ref[rows, :].astype(F32),
                            axis=1, keepdims=True)
                delta_s[i] = jnp.transpose(jnp.broadcast_to(d, (blk, SLOT)))[:SUB, :]

        k = k_ref[...]
        v = v_ref[...]
        dk_s[...] = jnp.zeros(dk_s.shape, F32)
        dv_s[...] = jnp.zeros(dv_s.shape, F32)

        def q_step(i, masked):
            rows = pl.ds(pl.multiple_of(i * blk, blk), blk)
            q = q_ref[rows, :]
            do = do_ref[rows, :]
            st = lax.dot_general(k, q, NT, preferred_element_type=F32)
            if masked:
                kv_i = lax.broadcasted_iota(jnp.int32, st.shape, 0)
                q_i = lax.broadcasted_iota(jnp.int32, st.shape, 1)
                st = jnp.where(q_i >= kv_i, st, NEG_INF)
            pt = jnp.exp2(st - lse_ref[i][:1, :])
            dpt = lax.dot_general(v, do, NT, preferred_element_type=F32)
            dst = (pt * (dpt - delta_s[i][:1, :])).astype(BF16)
            dv_s[...] += jnp.dot(pt.astype(BF16), do, preferred_element_type=F32)
            dk_s[...] += jnp.dot(dst, q, preferred_element_type=F32)
            dq_s[rows, :] += lax.dot_general(dst, k, TN, preferred_element_type=F32)

        q_step(j, True)

        def above_diagonal(i, carry):
            q_step(i, False)
            return carry
        lax.fori_loop(j + 1, nb, above_diagonal, 0)
        dk_ref[...] = dk_s[...] * LN2
        dv_ref[...] = dv_s[...].astype(dv_ref.dtype)

        @pl.when(j == nb - 1)
        def _():
            dq = dq_s[...] * scale
            ga_ref[...] = (dq * cos_ref[...]).astype(ga_ref.dtype)
            gb_ref[...] = (dq * sin_ref[...]).astype(gb_ref.dtype)

        end(refs)

    full = pl.BlockSpec((S, SLOT), lambda b, h, j: (b, h))
    kv = pl.BlockSpec((blk, SLOT), lambda b, h, j: (b * nb + j, h))
    tab = pl.BlockSpec((S, SLOT), lambda b, h, j: (b, 0))
    stat = pl.BlockSpec((None, nb, SUB, blk), lambda b, h, j: (b * n_h + h, 0, 0, 0))
    return pl.pallas_call(
        body, name="attn_bwd", grid=grid,
        in_specs=[full, kv, kv, full, full, stat, tab, tab] + j_in,
        out_specs=[kv, kv, full, full] + j_out,
        out_shape=[jax.ShapeDtypeStruct((T, n_h * SLOT), F32)]
        + [jax.ShapeDtypeStruct((T, n_h * SLOT), BF16)] * 3 + j_shapes,
        scratch_shapes=[pltpu.VMEM((S, SLOT), F32), pltpu.VMEM((nb, SUB, blk), F32),
                        pltpu.VMEM((blk, SLOT), F32), pltpu.VMEM((blk, SLOT), F32)] + j_scratch,
        compiler_params=_params(("arbitrary", "arbitrary", "arbitrary")))(
            Q, K, V, O, dO, LSE, cosq, sinq, *j_args)


def _layer_norm_parts(u):
    xc = u - jnp.mean(u, axis=-1, keepdims=True)
    rstd = lax.rsqrt(jnp.mean(xc * xc, axis=-1, keepdims=True) + EPS)
    return xc * rstd, rstd


def _shift_scratch(tm, C):
    return pltpu.VMEM((SUB - 1, tm + HALO - SUB, C), F32)


def _preshift(ext, sh, tm):
    for r in range(1, SUB):
        sh[r - 1] = ext[pl.ds(r, tm + HALO - SUB), :]


def _shifted(ext, sh, off, base, n):
    q, r = divmod(off, SUB)
    src = ext if r == 0 else sh.at[r - 1]
    return src[pl.ds(base + SUB * q, n), :]


CONV_ROWS = 32


def conv_fwd(cu, w, b_dw, g_cn, b_cn, tm, S):
    T, C = cu.shape
    nex, hb = S // tm, tm // HALO

    def body(cur_ref, prev_ref, w_ref, b_ref, g_ref, bc_ref, u_ref, s_ref, ext, sh):
        first = pl.program_id(0) % nex == 0
        ext[pl.ds(0, HALO), :] = jnp.where(first, 0.0, prev_ref[...])
        ext[pl.ds(HALO, tm), :] = cur_ref[...]
        _preshift(ext, sh, tm)

        acc = jnp.zeros((tm, C), F32)
        for j in range(CONV_W):
            acc = acc + w_ref[pl.ds(j, 1), :] * _shifted(ext, sh, HALO - CONV_W + 1 + j, 0, tm)
        u = acc + b_ref[...]
        ln = _layer_norm_parts(u)[0] * g_ref[...] + bc_ref[...]
        u_ref[...] = u
        s_ref[...] = (ln * _sig(ln)).astype(s_ref.dtype)

    vec = pl.BlockSpec((1, C), lambda i: (0, 0))
    return pl.pallas_call(
        body, name="conv_fwd", grid=(T // tm,),
        in_specs=[pl.BlockSpec((tm, C), lambda i: (i, 0)),
                  pl.BlockSpec((HALO, C), lambda i: (jnp.maximum(i * hb - 1, 0), 0)),
                  pl.BlockSpec((HALO, C), lambda i: (0, 0)), vec, vec, vec],
        out_specs=[pl.BlockSpec((tm, C), lambda i: (i, 0))] * 2,
        out_shape=[jax.ShapeDtypeStruct((T, C), F32), jax.ShapeDtypeStruct((T, C), BF16)],
        scratch_shapes=[pltpu.VMEM((HALO + tm, C), F32), _shift_scratch(tm, C)],
        compiler_params=_params(("arbitrary",)))(cu, cu, w, b_dw, g_cn, b_cn)


def conv_bwd(du, cu, glu_a, glu_b, w, tm, S):
    T, C = du.shape
    nex, hb = S // tm, tm // HALO
    last_blk = T // HALO - 1

    def body(du_ref, nxt_ref, cu_ref, prev_ref, a_ref, b_ref, w_ref,
             da_ref, db_ref, dw_ref, dbias_ref, extd, extc, shd, shc):
        i = pl.program_id(0)
        first = i % nex == 0
        last = i % nex == nex - 1
        extd[pl.ds(0, tm), :] = du_ref[...]
        extd[pl.ds(tm, HALO), :] = jnp.where(last, 0.0, nxt_ref[...])
        extc[pl.ds(0, HALO), :] = jnp.where(first, 0.0, prev_ref[...])
        extc[pl.ds(HALO, tm), :] = cu_ref[...]
        _preshift(extd, shd, tm)
        _preshift(extc, shc, tm)

        @pl.when(i == 0)
        def _():
            dw_ref[...] = jnp.zeros(dw_ref.shape, F32)
            dbias_ref[...] = jnp.zeros(dbias_ref.shape, F32)

        def rows_step(ci, carry):
            base = pl.multiple_of(ci * CONV_ROWS, CONV_ROWS)
            rows = pl.ds(base, CONV_ROWS)
            du_rows = du_ref[rows, :]
            dcu = jnp.zeros((CONV_ROWS, C), F32)
            for j in range(CONV_W):
                dcu = dcu + w_ref[pl.ds(j, 1), :] * _shifted(extd, shd, CONV_W - 1 - j,
                                                             base, CONV_ROWS)
                dw_ref[pl.ds(SUB * j, SUB), :] += _sum8(
                    du_rows * _shifted(extc, shc, HALO - CONV_W + 1 + j, base, CONV_ROWS))
            dbias_ref[...] += _sum8(du_rows)
            sb = _sig(b_ref[rows, :])
            da_ref[rows, :] = (dcu * sb).astype(da_ref.dtype)
            db_ref[rows, :] = (dcu * a_ref[rows, :] * sb * (1.0 - sb)).astype(db_ref.dtype)
            return carry
        lax.fori_loop(0, tm // CONV_ROWS, rows_step, 0)

    cur = pl.BlockSpec((tm, C), lambda i: (i, 0))
    return pl.pallas_call(
        body, name="conv_bwd", grid=(T // tm,),
        in_specs=[cur, pl.BlockSpec((HALO, C), lambda i: (jnp.minimum((i + 1) * hb, last_blk), 0)),
                  cur, pl.BlockSpec((HALO, C), lambda i: (jnp.maximum(i * hb - 1, 0), 0)),
                  cur, cur, pl.BlockSpec((HALO, C), lambda i: (0, 0))],
        out_specs=[cur, cur, pl.BlockSpec((HALO * SUB, C), lambda i: (0, 0)),
                   pl.BlockSpec((SUB, C), lambda i: (0, 0))],
        out_shape=[jax.ShapeDtypeStruct((T, C), BF16), jax.ShapeDtypeStruct((T, C), BF16),
                   jax.ShapeDtypeStruct((HALO * SUB, C), F32), jax.ShapeDtypeStruct((SUB, C), F32)],
        scratch_shapes=[pltpu.VMEM((tm + HALO, C), F32), pltpu.VMEM((HALO + tm, C), F32),
                        _shift_scratch(tm, C), _shift_scratch(tm, C)],
        compiler_params=_params(("arbitrary",)))(du, du, cu, cu, glu_a, glu_b, w)


def silu_small(c_all):
    def body(c_ref, o_ref):
        v = c_ref[...]
        o_ref[...] = v * _sig(v)
    return pl.pallas_call(body, name="silu_c", out_shape=jax.ShapeDtypeStruct(c_all.shape, F32))(c_all)


def ada_fwd(c_act, w_ada, b_ada):
    n_l, D, ns = w_ada.shape
    B = c_act.shape[0]
    tn = _tile(ns, 512, LANES)

    def body(c_ref, w_ref, b_ref, o_ref):
        o_ref[...] = jnp.dot(c_ref[...], w_ref[...], preferred_element_type=F32,
                             precision=lax.Precision.HIGHEST) + b_ref[...]

    return pl.pallas_call(
        body, name="ada_fwd", grid=(n_l, ns // tn),
        in_specs=[pl.BlockSpec((B, D), lambda l, j: (0, 0)),
                  pl.BlockSpec((None, D, tn), lambda l, j: (l, 0, j)),
                  pl.BlockSpec((None, 1, tn), lambda l, j: (l, 0, j))],
        out_specs=pl.BlockSpec((None, B, tn), lambda l, j: (l, 0, j)),
        out_shape=jax.ShapeDtypeStruct((n_l, B, ns), F32),
        compiler_params=_params(("parallel", "parallel")))(c_act, w_ada, b_ada.reshape(n_l, 1, ns))


def ada_bwd_w(c_act_t, dmod):
    D, B = c_act_t.shape
    n_l, _, ns = dmod.shape
    tn = _tile(ns, 512, LANES)

    def body(c_ref, d_ref, o_ref):
        o_ref[...] = jnp.dot(c_ref[...], d_ref[...], preferred_element_type=F32,
                             precision=lax.Precision.HIGHEST)

    return pl.pallas_call(
        body, name="ada_bwd_w", grid=(n_l, ns // tn),
        in_specs=[pl.BlockSpec((D, B), lambda l, j: (0, 0)),
                  pl.BlockSpec((None, B, tn), lambda l, j: (l, 0, j))],
        out_specs=pl.BlockSpec((None, D, tn), lambda l, j: (l, 0, j)),
        out_shape=jax.ShapeDtypeStruct((n_l, D, ns), F32),
        compiler_params=_params(("parallel", "parallel")))(c_act_t, dmod)


def group_sum(name, v, group):
    rows, W = v.shape
    n = rows // group

    def body(v_ref, o_ref):
        o_ref[...] = jnp.sum(v_ref[...].reshape(n, group, W), axis=1)

    return pl.pallas_call(body, name=name, out_shape=jax.ShapeDtypeStruct((n, W), F32),
                          compiler_params=_params(None))(v)


def lane_total(name, v):
    def body(v_ref, o_ref):
        o_ref[...] = jnp.broadcast_to(jnp.sum(v_ref[...], axis=1, keepdims=True), o_ref.shape)
    return pl.pallas_call(body, name=name, out_shape=jax.ShapeDtypeStruct((1, LANES), F32))(v)


def adamw(name, w, g, m, v):
    rows, cols = w.shape
    tr = _tile(rows, max(SUB, (1 << 19) // cols // SUB * SUB), SUB)

    def body(w_ref, g_ref, m_ref, v_ref, d_ref, nm_ref, nv_ref):
        gg = g_ref[...]
        nm = ADAM_B1 * m_ref[...] + (1.0 - ADAM_B1) * gg
        nv = ADAM_B2 * v_ref[...] + (1.0 - ADAM_B2) * (gg * gg)
        m_hat = nm / (1.0 - ADAM_B1 ** ADAM_STEP)
        v_hat = nv / (1.0 - ADAM_B2 ** ADAM_STEP)
        d_ref[...] = -ADAM_LR * (m_hat / (jnp.sqrt(v_hat) + ADAM_EPS) + ADAM_WD * w_ref[...])
        nm_ref[...] = nm
        nv_ref[...] = nv

    spec = pl.BlockSpec((tr, cols), lambda i: (i, 0))
    return pl.pallas_call(
        body, name=name, grid=(rows // tr,), in_specs=[spec] * 4, out_specs=[spec] * 3,
        out_shape=[jax.ShapeDtypeStruct((rows, cols), F32)] * 3,
        compiler_params=_params(("parallel",)))(w, g, m, v)


def rope_tables(pos_col, invf, one_nope, rope_mask, tm):
    T = pos_col.shape[0]

    def body(p_ref, f_ref, o_ref, r_ref, cq_ref, sq_ref, ck_ref):
        ang = p_ref[...] * f_ref[...]
        cs = jnp.cos(ang) * r_ref[...]
        cq_ref[...] = o_ref[...] + cs
        sq_ref[...] = jnp.sin(ang) * r_ref[...]
        ck_ref[...] = cs

    vec = pl.BlockSpec((1, LANES), lambda i: (0, 0))
    out = pl.BlockSpec((tm, LANES), lambda i: (i, 0))
    return pl.pallas_call(
        body, name="rope_tables", grid=(T // tm,),
        in_specs=[pl.BlockSpec((tm, 1), lambda i: (i, 0)), vec, vec, vec], out_specs=[out] * 3,
        out_shape=[jax.ShapeDtypeStruct((T, LANES), F32)] * 3,
        compiler_params=_params(("parallel",)))(pos_col, invf, one_nope, rope_mask)


def _place():
    return lax.axis_index("x"), lax.axis_index("y"), lax.axis_index("c")


def allgather8(name, v):
    R, W = v.shape

    def body(x_ref, out_ref, send_sems, recv_sems, local_sem):
        x, y, c = _place()
        me = 4 * x + 2 * y + c
        mine = pltpu.make_async_copy(x_ref, out_ref.at[me], local_sem)
        mine.start()
        sends, peers = [], []
        for k in range(1, N_DEV):
            px, py, pc = x ^ ((k >> 2) & 1), y ^ ((k >> 1) & 1), c ^ (k & 1)
            peers.append((px, py, pc))
            cp = pltpu.make_async_remote_copy(
                src_ref=x_ref, dst_ref=out_ref.at[me], send_sem=send_sems.at[k - 1],
                recv_sem=recv_sems.at[k - 1], device_id=(px, py, pc), device_id_type=MESH)
            cp.start()
            sends.append(cp)
        for k, (px, py, pc) in enumerate(peers):
            pltpu.make_async_remote_copy(
                src_ref=x_ref, dst_ref=out_ref.at[4 * px + 2 * py + pc], send_sem=send_sems.at[k],
                recv_sem=recv_sems.at[k], device_id=(px, py, pc), device_id_type=MESH).wait_recv()
        for cp in sends:
            cp.wait_send()
        mine.wait()

    return pl.pallas_call(
        body, name=name, out_shape=jax.ShapeDtypeStruct((N_DEV, R, W), v.dtype),
        in_specs=[pl.BlockSpec(memory_space=pltpu.VMEM)],
        out_specs=pl.BlockSpec(memory_space=pltpu.VMEM),
        scratch_shapes=[pltpu.SemaphoreType.DMA((N_DEV - 1,)), pltpu.SemaphoreType.DMA((N_DEV - 1,)),
                        pltpu.SemaphoreType.DMA])(v)


def _other_chips(x, y):
    return [(1 - x, y), (x, 1 - y), (1 - x, 1 - y)]


def _hbm_exchange(name, body, ins, out_shapes, n_sems):
    n_in, n_out = len(ins), len(out_shapes)

    def wrapped(*refs):
        send_sems, recv_sems = refs[n_in + n_out:]

        def copy(k, src, dst, to):
            return pltpu.make_async_remote_copy(src_ref=src, dst_ref=dst, send_sem=send_sems.at[k],
                                                recv_sem=recv_sems.at[k], device_id=to,
                                                device_id_type=MESH)
        body(refs[:n_in], refs[n_in:n_in + n_out], copy)

    hbm = pl.BlockSpec(memory_space=pl.ANY)
    return pl.pallas_call(
        wrapped, name=name, out_shape=out_shapes, in_specs=[hbm] * n_in, out_specs=[hbm] * n_out,
        scratch_shapes=[pltpu.SemaphoreType.DMA((n_sems,)), pltpu.SemaphoreType.DMA((n_sems,))])(*ins)


def _gather_start(w_refs, o_refs, copy):
    x, y, c = _place()
    for t, (w, o) in enumerate(zip(w_refs, o_refs)):
        for j, (cx, cy) in enumerate(_other_chips(x, y)):
            copy(6 * t + j, w.at[c], o.at[2 * x + y, c], (cx, cy, c)).start()


def _gather_finish(w_refs, o_refs, copy):
    x, y, c = _place()
    chips = _other_chips(x, y)
    for t, (w, o) in enumerate(zip(w_refs, o_refs)):
        for j, (cx, cy) in enumerate(chips):
            landed = o.at[2 * cx + cy, c]
            copy(6 * t + j, w.at[c], landed, (cx, cy, c)).wait_recv()
            copy(6 * t + 3 + j, landed, landed, (x, y, 1 - c)).start()
    for t, (w, o) in enumerate(zip(w_refs, o_refs)):
        for j, (cx, cy) in enumerate(chips):
            copy(6 * t + 3 + j, w.at[c], o.at[2 * cx + cy, 1 - c], (x, y, 1 - c)).wait_recv()
    for t, (w, o) in enumerate(zip(w_refs, o_refs)):
        for j, (cx, cy) in enumerate(chips):
            copy(6 * t + j, w.at[c], o.at[2 * x + y, c], (cx, cy, c)).wait_send()
            landed = o.at[2 * cx + cy, c]
            copy(6 * t + 3 + j, landed, landed, (x, y, 1 - c)).wait_send()


def weight_gather_job(ws):
    return (ws, [jax.ShapeDtypeStruct((N_CHIPS,) + w.shape, w.dtype) for w in ws], 6 * len(ws),
            _gather_start, _gather_finish)


def _exchange_start(p_refs, b_refs, copy):
    x, y, c = _place()
    for t, (p, b) in enumerate(zip(p_refs, b_refs)):
        for j, (cx, cy) in enumerate(_other_chips(x, y)):
            copy(3 * t + j, p.at[2 * cx + cy], b.at[j], (cx, cy, c)).start()


def _exchange_finish(p_refs, b_refs, copy):
    x, y, c = _place()
    for t, (p, b) in enumerate(zip(p_refs, b_refs)):
        for j, (cx, cy) in enumerate(_other_chips(x, y)):
            copy(3 * t + j, p.at[2 * cx + cy], b.at[j], (cx, cy, c)).wait()


def chip_exchange_job(ps):
    return (ps, [jax.ShapeDtypeStruct((3,) + p.shape[1:], p.dtype) for p in ps], 3 * len(ps),
            _exchange_start, _exchange_finish)


def run_job(name, job):
    ins, outs, n_sems, start, finish = job

    def body(in_refs, out_refs, copy):
        start(in_refs, out_refs, copy)
        finish(in_refs, out_refs, copy)
    return _hbm_exchange(name, body, ins, outs, n_sems)


def sibling_swap(gs):
    n = len(gs)

    def body(g_refs, a_refs, copy):
        x, y, c = _place()
        cps = []
        for t in range(n):
            cp = copy(t, g_refs[t].at[pl.ds(0, N_CHIPS), 1 - c], a_refs[t], (x, y, 1 - c))
            cp.start()
            cps.append(cp)
        for cp in cps:
            cp.wait()

    return _hbm_exchange("grad_sibling_swap", body, gs,
                         [jax.ShapeDtypeStruct(g.shape[:1] + g.shape[2:], g.dtype) for g in gs], n)


def chip_partial(name, g, a, c_idx):
    n_s, _, kh, ns = g.shape
    tr = _tile(kh, max(16, (1 << 19) // ns // 16 * 16), 16)

    def body(c_ref, g_ref, a_ref, o_ref):
        o_ref[...] = (g_ref[...] + a_ref[...]).astype(o_ref.dtype)

    return pl.pallas_call(
        body, name=name,
        grid_spec=pltpu.PrefetchScalarGridSpec(
            num_scalar_prefetch=1, grid=(n_s, kh // tr),
            in_specs=[pl.BlockSpec((None, None, tr, ns), lambda i, r, cr: (i, cr[0], r, 0)),
                      pl.BlockSpec((None, tr, ns), lambda i, r, cr: (i, r, 0))],
            out_specs=pl.BlockSpec((None, tr, ns), lambda i, r, cr: (i, r, 0))),
        out_shape=jax.ShapeDtypeStruct((n_s, kh, ns), BF16),
        compiler_params=_params(("parallel", "parallel")))(c_idx, g, a)


def shard_total(name, p, b, s_idx):
    _, rows, ns = p.shape
    tr = _tile(rows, max(16, (1 << 19) // ns // 16 * 16), 16)

    def body(s_ref, p_ref, b0, b1, b2, o_ref):
        o_ref[...] = ((p_ref[...].astype(F32) + b0[...].astype(F32)) + b1[...].astype(F32)
                      ) + b2[...].astype(F32)

    def bspec(j):
        return pl.BlockSpec((None, tr, ns), lambda r, sr: (j, r, 0))

    return pl.pallas_call(
        body, name=name,
        grid_spec=pltpu.PrefetchScalarGridSpec(
            num_scalar_prefetch=1, grid=(rows // tr,),
            in_specs=[pl.BlockSpec((None, tr, ns), lambda r, sr: (sr[0], r, 0)),
                      bspec(0), bspec(1), bspec(2)],
            out_specs=pl.BlockSpec((tr, ns), lambda r, sr: (r, 0))),
        out_shape=jax.ShapeDtypeStruct((rows, ns), F32),
        compiler_params=_params(("parallel",)))(s_idx, p, b, b, b)


def sibling_share(fs):
    n = len(fs)

    def body(f_refs, o_refs, copy):
        x, y, c = _place()
        cps = []
        for t in range(n):
            cp = copy(t, f_refs[t], o_refs[t], (x, y, 1 - c))
            cp.start()
            cps.append(cp)
        for cp in cps:
            cp.wait()

    return _hbm_exchange("grad_sibling_share", body, fs,
                         [jax.ShapeDtypeStruct(f.shape, f.dtype) for f in fs], n)


def _rot_cols(w):
    h = w.shape[-1] // 2
    return jnp.concatenate([-w[..., h:], w[..., :h]], axis=-1)


def _slots(parts, lead, n_h):
    width = sum(p.shape[-1] for p in parts)
    pad = jnp.zeros(lead + (n_h, SLOT - width), parts[0].dtype)
    return jnp.concatenate(parts + [pad], axis=-1).reshape(lead + (n_h * SLOT,))


def layout_weights(w_in, w_uq, w_ukv, w_o_attn, dims):
    QL, KVL, C, D, n_h = dims
    o = 0
    w_ql, o = w_in[..., o:o + QL], o + QL
    w_kvl, o = w_in[..., o:o + KVL], o + KVL
    w_kr, o = w_in[..., o:o + ROPE], o + ROPE
    w_glu_a, o = w_in[..., o:o + C], o + C
    w_glu_b, o = w_in[..., o:o + C], o + C
    w_ga, o = w_in[..., o:o + D], o + D
    w_gb = w_in[..., o:o + D]
    z = lambda n: jnp.zeros(w_kr.shape[:-1] + (n,), w_kr.dtype)
    kr_a = jnp.concatenate([z(NOPE), w_kr, z(SLOT - QK_DIM)], axis=-1)
    kr_b = jnp.concatenate([z(NOPE), _rot_cols(w_kr), z(SLOT - QK_DIM)], axis=-1)
    lead = w_uq.shape[:-1]
    q = w_uq.reshape(lead + (n_h, QK_DIM))
    zq = jnp.zeros(lead + (n_h, NOPE), w_uq.dtype)
    wq_a = _slots([q[..., :NOPE], q[..., NOPE:]], lead, n_h)
    wq_b = _slots([zq, _rot_cols(q[..., NOPE:])], lead, n_h)
    lead = w_ukv.shape[:-1]
    kv = w_ukv.reshape(lead + (n_h, NOPE + VDIM))
    w_kn = _slots([kv[..., :NOPE]], lead, n_h)
    w_v = _slots([kv[..., NOPE:]], lead, n_h)
    lead = w_o_attn.shape[:-2]
    wo = w_o_attn.reshape(lead + (n_h, VDIM, D))
    wo = jnp.concatenate([wo, jnp.zeros(lead + (n_h, SLOT - VDIM, D), wo.dtype)], axis=-2)
    w_o = wo.reshape(lead + (n_h * SLOT, D))
    return dict(ql=w_ql, kvl=w_kvl, kr_a=kr_a, kr_b=kr_b, glu_a=w_glu_a, glu_b=w_glu_b,
                ga=w_ga, gb=w_gb, q_a=wq_a, q_b=wq_b, kn=w_kn, v=w_v, o=w_o)


GATHERED = ("w_in", "w_uq", "w_ukv", "w_o_attn", "w_pw2", "w_out", "w_gu", "w_down")
ROW_SHARDED = ("w_out", "w_down")
N_BEFORE_ATTN = 6
GRADS_BEFORE_ATTN = (4, 5, 6, 7)


def halves(a):
    return a.reshape(a.shape[:-2] + (2, a.shape[-2] // 2, a.shape[-1]))


def full_layer(name, gathered):
    n_c, _, kh, ns = gathered.shape
    if name in ROW_SHARDED:
        return gathered.reshape(n_c * 2 * kh, ns)
    return jnp.concatenate([gathered[s].reshape(2 * kh, ns) for s in range(n_c)], axis=1)


def owed_pieces(name, g):
    if name in ROW_SHARDED:
        return halves(g.reshape(N_CHIPS, g.shape[0] // N_CHIPS, g.shape[1]))
    k, n = g.shape
    return halves(jnp.transpose(g.reshape(k, N_CHIPS, n // N_CHIPS), (1, 0, 2)))


def _pad_rows8(flat):
    n = flat.shape[0]
    w = -(-n // (SUB * LANES)) * LANES
    return jnp.concatenate([flat, jnp.zeros((SUB * w - n,), flat.dtype)]).reshape(SUB, w)


def kernel(x, c, positions, w_ada, b_ada, g_mix, w_in, g_q, w_uq, g_kv, w_ukv, w_o_attn, w_dw, b_dw, g_cn, b_cn, w_pw2, w_out, g_ffn, w_gu, w_down, g_final, loss_target, m_w_ada, m_b_ada, m_g_mix, m_w_in, m_g_q, m_w_uq, m_g_kv, m_w_ukv, m_w_o_attn, m_w_dw, m_b_dw, m_g_cn, m_b_cn, m_w_pw2, m_w_out, m_g_ffn, m_w_gu, m_w_down, m_g_final, v_w_ada, v_b_ada, v_g_mix, v_w_in, v_g_q, v_w_uq, v_g_kv, v_w_ukv, v_w_o_attn, v_w_dw, v_b_dw, v_g_cn, v_b_cn, v_w_pw2, v_w_out, v_g_ffn, v_w_gu, v_w_down, v_g_final):
    weights = dict(w_ada=w_ada, b_ada=b_ada, g_mix=g_mix, w_in=w_in, g_q=g_q, w_uq=w_uq, g_kv=g_kv,
                   w_ukv=w_ukv, w_o_attn=w_o_attn, w_dw=w_dw, b_dw=b_dw, g_cn=g_cn, b_cn=b_cn,
                   w_pw2=w_pw2, w_out=w_out, g_ffn=g_ffn, w_gu=w_gu, w_down=w_down, g_final=g_final)
    mom = dict(w_ada=m_w_ada, b_ada=m_b_ada, g_mix=m_g_mix, w_in=m_w_in, g_q=m_g_q, w_uq=m_w_uq,
               g_kv=m_g_kv, w_ukv=m_w_ukv, w_o_attn=m_w_o_attn, w_dw=m_w_dw, b_dw=m_b_dw,
               g_cn=m_g_cn, b_cn=m_b_cn, w_pw2=m_w_pw2, w_out=m_w_out, g_ffn=m_g_ffn, w_gu=m_w_gu,
               w_down=m_w_down, g_final=m_g_final)
    var = dict(w_ada=v_w_ada, b_ada=v_b_ada, g_mix=v_g_mix, w_in=v_w_in, g_q=v_g_q, w_uq=v_w_uq,
               g_kv=v_g_kv, w_ukv=v_w_ukv, w_o_attn=v_w_o_attn, w_dw=v_w_dw, b_dw=v_b_dw,
               g_cn=v_g_cn, b_cn=v_b_cn, w_pw2=v_w_pw2, w_out=v_w_out, g_ffn=v_g_ffn, w_gu=v_w_gu,
               w_down=v_w_down, g_final=v_g_final)
    order = list(weights)

    n_e, S, D = x.shape
    T = n_e * S
    n_l = w_in.shape[0]
    QL, KVL, C = g_q.shape[1], g_kv.shape[1], g_cn.shape[1]
    n_h = w_uq.shape[2] * N_CHIPS // QK_DIM
    F = w_gu.shape[2] * N_CHIPS // 2
    B = n_e * N_DEV
    dims = (QL, KVL, C, D, n_h)
    scale = QK_DIM ** -0.5
    tm = _tile(S, 512, HALO)
    tm_mm = _tile(S, 1024, HALO)
    blk = _tile(S, ATTN_LB, LANES)
    tq = _tile(S, ATTN_TQ, blk)
    q_scale = scale / LN2
    mx, my, mc = _place()
    dev = 4 * mx + 2 * my + mc
    chip = 2 * mx + my
    c_idx = jnp.reshape(mc, (1,)).astype(jnp.int32)
    s_idx = jnp.reshape(chip, (1,)).astype(jnp.int32)

    xt = x.reshape(T, D)
    tgt = loss_target.reshape(T, D)

    mine = [[halves(weights[k][l].astype(BF16)) for k in GATHERED] for l in range(n_l)]
    zero = jnp.zeros((), jnp.int32)
    full = {k: [None] * n_l for k in GATHERED}
    lay = [None] * n_l

    def take_weights(l, which, got):
        for t, g in zip(which, got):
            k, w = GATHERED[t], mine[l][t]
            g = lax.dynamic_update_slice(g, w[None], (chip.astype(jnp.int32), zero, zero, zero))
            full[k][l] = full_layer(k, g)
        if lay[l] is None:
            lay[l] = layout_weights(full["w_in"][l], full["w_uq"][l], full["w_ukv"][l],
                                    full["w_o_attn"][l], dims)

    everything = list(range(len(GATHERED)))
    early, late = everything[:N_BEFORE_ATTN], everything[N_BEFORE_ATTN:]
    take_weights(0, early, run_job("weight_allgather",
                                   weight_gather_job([mine[0][t] for t in early])))

    ns_ada = w_ada.shape[2]
    n_c, n_wdw = n_e * D, int(np.prod(w_dw.shape))
    small_all = allgather8("gather_c_wdw", _pad_rows8(
        jnp.concatenate([c.reshape(-1), w_dw.reshape(-1)]))).reshape(N_DEV, -1)
    c_all = small_all[:, :n_c].reshape(B, D)
    wdw_parts = small_all[:, n_c:n_c + n_wdw].reshape((N_DEV,) + w_dw.shape)
    w_dw_full = jnp.concatenate([wdw_parts[2 * s] for s in range(N_CHIPS)], axis=2)
    w_dw_pad = jnp.concatenate([w_dw_full, jnp.zeros((n_l, HALO - CONV_W, C), F32)], axis=1)

    c_act = silu_small(c_all)
    b_ada_mine = lax.dynamic_slice_in_dim(b_ada, chip * ns_ada, ns_ada, axis=1)
    mod_part = ada_fwd(c_act, w_ada, b_ada_mine)
    mod_all = allgather8("gather_mod", mod_part.reshape(n_l * B, ns_ada)).reshape(
        N_DEV, n_l, B, ns_ada)
    mod_full = jnp.concatenate([mod_all[2 * s] for s in range(N_CHIPS)], axis=2)
    mod_mine = lax.dynamic_slice_in_dim(mod_full, dev * n_e, n_e, axis=1)
    mods = mod_mine.reshape(n_l, n_e, N_MOD, 1, D)


    lane = np.arange(LANES)
    in_rope = (lane >= NOPE) & (lane < QK_DIM)
    inv_freq = ROPE_THETA ** (-np.arange(0, ROPE, 2, dtype=np.float32) / ROPE)
    invf = np.where(in_rope, inv_freq[(lane - NOPE) % (ROPE // 2)], 0.0).astype(np.float32)
    cos_q, sin_q, cos_k = rope_tables(
        positions.astype(F32).reshape(T, 1), jnp.asarray(invf).reshape(1, LANES),
        jnp.asarray((lane < NOPE).astype(np.float32)).reshape(1, LANES),
        jnp.asarray(in_rope.astype(np.float32)).reshape(1, LANES), tm)

    def rope_epi(accs, ex):
        n = accs[0].shape[1]
        return (accs[0] * _lanes(ex[0], n) + accs[1] * _lanes(ex[1], n),)

    def rms_epi(accs, ex):
        a = accs[0]
        return a, a * _rstd(a) * ex[0]

    def modnorm(name, xin, g, sc, sh):
        def fn(xv, gv, scv, shv):
            return (xv * _rstd(xv) * gv * (1.0 + scv) + shv,)
        return rowwise(name, [('tile', xin), ('vec', g), ('exvec', sc), ('exvec', sh)],
                       [('tile', D, BF16)], fn, tm, S)[0]

    def modnorm_bwd(name, dh, xin, dres, g, sc):
        def fn(dhv, xv, drv, gv, scv):
            rstd = _rstd(xv)
            xhat = xv * rstd
            dx = _norm_bwd(dhv * gv * (1.0 + scv), xhat, rstd) + drv
            return dx, dhv, dhv * xhat * gv, dhv * xhat * (1.0 + scv)
        return rowwise(name, [('tile', dh), ('tile', xin), ('tile', dres), ('vec', g), ('exvec', sc)],
                       [('tile', D, F32), ('exacc', D), ('exacc', D), ('acc', D)], fn, tm, S)

    def rms_bwd(name, dy, xin, g, width):
        def fn(dyv, xv, gv):
            rstd = _rstd(xv)
            xhat = xv * rstd
            return _norm_bwd(dyv * gv, xhat, rstd), dyv * xhat
        return rowwise(name, [('tile', dy), ('tile', xin), ('vec', g)],
                       [('tile', width, BF16), ('acc', width)], fn, tm, S)

    def gate_bwd(name, dxo, branch, gt):
        def fn(dv, bv, gv):
            return dv * gv, dv * bv
        return rowwise(name, [('tile', dxo), ('tile', branch), ('exvec', gt)],
                       [('tile', D, BF16), ('exacc', D)], fn, tm, S)

    tn_d = _tile(D, 512, LANES)
    tn_f = _tile(F, 1536, LANES)
    tn_s = _tile(n_h * SLOT, 512, LANES)
    row = lambda a: a.reshape(1, -1)

    saved = []
    xc = xt
    for l in range(n_l):
        W = lay[l]
        sh1, sc1, gt1, sh2, sc2, gt2 = [mods[l, :, k] for k in range(N_MOD)]
        h1 = modnorm("modnorm_mix", xc, row(g_mix[l]), sc1, sh1)
        ga, gb = mm_fused("proj_gates", [h1], [(0, W["ga"]), (0, W["gb"])], [], lambda a, e: a,
                          [F32, F32], tm_mm, tn_d, S)
        cu, glu_a, glu_b = mm_fused(
            "proj_glu", [h1], [(0, W["glu_a"]), (0, W["glu_b"])], [],
            lambda a, e: (a[0] * _sig(a[1]), a[0], a[1]), [F32, F32, F32], tm, C, S)
        q_lat, qn = mm_fused("proj_q_lat", [h1], [(0, W["ql"])], [('vec', row(g_q[l]))], rms_epi,
                             [F32, BF16], tm, QL, S)
        kv_lat, kvn = mm_fused("proj_kv_lat", [h1], [(0, W["kvl"])], [('vec', row(g_kv[l]))],
                               rms_epi, [F32, BF16], tm, KVL, S)
        kr = mm_fused("proj_k_rope", [h1], [(0, W["kr_a"]), (0, W["kr_b"])],
                      [('row128', cos_k), ('row128', sin_q)], rope_epi, [F32], tm, SLOT, S)[0]
        q_all = mm_fused("q_up", [qn], [(0, W["q_a"]), (0, W["q_b"])],
                         [('row128', cos_q), ('row128', sin_q)],
                         lambda a, e: (rope_epi(a, e)[0] * q_scale,), [BF16], tm, tn_s, S)[0]
        k_all, v_all = mm_fused(
            "kv_up", [kvn], [(0, W["kn"]), (0, W["v"])], [('row128', kr)],
            lambda a, e: (a[0] + _lanes(e[0], a[0].shape[1]), a[1]), [BF16, BF16], tm, tn_s, S)
        riders = [(0, t) for t in late] if l == 0 else []
        riders += [(l + 1, t) for t in everything] if l + 1 < n_l else []
        o_all, lse, *got = attn_fwd(
            q_all, k_all, v_all, n_e, S, n_h, tq, blk,
            job=weight_gather_job([mine[ll][t] for ll, t in riders]) if riders else None)
        for ll in sorted({ll for ll, _ in riders}):
            take_weights(ll, [t for l2, t in riders if l2 == ll],
                         [g for (l2, _), g in zip(riders, got) if l2 == ll])
        u, s_act = conv_fwd(cu, w_dw_pad[l], row(b_dw[l]), row(g_cn[l]), row(b_cn[l]), tm, S)

        def merge_epi(a, e):
            return _sig(e[0]) * a[0] + _sig(e[1]) * a[1], a[0], a[1]
        y, ya, yc = mm_fused("merge", [o_all, s_act], [(0, W["o"]), (1, full["w_pw2"][l])],
                             [('tile', ga), ('tile', gb)], merge_epi, [BF16, F32, F32], tm_mm, tn_d, S)
        x2, o_mix = mm_fused("mix_out", [y], [(0, full["w_out"][l])], [('tile', xc), ('exvec', gt1)],
                             lambda a, e: (e[0] + e[1] * a[0], a[0]), [F32, F32], tm_mm, tn_d, S)
        h2 = modnorm("modnorm_ffn", x2, row(g_ffn[l]), sc2, sh2)

        def swiglu_epi(a, e):
            return a[0], a[1], a[0] * _sig(a[0]) * a[1]
        w_gu_l = full["w_gu"][l]
        g_act, up, act = mm_fused("ffn_up", [h2], [(0, w_gu_l), (0, w_gu_l, False, F // tn_f)], [],
                                  swiglu_epi, [BF16, BF16, BF16], tm_mm, tn_f, S, N=F)
        x3, dn = mm_fused("ffn_down", [act], [(0, full["w_down"][l])], [('tile', x2), ('exvec', gt2)],
                          lambda a, e: (e[0] + e[1] * a[0], a[0]), [F32, F32], tm_mm, tn_d, S)
        saved.append(dict(x=xc, h1=h1, ga=ga, gb=gb, cu=cu, glu_a=glu_a, glu_b=glu_b, q_lat=q_lat,
                          qn=qn, kv_lat=kv_lat, kvn=kvn, q_all=q_all, k_all=k_all, v_all=v_all,
                          o_all=o_all, lse=lse, u=u, s_act=s_act, y=y, ya=ya, yc=yc, x2=x2,
                          o_mix=o_mix, h2=h2, g_act=g_act, up=up, act=act, dn=dn))
        xc = x3

    def loss_fn(xv, tv, gv):
        rstd = _rstd(xv)
        xhat = xv * rstd
        err = xhat * gv - tv
        dy = err * (1.0 / D)
        return _norm_bwd(dy * gv, xhat, rstd), err * err * (0.5 / D), dy * xhat
    dxc, loss_acc, dg_final = rowwise("loss_head", [('tile', xc), ('tile', tgt), ('vec', row(g_final))],
                                      [('tile', D, F32), ('acc', D), ('acc', D)], loss_fn, tm, S)

    gfull = {k: [None] * n_l for k in GATHERED}
    g_wdw = [None] * n_l
    small_acc = {k: [None] * n_l for k in ("g_mix", "g_q", "g_kv", "b_dw", "g_cn", "b_cn", "g_ffn")}
    dmod_acc = [None] * n_l
    parts = [[None] * len(GATHERED) for _ in range(n_l)]
    received = [[None] * len(GATHERED) for _ in range(n_l)]

    def partial_sums(l, which):
        owed = [owed_pieces(GATHERED[t], gfull[GATHERED[t]][l]) for t in which]
        return [chip_partial("grad_chip_partial_" + GATHERED[t], g, a, c_idx)
                for t, g, a in zip(which, owed, sibling_swap(owed))]

    lay_T = jax.linear_transpose(
        lambda a, b, cc, d: layout_weights(a, b, cc, d, dims),
        *[jax.ShapeDtypeStruct(full[k][0].shape, F32) for k in ("w_in", "w_uq", "w_ukv", "w_o_attn")])

    for l in reversed(range(n_l)):
        sv = saved[l]
        W = lay[l]
        sh1, sc1, gt1, sh2, sc2, gt2 = [mods[l, :, k] for k in range(N_MOD)]
        w_gu_l = full["w_gu"][l]
        ddn, dgt2 = gate_bwd("ffn_gate_bwd", dxc, sv["dn"], gt2)

        def swiglu_bwd_epi(a, e):
            gv, uv = e[0].astype(F32), e[1].astype(F32)
            sg = _sig(gv)
            return a[0] * uv * sg * (1.0 + gv * (1.0 - sg)), a[0] * gv * sg
        dg_act, dup = mm_fused("ffn_down_bwd", [ddn], [(0, full["w_down"][l], True)],
                               [('tile', sv["g_act"]), ('tile', sv["up"])], swiglu_bwd_epi,
                               [BF16, BF16], tm_mm, tn_f, S)
        gfull["w_down"][l] = mm_tn("ffn_down_dw", sv["act"], ddn)
        gfull["w_gu"][l] = jnp.concatenate([mm_tn("ffn_gate_dw", sv["h2"], dg_act),
                                            mm_tn("ffn_up_dw", sv["h2"], dup)], axis=1)
        dh2 = mm_fused("ffn_up_bwd", [dg_act, dup], [(0, w_gu_l, True, 0), (1, w_gu_l, True, 1)], [],
                       lambda a, e: (a[0] + a[1],), [F32], tm_mm, tn_d, S)[0]
        dx2, dsh2, dsc2, dg_ffn = modnorm_bwd("modnorm_ffn_bwd", dh2, sv["x2"], dxc,
                                              row(g_ffn[l]), sc2)
        ddo, dgt1 = gate_bwd("mix_gate_bwd", dx2, sv["o_mix"], gt1)

        def merge_bwd_epi(a, e):
            sa, sb = _sig(e[0]), _sig(e[1])
            dy = a[0]
            return dy * sa, dy * sb, dy * e[2] * sa * (1.0 - sa), dy * e[3] * sb * (1.0 - sb)
        dya, dyc, dga, dgb = mm_fused(
            "mix_out_bwd", [ddo], [(0, full["w_out"][l], True)],
            [('tile', sv["ga"]), ('tile', sv["gb"]), ('tile', sv["ya"]), ('tile', sv["yc"])],
            merge_bwd_epi, [BF16] * 4, tm_mm, tn_d, S)
        gfull["w_out"][l] = mm_tn("mix_out_dw", sv["y"], ddo)
        do_all = mm_fused("attn_out_bwd", [dya], [(0, W["o"], True)], [], lambda a, e: a, [BF16],
                          tm, tn_s, S)[0]
        d_wo = mm_tn("attn_out_dw", sv["o_all"], dya)
        ds_act = mm_fused("conv_out_bwd", [dyc], [(0, full["w_pw2"][l], True)], [], lambda a, e: a,
                          [F32], tm, C, S)[0]
        gfull["w_pw2"][l] = mm_tn("conv_out_dw", sv["s_act"], dyc)

        def ln_silu_bwd(dsv, uv, gv, bv):
            xhat, rstd = _layer_norm_parts(uv)
            ln = xhat * gv + bv
            sg = _sig(ln)
            dln = dsv * sg * (1.0 + ln * (1.0 - sg))
            dxhat = dln * gv
            du_ = rstd * (dxhat - jnp.mean(dxhat, axis=-1, keepdims=True)
                          - xhat * jnp.mean(dxhat * xhat, axis=-1, keepdims=True))
            return du_, dln * xhat, dln
        du, dg_cn, db_cn = rowwise(
            "conv_norm_bwd", [('tile', ds_act), ('tile', sv["u"]), ('vec', row(g_cn[l])),
                              ('vec', row(b_cn[l]))],
            [('tile', C, F32), ('acc', C), ('acc', C)], ln_silu_bwd, tm, S)
        dglu_a, dglu_b, dw_acc, db_dw = conv_bwd(du, sv["cu"], sv["glu_a"], sv["glu_b"],
                                                 w_dw_pad[l], tm, S)
        g_wdw[l] = group_sum("conv_dw_rows", dw_acc, SUB)[:CONV_W]
        if l == 0:
            for t, p in zip(GRADS_BEFORE_ATTN, partial_sums(0, GRADS_BEFORE_ATTN)):
                parts[0][t] = p
        riders = [(l + 1, t) for t in everything] if l + 1 < n_l else []
        riders += [(0, t) for t in GRADS_BEFORE_ATTN] if l == 0 else []
        dk_all, dv_all, gq_a, gq_b, *got = attn_bwd(
            sv["q_all"], sv["k_all"], sv["v_all"], sv["o_all"], do_all, sv["lse"], cos_q, sin_q,
            n_e, S, n_h, blk, scale,
            job=chip_exchange_job([parts[ll][t] for ll, t in riders]) if riders else None)
        for (ll, t), b in zip(riders, got):
            received[ll][t] = b
        dqn = mm_fused("q_up_bwd", [gq_a, gq_b], [(0, W["q_a"], True), (1, W["q_b"], True)], [],
                       lambda a, e: (a[0] + a[1],), [F32], tm, QL, S)[0]
        d_wqa = mm_tn("q_up_dw_a", sv["qn"], gq_a)
        d_wqb = mm_tn("q_up_dw_b", sv["qn"], gq_b)
        dq_lat, dg_q = rms_bwd("q_norm_bwd", dqn, sv["q_lat"], row(g_q[l]), QL)

        def k_split(dkv, ckv, skv):
            tot = dkv[:, :SLOT]
            for h in range(1, n_h):
                tot = tot + dkv[:, h * SLOT:(h + 1) * SLOT]
            return dkv, tot * ckv, tot * skv
        dk_b, dkr_a, dkr_b = rowwise("k_rope_bwd", [('tile', dk_all), ('tile', cos_k), ('tile', sin_q)],
                                     [('tile', n_h * SLOT, BF16), ('tile', SLOT, BF16),
                                      ('tile', SLOT, BF16)], k_split, tm, S)
        dkvn = mm_fused("kv_up_bwd", [dk_b, dv_all], [(0, W["kn"], True), (1, W["v"], True)], [],
                        lambda a, e: (a[0] + a[1],), [F32], tm, KVL, S)[0]
        d_wkn = mm_tn("kv_up_dw_k", sv["kvn"], dk_b)
        d_wv = mm_tn("kv_up_dw_v", sv["kvn"], dv_all)
        dkv_lat, dg_kv = rms_bwd("kv_norm_bwd", dkvn, sv["kv_lat"], row(g_kv[l]), KVL)
        segs = [("ga", dga), ("gb", dgb), ("glu_a", dglu_a), ("glu_b", dglu_b), ("ql", dq_lat),
                ("kvl", dkv_lat), ("kr_a", dkr_a), ("kr_b", dkr_b)]
        dh1 = mm_fused("proj_bwd", [g for _, g in segs],
                       [(k, W[nm], True) for k, (nm, _) in enumerate(segs)],
                       [], lambda a, e: (functools.reduce(lambda p, q: p + q, a),), [F32],
                       tm_mm, tn_d, S)[0]
        d_lay = {nm: mm_tn("proj_dw_" + nm, sv["h1"], g) for nm, g in segs}
        d_lay.update(q_a=d_wqa, q_b=d_wqb, kn=d_wkn, v=d_wv, o=d_wo)
        (gfull["w_in"][l], gfull["w_uq"][l], gfull["w_ukv"][l],
         gfull["w_o_attn"][l]) = lay_T({k: d_lay[k] for k in lay[l]})
        dxc, dsh1, dsc1, dg_mix = modnorm_bwd("modnorm_mix_bwd", dh1, sv["x"], dx2,
                                              row(g_mix[l]), sc1)
        dmod_acc[l] = [dsh1, dsc1, dgt1, dsh2, dsc2, dgt2]
        for k, a in (("g_mix", dg_mix), ("g_q", dg_q), ("g_kv", dg_kv), ("b_dw", db_dw),
                     ("g_cn", dg_cn), ("b_cn", db_cn), ("g_ffn", dg_ffn)):
            small_acc[k][l] = a
        rest = [t for t in everything if parts[l][t] is None]
        for t, p in zip(rest, partial_sums(l, rest)):
            parts[l][t] = p

    rest = [t for t in everything if received[0][t] is None]
    for t, b in zip(rest, run_job("grad_chip_exchange", chip_exchange_job([parts[0][t] for t in rest]))):
        received[0][t] = b
    grad_x = dxc.reshape(n_e, S, D)

    dmod_rows = jnp.concatenate([a for l in range(n_l) for a in dmod_acc[l]], axis=0)
    dmod_own = group_sum("dmod_rows", dmod_rows, SUB).reshape(n_l, N_MOD, n_e, D)
    dmod_own = jnp.transpose(dmod_own, (0, 2, 1, 3)).reshape(n_l * n_e, N_MOD * D)
    dmod_all = allgather8("gather_dmod", dmod_own).reshape(N_DEV, n_l, n_e, N_MOD * D)
    dmod_all = jnp.transpose(dmod_all, (1, 0, 2, 3)).reshape(n_l, B, N_MOD * D)
    dmod_mine = lax.dynamic_slice_in_dim(dmod_all, chip * ns_ada, ns_ada, axis=2)
    grad_w_ada = ada_bwd_w(jnp.transpose(c_act), dmod_mine)
    grad_b_ada = group_sum("grad_b_ada", dmod_all.reshape(n_l * B, N_MOD * D), B)

    mine_half = [shard_total("grad_shard_total_" + k, p, b, s_idx)
                 for l in range(n_l) for k, p, b in zip(GATHERED, parts[l], received[l])]
    both = []
    for half, other in zip(mine_half, sibling_share(mine_half)):
        kh, ns = half.shape
        both.append(lax.dynamic_update_slice(
            jnp.broadcast_to(other[None], (2, kh, ns)), half[None],
            (mc.astype(jnp.int32), zero, zero)).reshape(2 * kh, ns))
    n_g = len(GATHERED)
    red = {k: jnp.stack([both[l * n_g + t] for l in range(n_l)]) for t, k in enumerate(GATHERED)}

    wdw_full_g = jnp.stack(g_wdw)
    pieces = [loss_acc, dg_final] + [small_acc[k][l] for k in small_acc for l in range(n_l)]
    widths = [p.shape[1] for p in pieces]
    n_acc = sum(widths)
    wdw_blk = _pad_rows8(wdw_full_g.reshape(-1))
    wdw_w = wdw_blk.shape[1]
    gathered_small = allgather8("gather_small_grads", jnp.concatenate(pieces + [wdw_blk], axis=1))
    acc_sum = group_sum("small_total", gathered_small[:, :, :n_acc].reshape(N_DEV * SUB, n_acc),
                        N_DEV * SUB)
    wdw_sum = group_sum("wdw_total", gathered_small[:, :, n_acc:].reshape(N_DEV, SUB * wdw_w), N_DEV)
    offs = np.cumsum([0] + widths)
    take = lambda i: acc_sum[:, offs[i]:offs[i + 1]]
    loss = lane_total("loss_total", take(0))[0, 0]
    g_small = {"g_final": take(1).reshape(-1)}
    i = 2
    for k in small_acc:
        g_small[k] = jnp.concatenate([take(i + l) for l in range(n_l)], axis=0)
        i += n_l
    wdw_total = wdw_sum.reshape(-1)[:n_wdw * N_CHIPS].reshape(wdw_full_g.shape)
    grad_w_dw = lax.dynamic_slice_in_dim(wdw_total, chip * w_dw.shape[2], w_dw.shape[2], axis=2)

    grads = dict(w_ada=grad_w_ada, b_ada=grad_b_ada, w_dw=grad_w_dw, **g_small, **red)

    deltas, new_m, new_v = {}, {}, {}
    for k in order:
        shp = weights[k].shape
        two = (1, shp[0]) if len(shp) == 1 else (int(np.prod(shp[:-1])), shp[-1])
        d, nm, nv = adamw("adamw_" + k, weights[k].reshape(two), grads[k].reshape(two),
                          mom[k].reshape(two), var[k].reshape(two))
        deltas[k], new_m[k], new_v[k] = d.reshape(shp), nm.reshape(shp), nv.reshape(shp)
        grads[k] = grads[k].reshape(shp)

    return (loss, grad_x, *[grads[k] for k in order], *[deltas[k] for k in order],
            *[new_m[k] for k in order], *[new_v[k] for k in order])
```

```python
import functools

import numpy as np
import jax
import jax.numpy as jnp
from jax import lax
from jax.experimental import pallas as pl
from jax.experimental.pallas import tpu as pltpu

F32 = jnp.float32
BF16 = jnp.bfloat16
MESH = pl.DeviceIdType.MESH

EPS = 1e-6
NEG_INF = -1e30
NOPE, ROPE, VDIM = 64, 32, 64
QK_DIM = NOPE + ROPE
SLOT = 128
CONV_W = 31
HALO = 32
N_MOD = 6
ROPE_THETA = 10000.0
N_CHIPS = 4
N_DEV = 8
SUB = 8
LANES = 128
VMEM_LIMIT = 56 * 1024 * 1024

ADAM_LR, ADAM_B1, ADAM_B2, ADAM_EPS, ADAM_WD, ADAM_STEP = 0.001, 0.9, 0.999, 1e-08, 0.01, 10


def _tile(n, cap, mult):
    best = None
    for d in range(mult, min(n, cap) + 1, mult):
        if n % d == 0:
            best = d
    return best if best is not None else n


def _params(sem):
    return pltpu.CompilerParams(dimension_semantics=sem, vmem_limit_bytes=VMEM_LIMIT)


def _sig(x):
    return 1.0 / (1.0 + jnp.exp(-x))


def _sum8(x):
    r, w = x.shape
    return jnp.sum(x.reshape(r // SUB, SUB, w), axis=0)


def _lanes(v, n):
    return v if n == v.shape[1] else jnp.tile(v, (1, n // v.shape[1]))


def _rstd(x):
    return lax.rsqrt(jnp.mean(x * x, axis=-1, keepdims=True) + EPS)


def _norm_bwd(dxhat, xhat, rstd):
    return rstd * (dxhat - xhat * jnp.mean(dxhat * xhat, axis=-1, keepdims=True))


def mm_fused(name, As, pairs, extras, epilogue, out_dtypes, tm, tn, S, N=None):
    T = As[0].shape[0]
    pairs = [(p[0], p[1], p[2] if len(p) > 2 else False, p[3] if len(p) > 3 else 0) for p in pairs]
    if N is None:
        N = pairs[0][1].shape[0] if pairs[0][2] else pairs[0][1].shape[1]
    nex = S // tm
    in_specs, args = [], []
    for a in As:
        in_specs.append(pl.BlockSpec((tm, a.shape[1]), lambda i, j: (i, 0)))
        args.append(a)
    for ai, b, trans, off in pairs:
        kdim = As[ai].shape[1]
        if trans:
            in_specs.append(pl.BlockSpec((tn, kdim), lambda i, j, off=off: (j, off)))
        else:
            in_specs.append(pl.BlockSpec((kdim, tn), lambda i, j, off=off: (0, j + off)))
        args.append(b)
    for kind, arr in extras:
        if kind == 'tile':
            in_specs.append(pl.BlockSpec((tm, tn), lambda i, j: (i, j)))
        elif kind == 'row128':
            in_specs.append(pl.BlockSpec((tm, LANES), lambda i, j: (i, 0)))
        elif kind == 'vec':
            in_specs.append(pl.BlockSpec((1, tn), lambda i, j: (0, j)))
        else:
            in_specs.append(pl.BlockSpec((None, 1, tn), lambda i, j: (i // nex, 0, j)))
        args.append(arr)
    n_a, n_p, n_e = len(As), len(pairs), len(extras)

    def body(*refs):
        a_refs, b_refs = refs[:n_a], refs[n_a:n_a + n_p]
        e_refs, o_refs = refs[n_a + n_p:n_a + n_p + n_e], refs[n_a + n_p + n_e:]
        accs = [lax.dot_general(a_refs[ai][...], b_refs[k][...], NT if trans else NN,
                                preferred_element_type=F32)
                for k, (ai, _, trans, _) in enumerate(pairs)]
        outs = epilogue(accs, [r[...] for r in e_refs])
        for o_ref, o in zip(o_refs, outs):
            o_ref[...] = o.astype(o_ref.dtype)

    return pl.pallas_call(
        body, name=name, grid=(T // tm, N // tn), in_specs=in_specs,
        out_specs=[pl.BlockSpec((tm, tn), lambda i, j: (i, j)) for _ in out_dtypes],
        out_shape=[jax.ShapeDtypeStruct((T, N), dt) for dt in out_dtypes],
        compiler_params=_params(("parallel", "parallel")))(*args)


def mm_tn(name, A, G):
    T, K = A.shape
    N = G.shape[1]
    tt = _tile(T, 1024, 16)
    tk = _tile(K, 1536, LANES)
    tn = _tile(N, 1536, LANES)

    def body(a_ref, g_ref, o_ref):
        part = lax.dot_general(a_ref[...], g_ref[...], (((0,), (0,)), ((), ())),
                               preferred_element_type=F32)

        @pl.when(pl.program_id(2) == 0)
        def _():
            o_ref[...] = part

        @pl.when(pl.program_id(2) > 0)
        def _():
            o_ref[...] += part

    return pl.pallas_call(
        body, name=name, grid=(K // tk, N // tn, T // tt),
        in_specs=[pl.BlockSpec((tt, tk), lambda k, n, t: (t, k)),
                  pl.BlockSpec((tt, tn), lambda k, n, t: (t, n))],
        out_specs=pl.BlockSpec((tk, tn), lambda k, n, t: (k, n)),
        out_shape=jax.ShapeDtypeStruct((K, N), F32),
        compiler_params=_params(("parallel", "parallel", "arbitrary")))(A, G)


def rowwise(name, ins, outs, fn, tm, S):
    T = next(a.shape[0] for k, a in ins if k == 'tile')
    nex = S // tm
    n_ex = T // S
    in_specs, args = [], []
    for kind, arr in ins:
        if kind == 'tile':
            in_specs.append(pl.BlockSpec((tm, arr.shape[1]), lambda i: (i, 0)))
        elif kind == 'vec':
            in_specs.append(pl.BlockSpec((1, arr.shape[1]), lambda i: (0, 0)))
        else:
            in_specs.append(pl.BlockSpec((None, 1, arr.shape[2]), lambda i: (i // nex, 0, 0)))
        args.append(arr)
    out_specs, out_shape = [], []
    for o in outs:
        if o[0] == 'tile':
            out_specs.append(pl.BlockSpec((tm, o[1]), lambda i: (i, 0)))
            out_shape.append(jax.ShapeDtypeStruct((T, o[1]), o[2]))
        elif o[0] == 'acc':
            out_specs.append(pl.BlockSpec((SUB, o[1]), lambda i: (0, 0)))
            out_shape.append(jax.ShapeDtypeStruct((SUB, o[1]), F32))
        else:
            out_specs.append(pl.BlockSpec((SUB, o[1]), lambda i: (i // nex, 0)))
            out_shape.append(jax.ShapeDtypeStruct((n_ex * SUB, o[1]), F32))
    n_in = len(ins)

    def body(*refs):
        i = pl.program_id(0)
        vals = fn(*[r[...] for r in refs[:n_in]])
        for o, o_ref, v in zip(outs, refs[n_in:], vals):
            if o[0] == 'tile':
                o_ref[...] = v.astype(o_ref.dtype)
            else:
                part = _sum8(v)
                first = (i == 0) if o[0] == 'acc' else (i % nex == 0)

                @pl.when(first)
                def _(o_ref=o_ref, part=part):
                    o_ref[...] = part

                @pl.when(jnp.logical_not(first))
                def _(o_ref=o_ref, part=part):
                    o_ref[...] += part

    return pl.pallas_call(
        body, name=name, grid=(T // tm,), in_specs=in_specs, out_specs=out_specs,
        out_shape=out_shape, compiler_params=_params(("arbitrary",)))(*args)


NN = (((1,), (0,)), ((), ()))
NT = (((1,), (1,)), ((), ()))
TN = (((0,), (0,)), ((), ()))
LN2 = 0.6931471805599453
ATTN_TQ = 1024
ATTN_LB = 512


def _hosted(job, n_in, n_out, grid):
    if job is None:
        return [], [], [], [], (lambda refs: None), (lambda refs: None), []
    ins, outs, n_sems, start, finish = job
    hbm = pl.BlockSpec(memory_space=pl.ANY)
    n_j = len(ins)

    def split(refs):
        j_in = refs[n_in:n_in + n_j]
        j_out = refs[n_in + n_j + n_out:n_in + n_j + n_out + n_j]
        send_sems, recv_sems = refs[-2:]

        def copy(k, src, dst, to):
            return pltpu.make_async_remote_copy(src_ref=src, dst_ref=dst, send_sem=send_sems.at[k],
                                                recv_sem=recv_sems.at[k], device_id=to,
                                                device_id_type=MESH)
        return j_in, j_out, copy

    def at(step_of):
        cond = None
        for axis, size in enumerate(grid):
            hit = pl.program_id(axis) == step_of(size)
            cond = hit if cond is None else jnp.logical_and(cond, hit)
        return cond

    def begin(refs):
        @pl.when(at(lambda size: 0))
        def _():
            start(*split(refs))

    def end(refs):
        @pl.when(at(lambda size: size - 1))
        def _():
            finish(*split(refs))

    sems = [pltpu.SemaphoreType.DMA((n_sems,)), pltpu.SemaphoreType.DMA((n_sems,))]
    return [hbm] * n_j, [hbm] * n_j, list(outs), sems, begin, end, list(ins)


def attn_fwd(Q, K, V, n_b, S, n_h, tq, lb, job=None):
    T = n_b * S
    nq = S // tq
    ratio = tq // lb
    tk = lb
    grid = (n_b, n_h, nq)
    j_in, j_out, j_shapes, j_scratch, begin, end, j_args = _hosted(job, 3, 2, grid)
    n_j = len(j_args)

    def body(*refs):
        q_ref, k_ref, v_ref = refs[:3]
        o_ref, lse_ref = refs[3 + n_j:5 + n_j]
        m_s, l_s, acc_s = refs[5 + 2 * n_j:8 + 2 * n_j]
        begin(refs)
        i = pl.program_id(2)
        m_s[...] = jnp.full(m_s.shape, NEG_INF, F32)
        l_s[...] = jnp.zeros(l_s.shape, F32)
        acc_s[...] = jnp.zeros(acc_s.shape, F32)

        def kv_step(j, diag_off, rsplit):
            start = pl.multiple_of(j * tk, tk)
            k = k_ref[pl.ds(start, tk), :]
            v = v_ref[pl.ds(start, tk), :]
            rc = tq // rsplit
            for r in range(rsplit):
                if diag_off is not None and diag_off > r * rc + rc - 1:
                    continue
                rows = pl.ds(r * rc, rc)
                s = lax.dot_general(q_ref[rows, :], k, NT, preferred_element_type=F32)
                if diag_off is not None and diag_off + tk - 1 > r * rc:
                    rr = lax.broadcasted_iota(jnp.int32, s.shape, 0) + r * rc
                    cc = lax.broadcasted_iota(jnp.int32, s.shape, 1) + diag_off
                    s = jnp.where(rr >= cc, s, NEG_INF)
                m_prev = m_s[rows, :]
                m_new = jnp.maximum(m_prev, jnp.max(s, axis=1, keepdims=True))
                alpha = jnp.exp2(m_prev - m_new)
                p = jnp.exp2(s - _lanes(m_new, tk))
                l_s[rows, :] = alpha * l_s[rows, :] + jnp.sum(p, axis=1, keepdims=True)
                acc_s[rows, :] = alpha * acc_s[rows, :] + jnp.dot(p.astype(BF16), v,
                                                                  preferred_element_type=F32)
                m_s[rows, :] = m_new

        def below_diagonal(j, carry):
            kv_step(j, None, 1)
            return carry
        lax.fori_loop(0, i * ratio, below_diagonal, 0)
        for d in range(ratio):
            kv_step(i * ratio + d, d * tk, ratio)
        l = l_s[...]
        o_ref[...] = (acc_s[...] / l).astype(o_ref.dtype)
        lse = m_s[...] + jnp.log(l) * (1.0 / LN2)
        for u in range(ratio):
            lse_ref[u] = jnp.transpose(lse[u * lb:(u + 1) * lb, :])[:SUB, :]
        end(refs)

    qmap = lambda b, h, i: (b * nq + i, h)
    kmap = lambda b, h, i: (b, h)
    return pl.pallas_call(
        body, name="attn_fwd", grid=grid,
        in_specs=[pl.BlockSpec((tq, SLOT), qmap), pl.BlockSpec((S, SLOT), kmap),
                  pl.BlockSpec((S, SLOT), kmap)] + j_in,
        out_specs=[pl.BlockSpec((tq, SLOT), qmap),
                   pl.BlockSpec((None, ratio, SUB, lb), lambda b, h, i: (b * n_h + h, i, 0, 0))] + j_out,
        out_shape=[jax.ShapeDtypeStruct((T, n_h * SLOT), BF16),
                   jax.ShapeDtypeStruct((n_b * n_h, S // lb, SUB, lb), F32)] + j_shapes,
        scratch_shapes=[pltpu.VMEM((tq, SLOT), F32)] * 3 + j_scratch,
        compiler_params=_params(("arbitrary", "arbitrary", "arbitrary")))(Q, K, V, *j_args)


def attn_bwd(Q, K, V, O, dO, LSE, cosq, sinq, n_b, S, n_h, blk, scale, job=None):
    T = n_b * S
    nb = S // blk
    grid = (n_b, n_h, nb)
    j_in, j_out, j_shapes, j_scratch, begin, end, j_args = _hosted(job, 8, 4, grid)
    n_j = len(j_args)

    def body(*refs):
        q_ref, k_ref, v_ref, o_ref, do_ref, lse_ref, cos_ref, sin_ref = refs[:8]
        dk_ref, dv_ref, ga_ref, gb_ref = refs[8 + n_j:12 + n_j]
        dq_s, delta_s, dk_s, dv_s = refs[12 + 2 * n_j:16 + 2 * n_j]
        begin(refs)
        j = pl.program_id(2)

        @pl.when(j == 0)
        def _():
            dq_s[...] = jnp.zeros(dq_s.shape, F32)
            for i in range(nb):
                rows = pl.ds(i * blk, blk)
                d = jnp.sum(do_ref[rows, :].astype(F32) * o_ref[rows, :].astype(F32),
                            axis=1, keepdims=True)
                delta_s[i] = jnp.transpose(jnp.broadcast_to(d, (blk, SLOT)))[:SUB, :]

        k = k_ref[...]
        v = v_ref[...]
        dk_s[...] = jnp.zeros(dk_s.shape, F32)
        dv_s[...] = jnp.zeros(dv_s.shape, F32)

        def q_step(i, masked):
            rows = pl.ds(pl.multiple_of(i * blk, blk), blk)
            q = q_ref[rows, :]
            do = do_ref[rows, :]
            st = lax.dot_general(k, q, NT, preferred_element_type=F32)
            if masked:
                kv_i = lax.broadcasted_iota(jnp.int32, st.shape, 0)
                q_i = lax.broadcasted_iota(jnp.int32, st.shape, 1)
                st = jnp.where(q_i >= kv_i, st, NEG_INF)
            pt = jnp.exp2(st - lse_ref[i][:1, :])
            dpt = lax.dot_general(v, do, NT, preferred_element_type=F32)
            dst = (pt * (dpt - delta_s[i][:1, :])).astype(BF16)
            dv_s[...] += jnp.dot(pt.astype(BF16), do, preferred_element_type=F32)
            dk_s[...] += jnp.dot(dst, q, preferred_element_type=F32)
            dq_s[rows, :] += lax.dot_general(dst, k, TN, preferred_element_type=F32)

        q_step(j, True)

        def above_diagonal(i, carry):
            q_step(i, False)
            return carry
        lax.fori_loop(j + 1, nb, above_diagonal, 0)
        dk_ref[...] = dk_s[...] * LN2
        dv_ref[...] = dv_s[...].astype(dv_ref.dtype)

        @pl.when(j == nb - 1)
        def _():
            dq = dq_s[...] * scale
            ga_ref[...] = (dq * cos_ref[...]).astype(ga_ref.dtype)
            gb_ref[...] = (dq * sin_ref[...]).astype(gb_ref.dtype)

        end(refs)

    full = pl.BlockSpec((S, SLOT), lambda b, h, j: (b, h))
    kv = pl.BlockSpec((blk, SLOT), lambda b, h, j: (b * nb + j, h))
    tab = pl.BlockSpec((S, SLOT), lambda b, h, j: (b, 0))
    stat = pl.BlockSpec((None, nb, SUB, blk), lambda b, h, j: (b * n_h + h, 0, 0, 0))
    return pl.pallas_call(
        body, name="attn_bwd", grid=grid,
        in_specs=[full, kv, kv, full, full, stat, tab, tab] + j_in,
        out_specs=[kv, kv, full, full] + j_out,
        out_shape=[jax.ShapeDtypeStruct((T, n_h * SLOT), F32)]
        + [jax.ShapeDtypeStruct((T, n_h * SLOT), BF16)] * 3 + j_shapes,
        scratch_shapes=[pltpu.VMEM((S, SLOT), F32), pltpu.VMEM((nb, SUB, blk), F32),
                        pltpu.VMEM((blk, SLOT), F32), pltpu.VMEM((blk, SLOT), F32)] + j_scratch,
        compiler_params=_params(("arbitrary", "arbitrary", "arbitrary")))(
            Q, K, V, O, dO, LSE, cosq, sinq, *j_args)


def _layer_norm_parts(u):
    xc = u - jnp.mean(u, axis=-1, keepdims=True)
    rstd = lax.rsqrt(jnp.mean(xc * xc, axis=-1, keepdims=True) + EPS)
    return xc * rstd, rstd


def _shift_scratch(tm, C):
    return pltpu.VMEM((SUB - 1, tm + HALO - SUB, C), F32)


def _preshift(ext, sh, tm):
    for r in range(1, SUB):
        sh[r - 1] = ext[pl.ds(r, tm + HALO - SUB), :]


def _shifted(ext, sh, off, base, n):
    q, r = divmod(off, SUB)
    src = ext if r == 0 else sh.at[r - 1]
    return src[pl.ds(base + SUB * q, n), :]


CONV_ROWS = 32


def conv_fwd(cu, w, b_dw, g_cn, b_cn, tm, S):
    T, C = cu.shape
    nex, hb = S // tm, tm // HALO

    def body(cur_ref, prev_ref, w_ref, b_ref, g_ref, bc_ref, u_ref, s_ref, ext, sh):
        first = pl.program_id(0) % nex == 0
        ext[pl.ds(0, HALO), :] = jnp.where(first, 0.0, prev_ref[...])
        ext[pl.ds(HALO, tm), :] = cur_ref[...]
        _preshift(ext, sh, tm)

        acc = jnp.zeros((tm, C), F32)
        for j in range(CONV_W):
            acc = acc + w_ref[pl.ds(j, 1), :] * _shifted(ext, sh, HALO - CONV_W + 1 + j, 0, tm)
        u = acc + b_ref[...]
        ln = _layer_norm_parts(u)[0] * g_ref[...] + bc_ref[...]
        u_ref[...] = u
        s_ref[...] = (ln * _sig(ln)).astype(s_ref.dtype)

    vec = pl.BlockSpec((1, C), lambda i: (0, 0))
    return pl.pallas_call(
        body, name="conv_fwd", grid=(T // tm,),
        in_specs=[pl.BlockSpec((tm, C), lambda i: (i, 0)),
                  pl.BlockSpec((HALO, C), lambda i: (jnp.maximum(i * hb - 1, 0), 0)),
                  pl.BlockSpec((HALO, C), lambda i: (0, 0)), vec, vec, vec],
        out_specs=[pl.BlockSpec((tm, C), lambda i: (i, 0))] * 2,
        out_shape=[jax.ShapeDtypeStruct((T, C), F32), jax.ShapeDtypeStruct((T, C), BF16)],
        scratch_shapes=[pltpu.VMEM((HALO + tm, C), F32), _shift_scratch(tm, C)],
        compiler_params=_params(("arbitrary",)))(cu, cu, w, b_dw, g_cn, b_cn)


def conv_bwd(du, cu, glu_a, glu_b, w, tm, S):
    T, C = du.shape
    nex, hb = S // tm, tm // HALO
    last_blk = T // HALO - 1

    def body(du_ref, nxt_ref, cu_ref, prev_ref, a_ref, b_ref, w_ref,
             da_ref, db_ref, dw_ref, dbias_ref, extd, extc, shd, shc):
        i = pl.program_id(0)
        first = i % nex == 0
        last = i % nex == nex - 1
        extd[pl.ds(0, tm), :] = du_ref[...]
        extd[pl.ds(tm, HALO), :] = jnp.where(last, 0.0, nxt_ref[...])
        extc[pl.ds(0, HALO), :] = jnp.where(first, 0.0, prev_ref[...])
        extc[pl.ds(HALO, tm), :] = cu_ref[...]
        _preshift(extd, shd, tm)
        _preshift(extc, shc, tm)

        @pl.when(i == 0)
        def _():
            dw_ref[...] = jnp.zeros(dw_ref.shape, F32)
            dbias_ref[...] = jnp.zeros(dbias_ref.shape, F32)

        def rows_step(ci, carry):
            base = pl.multiple_of(ci * CONV_ROWS, CONV_ROWS)
            rows = pl.ds(base, CONV_ROWS)
            du_rows = du_ref[rows, :]
            dcu = jnp.zeros((CONV_ROWS, C), F32)
            for j in range(CONV_W):
                dcu = dcu + w_ref[pl.ds(j, 1), :] * _shifted(extd, shd, CONV_W - 1 - j,
                                                             base, CONV_ROWS)
                dw_ref[pl.ds(SUB * j, SUB), :] += _sum8(
                    du_rows * _shifted(extc, shc, HALO - CONV_W + 1 + j, base, CONV_ROWS))
            dbias_ref[...] += _sum8(du_rows)
            sb = _sig(b_ref[rows, :])
            da_ref[rows, :] = (dcu * sb).astype(da_ref.dtype)
            db_ref[rows, :] = (dcu * a_ref[rows, :] * sb * (1.0 - sb)).astype(db_ref.dtype)
            return carry
        lax.fori_loop(0, tm // CONV_ROWS, rows_step, 0)

    cur = pl.BlockSpec((tm, C), lambda i: (i, 0))
    return pl.pallas_call(
        body, name="conv_bwd", grid=(T // tm,),
        in_specs=[cur, pl.BlockSpec((HALO, C), lambda i: (jnp.minimum((i + 1) * hb, last_blk), 0)),
                  cur, pl.BlockSpec((HALO, C), lambda i: (jnp.maximum(i * hb - 1, 0), 0)),
                  cur, cur, pl.BlockSpec((HALO, C), lambda i: (0, 0))],
        out_specs=[cur, cur, pl.BlockSpec((HALO * SUB, C), lambda i: (0, 0)),
                   pl.BlockSpec((SUB, C), lambda i: (0, 0))],
        out_shape=[jax.ShapeDtypeStruct((T, C), BF16), jax.ShapeDtypeStruct((T, C), BF16),
                   jax.ShapeDtypeStruct((HALO * SUB, C), F32), jax.ShapeDtypeStruct((SUB, C), F32)],
        scratch_shapes=[pltpu.VMEM((tm + HALO, C), F32), pltpu.VMEM((HALO + tm, C), F32),
                        _shift_scratch(tm, C), _shift_scratch(tm, C)],
        compiler_params=_params(("arbitrary",)))(du, du, cu, cu, glu_a, glu_b, w)


def silu_small(c_all):
    def body(c_ref, o_ref):
        v = c_ref[...]
        o_ref[...] = v * _sig(v)
    return pl.pallas_call(body, name="silu_c", out_shape=jax.ShapeDtypeStruct(c_all.shape, F32))(c_all)


def ada_fwd(c_act, w_ada, b_ada):
    n_l, D, ns = w_ada.shape
    B = c_act.shape[0]
    tn = _tile(ns, 512, LANES)

    def body(c_ref, w_ref, b_ref, o_ref):
        o_ref[...] = jnp.dot(c_ref[...], w_ref[...], preferred_element_type=F32,
                             precision=lax.Precision.HIGHEST) + b_ref[...]

    return pl.pallas_call(
        body, name="ada_fwd", grid=(n_l, ns // tn),
        in_specs=[pl.BlockSpec((B, D), lambda l, j: (0, 0)),
                  pl.BlockSpec((None, D, tn), lambda l, j: (l, 0, j)),
                  pl.BlockSpec((None, 1, tn), lambda l, j: (l, 0, j))],
        out_specs=pl.BlockSpec((None, B, tn), lambda l, j: (l, 0, j)),
        out_shape=jax.ShapeDtypeStruct((n_l, B, ns), F32),
        compiler_params=_params(("parallel", "parallel")))(c_act, w_ada, b_ada.reshape(n_l, 1, ns))


def ada_bwd_w(c_act_t, dmod):
    D, B = c_act_t.shape
    n_l, _, ns = dmod.shape
    tn = _tile(ns, 512, LANES)

    def body(c_ref, d_ref, o_ref):
        o_ref[...] = jnp.dot(c_ref[...], d_ref[...], preferred_element_type=F32,
                             precision=lax.Precision.HIGHEST)

    return pl.pallas_call(
        body, name="ada_bwd_w", grid=(n_l, ns // tn),
        in_specs=[pl.BlockSpec((D, B), lambda l, j: (0, 0)),
                  pl.BlockSpec((None, B, tn), lambda l, j: (l, 0, j))],
        out_specs=pl.BlockSpec((None, D, tn), lambda l, j: (l, 0, j)),
        out_shape=jax.ShapeDtypeStruct((n_l, D, ns), F32),
        compiler_params=_params(("parallel", "parallel")))(c_act_t, dmod)


def group_sum(name, v, group):
    rows, W = v.shape
    n = rows // group

    def body(v_ref, o_ref):
        o_ref[...] = jnp.sum(v_ref[...].reshape(n, group, W), axis=1)

    return pl.pallas_call(body, name=name, out_shape=jax.ShapeDtypeStruct((n, W), F32),
                          compiler_params=_params(None))(v)


def lane_total(name, v):
    def body(v_ref, o_ref):
        o_ref[...] = jnp.broadcast_to(jnp.sum(v_ref[...], axis=1, keepdims=True), o_ref.shape)
    return pl.pallas_call(body, name=name, out_shape=jax.ShapeDtypeStruct((1, LANES), F32))(v)


def adamw(name, w, g, m, v):
    rows, cols = w.shape
    tr = _tile(rows, max(SUB, (1 << 19) // cols // SUB * SUB), SUB)

    def body(w_ref, g_ref, m_ref, v_ref, d_ref, nm_ref, nv_ref):
        gg = g_ref[...]
        nm = ADAM_B1 * m_ref[...] + (1.0 - ADAM_B1) * gg
        nv = ADAM_B2 * v_ref[...] + (1.0 - ADAM_B2) * (gg * gg)
        m_hat = nm / (1.0 - ADAM_B1 ** ADAM_STEP)
        v_hat = nv / (1.0 - ADAM_B2 ** ADAM_STEP)
        d_ref[...] = -ADAM_LR * (m_hat / (jnp.sqrt(v_hat) + ADAM_EPS) + ADAM_WD * w_ref[...])
        nm_ref[...] = nm
        nv_ref[...] = nv

    spec = pl.BlockSpec((tr, cols), lambda i: (i, 0))
    return pl.pallas_call(
        body, name=name, grid=(rows // tr,), in_specs=[spec] * 4, out_specs=[spec] * 3,
        out_shape=[jax.ShapeDtypeStruct((rows, cols), F32)] * 3,
        compiler_params=_params(("parallel",)))(w, g, m, v)


def rope_tables(pos_col, invf, one_nope, rope_mask, tm):
    T = pos_col.shape[0]

    def body(p_ref, f_ref, o_ref, r_ref, cq_ref, sq_ref, ck_ref):
        ang = p_ref[...] * f_ref[...]
        cs = jnp.cos(ang) * r_ref[...]
        cq_ref[...] = o_ref[...] + cs
        sq_ref[...] = jnp.sin(ang) * r_ref[...]
        ck_ref[...] = cs

    vec = pl.BlockSpec((1, LANES), lambda i: (0, 0))
    out = pl.BlockSpec((tm, LANES), lambda i: (i, 0))
    return pl.pallas_call(
        body, name="rope_tables", grid=(T // tm,),
        in_specs=[pl.BlockSpec((tm, 1), lambda i: (i, 0)), vec, vec, vec], out_specs=[out] * 3,
        out_shape=[jax.ShapeDtypeStruct((T, LANES), F32)] * 3,
        compiler_params=_params(("parallel",)))(pos_col, invf, one_nope, rope_mask)


def _place():
    return lax.axis_index("x"), lax.axis_index("y"), lax.axis_index("c")


def allgather8(name, v):
    R, W = v.shape

    def body(x_ref, out_ref, send_sems, recv_sems, local_sem):
        x, y, c = _place()
        me = 4 * x + 2 * y + c
        mine = pltpu.make_async_copy(x_ref, out_ref.at[me], local_sem)
        mine.start()
        sends, peers = [], []
        for k in range(1, N_DEV):
            px, py, pc = x ^ ((k >> 2) & 1), y ^ ((k >> 1) & 1), c ^ (k & 1)
            peers.append((px, py, pc))
            cp = pltpu.make_async_remote_copy(
                src_ref=x_ref, dst_ref=out_ref.at[me], send_sem=send_sems.at[k - 1],
                recv_sem=recv_sems.at[k - 1], device_id=(px, py, pc), device_id_type=MESH)
            cp.start()
            sends.append(cp)
        for k, (px, py, pc) in enumerate(peers):
            pltpu.make_async_remote_copy(
                src_ref=x_ref, dst_ref=out_ref.at[4 * px + 2 * py + pc], send_sem=send_sems.at[k],
                recv_sem=recv_sems.at[k], device_id=(px, py, pc), device_id_type=MESH).wait_recv()
        for cp in sends:
            cp.wait_send()
        mine.wait()

    return pl.pallas_call(
        body, name=name, out_shape=jax.ShapeDtypeStruct((N_DEV, R, W), v.dtype),
        in_specs=[pl.BlockSpec(memory_space=pltpu.VMEM)],
        out_specs=pl.BlockSpec(memory_space=pltpu.VMEM),
        scratch_shapes=[pltpu.SemaphoreType.DMA((N_DEV - 1,)), pltpu.SemaphoreType.DMA((N_DEV - 1,)),
                        pltpu.SemaphoreType.DMA])(v)


def _other_chips(x, y):
    return [(1 - x, y), (x, 1 - y), (1 - x, 1 - y)]


def _hbm_exchange(name, body, ins, out_shapes, n_sems):
    n_in, n_out = len(ins), len(out_shapes)

    def wrapped(*refs):
        send_sems, recv_sems = refs[n_in + n_out:]

        def copy(k, src, dst, to):
            return pltpu.make_async_remote_copy(src_ref=src, dst_ref=dst, send_sem=send_sems.at[k],
                                                recv_sem=recv_sems.at[k], device_id=to,
                                                device_id_type=MESH)
        body(refs[:n_in], refs[n_in:n_in + n_out], copy)

    hbm = pl.BlockSpec(memory_space=pl.ANY)
    return pl.pallas_call(
        wrapped, name=name, out_shape=out_shapes, in_specs=[hbm] * n_in, out_specs=[hbm] * n_out,
        scratch_shapes=[pltpu.SemaphoreType.DMA((n_sems,)), pltpu.SemaphoreType.DMA((n_sems,))])(*ins)


def _gather_start(w_refs, o_refs, copy):
    x, y, c = _place()
    for t, (w, o) in enumerate(zip(w_refs, o_refs)):
        for j, (cx, cy) in enumerate(_other_chips(x, y)):
            copy(6 * t + j, w.at[c], o.at[2 * x + y, c], (cx, cy, c)).start()


def _gather_finish(w_refs, o_refs, copy):
    x, y, c = _place()
    chips = _other_chips(x, y)
    for t, (w, o) in enumerate(zip(w_refs, o_refs)):
        for j, (cx, cy) in enumerate(chips):
            landed = o.at[2 * cx + cy, c]
            copy(6 * t + j, w.at[c], landed, (cx, cy, c)).wait_recv()
            copy(6 * t + 3 + j, landed, landed, (x, y, 1 - c)).start()
    for t, (w, o) in enumerate(zip(w_refs, o_refs)):
        for j, (cx, cy) in enumerate(chips):
            copy(6 * t + 3 + j, w.at[c], o.at[2 * cx + cy, 1 - c], (x, y, 1 - c)).wait_recv()
    for t, (w, o) in enumerate(zip(w_refs, o_refs)):
        for j, (cx, cy) in enumerate(chips):
            copy(6 * t + j, w.at[c], o.at[2 * x + y, c], (cx, cy, c)).wait_send()
            landed = o.at[2 * cx + cy, c]
            copy(6 * t + 3 + j, landed, landed, (x, y, 1 - c)).wait_send()


def weight_gather_job(ws):
    return (ws, [jax.ShapeDtypeStruct((N_CHIPS,) + w.shape, w.dtype) for w in ws], 6 * len(ws),
            _gather_start, _gather_finish)


def _exchange_start(p_refs, b_refs, copy):
    x, y, c = _place()
    for t, (p, b) in enumerate(zip(p_refs, b_refs)):
        for j, (cx, cy) in enumerate(_other_chips(x, y)):
            copy(3 * t + j, p.at[2 * cx + cy], b.at[j], (cx, cy, c)).start()


def _exchange_finish(p_refs, b_refs, copy):
    x, y, c = _place()
    for t, (p, b) in enumerate(zip(p_refs, b_refs)):
        for j, (cx, cy) in enumerate(_other_chips(x, y)):
            copy(3 * t + j, p.at[2 * cx + cy], b.at[j], (cx, cy, c)).wait()


def chip_exchange_job(ps):
    return (ps, [jax.ShapeDtypeStruct((3,) + p.shape[1:], p.dtype) for p in ps], 3 * len(ps),
            _exchange_start, _exchange_finish)


def run_job(name, job):
    ins, outs, n_sems, start, finish = job

    def body(in_refs, out_refs, copy):
        start(in_refs, out_refs, copy)
        finish(in_refs, out_refs, copy)
    return _hbm_exchange(name, body, ins, outs, n_sems)


def sibling_swap(gs):
    n = len(gs)

    def body(g_refs, a_refs, copy):
        x, y, c = _place()
        cps = []
        for t in range(n):
            cp = copy(t, g_refs[t].at[pl.ds(0, N_CHIPS), 1 - c], a_refs[t], (x, y, 1 - c))
            cp.start()
            cps.append(cp)
        for cp in cps:
            cp.wait()

    return _hbm_exchange("grad_sibling_swap", body, gs,
                         [jax.ShapeDtypeStruct(g.shape[:1] + g.shape[2:], g.dtype) for g in gs], n)


def chip_partial(name, g, a, c_idx):
    n_s, _, kh, ns = g.shape
    tr = _tile(kh, max(16, (1 << 19) // ns // 16 * 16), 16)

    def body(c_ref, g_ref, a_ref, o_ref):
        o_ref[...] = (g_ref[...] + a_ref[...]).astype(o_ref.dtype)

    return pl.pallas_call(
        body, name=name,
        grid_spec=pltpu.PrefetchScalarGridSpec(
            num_scalar_prefetch=1, grid=(n_s, kh // tr),
            in_specs=[pl.BlockSpec((None, None, tr, ns), lambda i, r, cr: (i, cr[0], r, 0)),
                      pl.BlockSpec((None, tr, ns), lambda i, r, cr: (i, r, 0))],
            out_specs=pl.BlockSpec((None, tr, ns), lambda i, r, cr: (i, r, 0))),
        out_shape=jax.ShapeDtypeStruct((n_s, kh, ns), BF16),
        compiler_params=_params(("parallel", "parallel")))(c_idx, g, a)


def shard_total(name, p, b, s_idx):
    _, rows, ns = p.shape
    tr = _tile(rows, max(16, (1 << 19) // ns // 16 * 16), 16)

    def body(s_ref, p_ref, b0, b1, b2, o_ref):
        o_ref[...] = ((p_ref[...].astype(F32) + b0[...].astype(F32)) + b1[...].astype(F32)
                      ) + b2[...].astype(F32)

    def bspec(j):
        return pl.BlockSpec((None, tr, ns), lambda r, sr: (j, r, 0))

    return pl.pallas_call(
        body, name=name,
        grid_spec=pltpu.PrefetchScalarGridSpec(
            num_scalar_prefetch=1, grid=(rows // tr,),
            in_specs=[pl.BlockSpec((None, tr, ns), lambda r, sr: (sr[0], r, 0)),
                      bspec(0), bspec(1), bspec(2)],
            out_specs=pl.BlockSpec((tr, ns), lambda r, sr: (r, 0))),
        out_shape=jax.ShapeDtypeStruct((rows, ns), F32),
        compiler_params=_params(("parallel",)))(s_idx, p, b, b, b)


def sibling_share(fs):
    n = len(fs)

    def body(f_refs, o_refs, copy):
        x, y, c = _place()
        cps = []
        for t in range(n):
            cp = copy(t, f_refs[t], o_refs[t], (x, y, 1 - c))
            cp.start()
            cps.append(cp)
        for cp in cps:
            cp.wait()

    return _hbm_exchange("grad_sibling_share", body, fs,
                         [jax.ShapeDtypeStruct(f.shape, f.dtype) for f in fs], n)


def _rot_cols(w):
    h = w.shape[-1] // 2
    return jnp.concatenate([-w[..., h:], w[..., :h]], axis=-1)


def _slots(parts, lead, n_h):
    width = sum(p.shape[-1] for p in parts)
    pad = jnp.zeros(lead + (n_h, SLOT - width), parts[0].dtype)
    return jnp.concatenate(parts + [pad], axis=-1).reshape(lead + (n_h * SLOT,))


def layout_weights(w_in, w_uq, w_ukv, w_o_attn, dims):
    QL, KVL, C, D, n_h = dims
    o = 0
    w_ql, o = w_in[..., o:o + QL], o + QL
    w_kvl, o = w_in[..., o:o + KVL], o + KVL
    w_kr, o = w_in[..., o:o + ROPE], o + ROPE
    w_glu_a, o = w_in[..., o:o + C], o + C
    w_glu_b, o = w_in[..., o:o + C], o + C
    w_ga, o = w_in[..., o:o + D], o + D
    w_gb = w_in[..., o:o + D]
    z = lambda n: jnp.zeros(w_kr.shape[:-1] + (n,), w_kr.dtype)
    kr_a = jnp.concatenate([z(NOPE), w_kr, z(SLOT - QK_DIM)], axis=-1)
    kr_b = jnp.concatenate([z(NOPE), _rot_cols(w_kr), z(SLOT - QK_DIM)], axis=-1)
    lead = w_uq.shape[:-1]
    q = w_uq.reshape(lead + (n_h, QK_DIM))
    zq = jnp.zeros(lead + (n_h, NOPE), w_uq.dtype)
    wq_a = _slots([q[..., :NOPE], q[..., NOPE:]], lead, n_h)
    wq_b = _slots([zq, _rot_cols(q[..., NOPE:])], lead, n_h)
    lead = w_ukv.shape[:-1]
    kv = w_ukv.reshape(lead + (n_h, NOPE + VDIM))
    w_kn = _slots([kv[..., :NOPE]], lead, n_h)
    w_v = _slots([kv[..., NOPE:]], lead, n_h)
    lead = w_o_attn.shape[:-2]
    wo = w_o_attn.reshape(lead + (n_h, VDIM, D))
    wo = jnp.concatenate([wo, jnp.zeros(lead + (n_h, SLOT - VDIM, D), wo.dtype)], axis=-2)
    w_o = wo.reshape(lead + (n_h * SLOT, D))
    return dict(ql=w_ql, kvl=w_kvl, kr_a=kr_a, kr_b=kr_b, glu_a=w_glu_a, glu_b=w_glu_b,
                ga=w_ga, gb=w_gb, q_a=wq_a, q_b=wq_b, kn=w_kn, v=w_v, o=w_o)


GATHERED = ("w_in", "w_uq", "w_ukv", "w_o_attn", "w_pw2", "w_out", "w_gu", "w_down")
ROW_SHARDED = ("w_out", "w_down")
N_BEFORE_ATTN = 6
GRADS_BEFORE_ATTN = (4, 5, 6, 7)


def halves(a):
    return a.reshape(a.shape[:-2] + (2, a.shape[-2] // 2, a.shape[-1]))


def full_layer(name, gathered):
    n_c, _, kh, ns = gathered.shape
    if name in ROW_SHARDED:
        return gathered.reshape(n_c * 2 * kh, ns)
    return jnp.concatenate([gathered[s].reshape(2 * kh, ns) for s in range(n_c)], axis=1)


def owed_pieces(name, g):
    if name in ROW_SHARDED:
        return halves(g.reshape(N_CHIPS, g.shape[0] // N_CHIPS, g.shape[1]))
    k, n = g.shape
    return halves(jnp.transpose(g.reshape(k, N_CHIPS, n // N_CHIPS), (1, 0, 2)))


def _pad_rows8(flat):
    n = flat.shape[0]
    w = -(-n // (SUB * LANES)) * LANES
    return jnp.concatenate([flat, jnp.zeros((SUB * w - n,), flat.dtype)]).reshape(SUB, w)


def kernel(x, c, positions, w_ada, b_ada, g_mix, w_in, g_q, w_uq, g_kv, w_ukv, w_o_attn, w_dw, b_dw, g_cn, b_cn, w_pw2, w_out, g_ffn, w_gu, w_down, g_final, loss_target, m_w_ada, m_b_ada, m_g_mix, m_w_in, m_g_q, m_w_uq, m_g_kv, m_w_ukv, m_w_o_attn, m_w_dw, m_b_dw, m_g_cn, m_b_cn, m_w_pw2, m_w_out, m_g_ffn, m_w_gu, m_w_down, m_g_final, v_w_ada, v_b_ada, v_g_mix, v_w_in, v_g_q, v_w_uq, v_g_kv, v_w_ukv, v_w_o_attn, v_w_dw, v_b_dw, v_g_cn, v_b_cn, v_w_pw2, v_w_out, v_g_ffn, v_w_gu, v_w_down, v_g_final):
    weights = dict(w_ada=w_ada, b_ada=b_ada, g_mix=g_mix, w_in=w_in, g_q=g_q, w_uq=w_uq, g_kv=g_kv,
                   w_ukv=w_ukv, w_o_attn=w_o_attn, w_dw=w_dw, b_dw=b_dw, g_cn=g_cn, b_cn=b_cn,
                   w_pw2=w_pw2, w_out=w_out, g_ffn=g_ffn, w_gu=w_gu, w_down=w_down, g_final=g_final)
    mom = dict(w_ada=m_w_ada, b_ada=m_b_ada, g_mix=m_g_mix, w_in=m_w_in, g_q=m_g_q, w_uq=m_w_uq,
               g_kv=m_g_kv, w_ukv=m_w_ukv, w_o_attn=m_w_o_attn, w_dw=m_w_dw, b_dw=m_b_dw,
               g_cn=m_g_cn, b_cn=m_b_cn, w_pw2=m_w_pw2, w_out=m_w_out, g_ffn=m_g_ffn, w_gu=m_w_gu,
               w_down=m_w_down, g_final=m_g_final)
    var = dict(w_ada=v_w_ada, b_ada=v_b_ada, g_mix=v_g_mix, w_in=v_w_in, g_q=v_g_q, w_uq=v_w_uq,
               g_kv=v_g_kv, w_ukv=v_w_ukv, w_o_attn=v_w_o_attn, w_dw=v_w_dw, b_dw=v_b_dw,
               g_cn=v_g_cn, b_cn=v_b_cn, w_pw2=v_w_pw2, w_out=v_w_out, g_ffn=v_g_ffn, w_gu=v_w_gu,
               w_down=v_w_down, g_final=v_g_final)
    order = list(weights)

    n_e, S, D = x.shape
    T = n_e * S
    n_l = w_in.shape[0]
    QL, KVL, C = g_q.shape[1], g_kv.shape[1], g_cn.shape[1]
    n_h = w_uq.shape[2] * N_CHIPS // QK_DIM
    F = w_gu.shape[2] * N_CHIPS // 2
    B = n_e * N_DEV
    dims = (QL, KVL, C, D, n_h)
    scale = QK_DIM ** -0.5
    tm = _tile(S, 512, HALO)
    tm_mm = _tile(S, 1024, HALO)
    blk = _tile(S, ATTN_LB, LANES)
    tq = _tile(S, ATTN_TQ, blk)
    q_scale = scale / LN2
    mx, my, mc = _place()
    dev = 4 * mx + 2 * my + mc
    chip = 2 * mx + my
    c_idx = jnp.reshape(mc, (1,)).astype(jnp.int32)
    s_idx = jnp.reshape(chip, (1,)).astype(jnp.int32)

    xt = x.reshape(T, D)
    tgt = loss_target.reshape(T, D)

    mine = [[halves(weights[k][l].astype(BF16)) for k in GATHERED] for l in range(n_l)]
    zero = jnp.zeros((), jnp.int32)
    full = {k: [None] * n_l for k in GATHERED}
    lay = [None] * n_l

    def take_weights(l, which, got):
        for t, g in zip(which, got):
            k, w = GATHERED[t], mine[l][t]
            g = lax.dynamic_update_slice(g, w[None], (chip.astype(jnp.int32), zero, zero, zero))
            full[k][l] = full_layer(k, g)
        if lay[l] is None:
            lay[l] = layout_weights(full["w_in"][l], full["w_uq"][l], full["w_ukv"][l],
                                    full["w_o_attn"][l], dims)

    everything = list(range(len(GATHERED)))
    early, late = everything[:N_BEFORE_ATTN], everything[N_BEFORE_ATTN:]
    take_weights(0, early, run_job("weight_allgather",
                                   weight_gather_job([mine[0][t] for t in early])))

    ns_ada = w_ada.shape[2]
    n_c, n_wdw = n_e * D, int(np.prod(w_dw.shape))
    small_all = allgather8("gather_c_wdw", _pad_rows8(
        jnp.concatenate([c.reshape(-1), w_dw.reshape(-1)]))).reshape(N_DEV, -1)
    c_all = small_all[:, :n_c].reshape(B, D)
    wdw_parts = small_all[:, n_c:n_c + n_wdw].reshape((N_DEV,) + w_dw.shape)
    w_dw_full = jnp.concatenate([wdw_parts[2 * s] for s in range(N_CHIPS)], axis=2)
    w_dw_pad = jnp.concatenate([w_dw_full, jnp.zeros((n_l, HALO - CONV_W, C), F32)], axis=1)

    c_act = silu_small(c_all)
    b_ada_mine = lax.dynamic_slice_in_dim(b_ada, chip * ns_ada, ns_ada, axis=1)
    mod_part = ada_fwd(c_act, w_ada, b_ada_mine)
    mod_all = allgather8("gather_mod", mod_part.reshape(n_l * B, ns_ada)).reshape(
        N_DEV, n_l, B, ns_ada)
    mod_full = jnp.concatenate([mod_all[2 * s] for s in range(N_CHIPS)], axis=2)
    mod_mine = lax.dynamic_slice_in_dim(mod_full, dev * n_e, n_e, axis=1)
    mods = mod_mine.reshape(n_l, n_e, N_MOD, 1, D)


    lane = np.arange(LANES)
    in_rope = (lane >= NOPE) & (lane < QK_DIM)
    inv_freq = ROPE_THETA ** (-np.arange(0, ROPE, 2, dtype=np.float32) / ROPE)
    invf = np.where(in_rope, inv_freq[(lane - NOPE) % (ROPE // 2)], 0.0).astype(np.float32)
    cos_q, sin_q, cos_k = rope_tables(
        positions.astype(F32).reshape(T, 1), jnp.asarray(invf).reshape(1, LANES),
        jnp.asarray((lane < NOPE).astype(np.float32)).reshape(1, LANES),
        jnp.asarray(in_rope.astype(np.float32)).reshape(1, LANES), tm)

    def rope_epi(accs, ex):
        n = accs[0].shape[1]
        return (accs[0] * _lanes(ex[0], n) + accs[1] * _lanes(ex[1], n),)

    def rms_epi(accs, ex):
        a = accs[0]
        return a, a * _rstd(a) * ex[0]

    def modnorm(name, xin, g, sc, sh):
        def fn(xv, gv, scv, shv):
            return (xv * _rstd(xv) * gv * (1.0 + scv) + shv,)
        return rowwise(name, [('tile', xin), ('vec', g), ('exvec', sc), ('exvec', sh)],
                       [('tile', D, BF16)], fn, tm, S)[0]

    def modnorm_bwd(name, dh, xin, dres, g, sc, gate=None):
        def fn(dhv, xv, drv, gv, scv, *gate_vals):
            rstd = _rstd(xv)
            xhat = xv * rstd
            dx = _norm_bwd(dhv * gv * (1.0 + scv), xhat, rstd) + drv
            outs = (dx, dhv, dhv * xhat * gv, dhv * xhat * (1.0 + scv))
            if gate_vals:
                outs += (dx * gate_vals[1], dx * gate_vals[0])
            return outs
        ins = [('tile', dh), ('tile', xin), ('tile', dres), ('vec', g), ('exvec', sc)]
        outs = [('tile', D, F32), ('exacc', D), ('exacc', D), ('acc', D)]
        if gate is not None:
            ins += [('tile', gate[0]), ('exvec', gate[1])]
            outs += [('tile', D, BF16), ('exacc', D)]
        return rowwise(name, ins, outs, fn, tm, S)

    def rms_bwd(name, dy, xin, g, width):
        def fn(dyv, xv, gv):
            rstd = _rstd(xv)
            xhat = xv * rstd
            return _norm_bwd(dyv * gv, xhat, rstd), dyv * xhat
        return rowwise(name, [('tile', dy), ('tile', xin), ('vec', g)],
                       [('tile', width, BF16), ('acc', width)], fn, tm, S)

    def gate_bwd(name, dxo, branch, gt):
        def fn(dv, bv, gv):
            return dv * gv, dv * bv
        return rowwise(name, [('tile', dxo), ('tile', branch), ('exvec', gt)],
                       [('tile', D, BF16), ('exacc', D)], fn, tm, S)

    tn_d = _tile(D, 512, LANES)
    tn_f = _tile(F, 1536, LANES)
    tn_s = _tile(n_h * SLOT, 512, LANES)
    row = lambda a: a.reshape(1, -1)

    saved = []
    xc = xt
    for l in range(n_l):
        W = lay[l]
        sh1, sc1, gt1, sh2, sc2, gt2 = [mods[l, :, k] for k in range(N_MOD)]
        h1 = modnorm("modnorm_mix", xc, row(g_mix[l]), sc1, sh1)
        ga, gb = mm_fused("proj_gates", [h1], [(0, W["ga"]), (0, W["gb"])], [], lambda a, e: a,
                          [F32, F32], tm_mm, tn_d, S)
        cu, glu_a, glu_b = mm_fused(
            "proj_glu", [h1], [(0, W["glu_a"]), (0, W["glu_b"])], [],
            lambda a, e: (a[0] * _sig(a[1]), a[0], a[1]), [F32, F32, F32], tm_mm, C, S)
        q_lat, qn = mm_fused("proj_q_lat", [h1], [(0, W["ql"])], [('vec', row(g_q[l]))], rms_epi,
                             [F32, BF16], tm_mm, QL, S)
        kv_lat, kvn = mm_fused("proj_kv_lat", [h1], [(0, W["kvl"])], [('vec', row(g_kv[l]))],
                               rms_epi, [F32, BF16], tm_mm, KVL, S)
        kr = mm_fused("proj_k_rope", [h1], [(0, W["kr_a"]), (0, W["kr_b"])],
                      [('row128', cos_k), ('row128', sin_q)], rope_epi, [F32], tm_mm, SLOT, S)[0]
        q_all = mm_fused("q_up", [qn], [(0, W["q_a"]), (0, W["q_b"])],
                         [('row128', cos_q), ('row128', sin_q)],
                         lambda a, e: (rope_epi(a, e)[0] * q_scale,), [BF16], tm_mm, tn_s, S)[0]
        k_all, v_all = mm_fused(
            "kv_up", [kvn], [(0, W["kn"]), (0, W["v"])], [('row128', kr)],
            lambda a, e: (a[0] + _lanes(e[0], a[0].shape[1]), a[1]), [BF16, BF16], tm_mm, tn_s, S)
        riders = [(0, t) for t in late] if l == 0 else []
        riders += [(l + 1, t) for t in everything] if l + 1 < n_l else []
        o_all, lse, *got = attn_fwd(
            q_all, k_all, v_all, n_e, S, n_h, tq, blk,
            job=weight_gather_job([mine[ll][t] for ll, t in riders]) if riders else None)
        for ll in sorted({ll for ll, _ in riders}):
            take_weights(ll, [t for l2, t in riders if l2 == ll],
                         [g for (l2, _), g in zip(riders, got) if l2 == ll])
        u, s_act = conv_fwd(cu, w_dw_pad[l], row(b_dw[l]), row(g_cn[l]), row(b_cn[l]), tm, S)

        def merge_epi(a, e):
            return _sig(e[0]) * a[0] + _sig(e[1]) * a[1], a[0], a[1]
        y, ya, yc = mm_fused("merge", [o_all, s_act], [(0, W["o"]), (1, full["w_pw2"][l])],
                             [('tile', ga), ('tile', gb)], merge_epi, [BF16, F32, F32], tm_mm, tn_d, S)
        x2, o_mix = mm_fused("mix_out", [y], [(0, full["w_out"][l])], [('tile', xc), ('exvec', gt1)],
                             lambda a, e: (e[0] + e[1] * a[0], a[0]), [F32, F32], tm_mm, tn_d, S)
        h2 = modnorm("modnorm_ffn", x2, row(g_ffn[l]), sc2, sh2)

        def swiglu_epi(a, e):
            return a[0], a[1], a[0] * _sig(a[0]) * a[1]
        w_gu_l = full["w_gu"][l]
        g_act, up, act = mm_fused("ffn_up", [h2], [(0, w_gu_l), (0, w_gu_l, False, F // tn_f)], [],
                                  swiglu_epi, [BF16, BF16, BF16], tm_mm, tn_f, S, N=F)
        x3, dn = mm_fused("ffn_down", [act], [(0, full["w_down"][l])], [('tile', x2), ('exvec', gt2)],
                          lambda a, e: (e[0] + e[1] * a[0], a[0]), [F32, F32], tm_mm, tn_d, S)
        saved.append(dict(x=xc, h1=h1, ga=ga, gb=gb, cu=cu, glu_a=glu_a, glu_b=glu_b, q_lat=q_lat,
                          qn=qn, kv_lat=kv_lat, kvn=kvn, q_all=q_all, k_all=k_all, v_all=v_all,
                          o_all=o_all, lse=lse, u=u, s_act=s_act, y=y, ya=ya, yc=yc, x2=x2,
                          o_mix=o_mix, h2=h2, g_act=g_act, up=up, act=act, dn=dn))
        xc = x3

    def loss_fn(xv, tv, gv):
        rstd = _rstd(xv)
        xhat = xv * rstd
        err = xhat * gv - tv
        dy = err * (1.0 / D)
        return _norm_bwd(dy * gv, xhat, rstd), err * err * (0.5 / D), dy * xhat
    dxc, loss_acc, dg_final = rowwise("loss_head", [('tile', xc), ('tile', tgt), ('vec', row(g_final))],
                                      [('tile', D, F32), ('acc', D), ('acc', D)], loss_fn, tm, S)

    gfull = {k: [None] * n_l for k in GATHERED}
    g_wdw = [None] * n_l
    small_acc = {k: [None] * n_l for k in ("g_mix", "g_q", "g_kv", "b_dw", "g_cn", "b_cn", "g_ffn")}
    dmod_acc = [None] * n_l
    parts = [[None] * len(GATHERED) for _ in range(n_l)]
    received = [[None] * len(GATHERED) for _ in range(n_l)]

    def partial_sums(l, which):
        owed = [owed_pieces(GATHERED[t], gfull[GATHERED[t]][l]) for t in which]
        return [chip_partial("grad_chip_partial_" + GATHERED[t], g, a, c_idx)
                for t, g, a in zip(which, owed, sibling_swap(owed))]

    lay_T = jax.linear_transpose(
        lambda a, b, cc, d: layout_weights(a, b, cc, d, dims),
        *[jax.ShapeDtypeStruct(full[k][0].shape, F32) for k in ("w_in", "w_uq", "w_ukv", "w_o_attn")])

    ffn_gate = None
    for l in reversed(range(n_l)):
        sv = saved[l]
        W = lay[l]
        sh1, sc1, gt1, sh2, sc2, gt2 = [mods[l, :, k] for k in range(N_MOD)]
        w_gu_l = full["w_gu"][l]
        ddn, dgt2 = gate_bwd("ffn_gate_bwd", dxc, sv["dn"], gt2) if ffn_gate is None else ffn_gate

        def swiglu_bwd_epi(a, e):
            gv, uv = e[0].astype(F32), e[1].astype(F32)
            sg = _sig(gv)
            return a[0] * uv * sg * (1.0 + gv * (1.0 - sg)), a[0] * gv * sg
        dg_act, dup = mm_fused("ffn_down_bwd", [ddn], [(0, full["w_down"][l], True)],
                               [('tile', sv["g_act"]), ('tile', sv["up"])], swiglu_bwd_epi,
                               [BF16, BF16], tm_mm, tn_f, S)
        gfull["w_down"][l] = mm_tn("ffn_down_dw", sv["act"], ddn)
        gfull["w_gu"][l] = jnp.concatenate([mm_tn("ffn_gate_dw", sv["h2"], dg_act),
                                            mm_tn("ffn_up_dw", sv["h2"], dup)], axis=1)
        dh2 = mm_fused("ffn_up_bwd", [dg_act, dup], [(0, w_gu_l, True, 0), (1, w_gu_l, True, 1)], [],
                       lambda a, e: (a[0] + a[1],), [F32], tm_mm, tn_d, S)[0]
        dx2, dsh2, dsc2, dg_ffn, ddo, dgt1 = modnorm_bwd(
            "modnorm_ffn_bwd", dh2, sv["x2"], dxc, row(g_ffn[l]), sc2, gate=(sv["o_mix"], gt1))

        def merge_bwd_epi(a, e):
            sa, sb = _sig(e[0]), _sig(e[1])
            dy = a[0]
            return dy * sa, dy * sb, dy * e[2] * sa * (1.0 - sa), dy * e[3] * sb * (1.0 - sb)
        dya, dyc, dga, dgb = mm_fused(
            "mix_out_bwd", [ddo], [(0, full["w_out"][l], True)],
            [('tile', sv["ga"]), ('tile', sv["gb"]), ('tile', sv["ya"]), ('tile', sv["yc"])],
            merge_bwd_epi, [BF16] * 4, tm_mm, tn_d, S)
        gfull["w_out"][l] = mm_tn("mix_out_dw", sv["y"], ddo)
        do_all = mm_fused("attn_out_bwd", [dya], [(0, W["o"], True)], [], lambda a, e: a, [BF16],
                          tm_mm, tn_s, S)[0]
        d_wo = mm_tn("attn_out_dw", sv["o_all"], dya)
        ds_act = mm_fused("conv_out_bwd", [dyc], [(0, full["w_pw2"][l], True)], [], lambda a, e: a,
                          [F32], tm_mm, C, S)[0]
        gfull["w_pw2"][l] = mm_tn("conv_out_dw", sv["s_act"], dyc)

        def ln_silu_bwd(dsv, uv, gv, bv):
            xhat, rstd = _layer_norm_parts(uv)
            ln = xhat * gv + bv
            sg = _sig(ln)
            dln = dsv * sg * (1.0 + ln * (1.0 - sg))
            dxhat = dln * gv
            du_ = rstd * (dxhat - jnp.mean(dxhat, axis=-1, keepdims=True)
                          - xhat * jnp.mean(dxhat * xhat, axis=-1, keepdims=True))
            return du_, dln * xhat, dln
        du, dg_cn, db_cn = rowwise(
            "conv_norm_bwd", [('tile', ds_act), ('tile', sv["u"]), ('vec', row(g_cn[l])),
                              ('vec', row(b_cn[l]))],
            [('tile', C, F32), ('acc', C), ('acc', C)], ln_silu_bwd, tm, S)
        dglu_a, dglu_b, dw_acc, db_dw = conv_bwd(du, sv["cu"], sv["glu_a"], sv["glu_b"],
                                                 w_dw_pad[l], tm, S)
        g_wdw[l] = group_sum("conv_dw_rows", dw_acc, SUB)[:CONV_W]
        if l == 0:
            for t, p in zip(GRADS_BEFORE_ATTN, partial_sums(0, GRADS_BEFORE_ATTN)):
                parts[0][t] = p
        riders = [(l + 1, t) for t in everything] if l + 1 < n_l else []
        riders += [(0, t) for t in GRADS_BEFORE_ATTN] if l == 0 else []
        dk_all, dv_all, gq_a, gq_b, *got = attn_bwd(
            sv["q_all"], sv["k_all"], sv["v_all"], sv["o_all"], do_all, sv["lse"], cos_q, sin_q,
            n_e, S, n_h, blk, scale,
            job=chip_exchange_job([parts[ll][t] for ll, t in riders]) if riders else None)
        for (ll, t), b in zip(riders, got):
            received[ll][t] = b
        dqn = mm_fused("q_up_bwd", [gq_a, gq_b], [(0, W["q_a"], True), (1, W["q_b"], True)], [],
                       lambda a, e: (a[0] + a[1],), [F32], tm_mm, QL, S)[0]
        d_wqa = mm_tn("q_up_dw_a", sv["qn"], gq_a)
        d_wqb = mm_tn("q_up_dw_b", sv["qn"], gq_b)
        dq_lat, dg_q = rms_bwd("q_norm_bwd", dqn, sv["q_lat"], row(g_q[l]), QL)

        def k_split(dkv, ckv, skv):
            tot = dkv[:, :SLOT]
            for h in range(1, n_h):
                tot = tot + dkv[:, h * SLOT:(h + 1) * SLOT]
            return dkv, tot * ckv, tot * skv
        dk_b, dkr_a, dkr_b = rowwise("k_rope_bwd", [('tile', dk_all), ('tile', cos_k), ('tile', sin_q)],
                                     [('tile', n_h * SLOT, BF16), ('tile', SLOT, BF16),
                                      ('tile', SLOT, BF16)], k_split, tm, S)
        dkvn = mm_fused("kv_up_bwd", [dk_b, dv_all], [(0, W["kn"], True), (1, W["v"], True)], [],
                        lambda a, e: (a[0] + a[1],), [F32], tm_mm, KVL, S)[0]
        d_wkn = mm_tn("kv_up_dw_k", sv["kvn"], dk_b)
        d_wv = mm_tn("kv_up_dw_v", sv["kvn"], dv_all)
        dkv_lat, dg_kv = rms_bwd("kv_norm_bwd", dkvn, sv["kv_lat"], row(g_kv[l]), KVL)
        segs = [("ga", dga), ("gb", dgb), ("glu_a", dglu_a), ("glu_b", dglu_b), ("ql", dq_lat),
                ("kvl", dkv_lat), ("kr_a", dkr_a), ("kr_b", dkr_b)]
        dh1 = mm_fused("proj_bwd", [g for _, g in segs],
                       [(k, W[nm], True) for k, (nm, _) in enumerate(segs)],
                       [], lambda a, e: (functools.reduce(lambda p, q: p + q, a),), [F32],
                       tm_mm, tn_d, S)[0]
        d_lay = {nm: mm_tn("proj_dw_" + nm, sv["h1"], g) for nm, g in segs}
        d_lay.update(q_a=d_wqa, q_b=d_wqb, kn=d_wkn, v=d_wv, o=d_wo)
        (gfull["w_in"][l], gfull["w_uq"][l], gfull["w_ukv"][l],
         gfull["w_o_attn"][l]) = lay_T({k: d_lay[k] for k in lay[l]})
        below = (saved[l - 1]["dn"], mods[l - 1, :, N_MOD - 1]) if l > 0 else None
        dxc, dsh1, dsc1, dg_mix, *ffn_gate = modnorm_bwd("modnorm_mix_bwd", dh1, sv["x"], dx2,
                                                         row(g_mix[l]), sc1, gate=below)
        dmod_acc[l] = [dsh1, dsc1, dgt1, dsh2, dsc2, dgt2]
        for k, a in (("g_mix", dg_mix), ("g_q", dg_q), ("g_kv", dg_kv), ("b_dw", db_dw),
                     ("g_cn", dg_cn), ("b_cn", db_cn), ("g_ffn", dg_ffn)):
            small_acc[k][l] = a
        rest = [t for t in everything if parts[l][t] is None]
        for t, p in zip(rest, partial_sums(l, rest)):
            parts[l][t] = p

    rest = [t for t in everything if received[0][t] is None]
    for t, b in zip(rest, run_job("grad_chip_exchange", chip_exchange_job([parts[0][t] for t in rest]))):
        received[0][t] = b
    grad_x = dxc.reshape(n_e, S, D)

    dmod_rows = jnp.concatenate([a for l in range(n_l) for a in dmod_acc[l]], axis=0)
    dmod_own = group_sum("dmod_rows", dmod_rows, SUB).reshape(n_l, N_MOD, n_e, D)
    dmod_own = jnp.transpose(dmod_own, (0, 2, 1, 3)).reshape(n_l * n_e, N_MOD * D)
    dmod_all = allgather8("gather_dmod", dmod_own).reshape(N_DEV, n_l, n_e, N_MOD * D)
    dmod_all = jnp.transpose(dmod_all, (1, 0, 2, 3)).reshape(n_l, B, N_MOD * D)
    dmod_mine = lax.dynamic_slice_in_dim(dmod_all, chip * ns_ada, ns_ada, axis=2)
    grad_w_ada = ada_bwd_w(jnp.transpose(c_act), dmod_mine)
    grad_b_ada = group_sum("grad_b_ada", dmod_all.reshape(n_l * B, N_MOD * D), B)

    mine_half = [shard_total("grad_shard_total_" + k, p, b, s_idx)
                 for l in range(n_l) for k, p, b in zip(GATHERED, parts[l], received[l])]
    both = []
    for half, other in zip(mine_half, sibling_share(mine_half)):
        kh, ns = half.shape
        both.append(lax.dynamic_update_slice(
            jnp.broadcast_to(other[None], (2, kh, ns)), half[None],
            (mc.astype(jnp.int32), zero, zero)).reshape(2 * kh, ns))
    n_g = len(GATHERED)
    red = {k: jnp.stack([both[l * n_g + t] for l in range(n_l)]) for t, k in enumerate(GATHERED)}

    wdw_full_g = jnp.stack(g_wdw)
    pieces = [loss_acc, dg_final] + [small_acc[k][l] for k in small_acc for l in range(n_l)]
    widths = [p.shape[1] for p in pieces]
    n_acc = sum(widths)
    wdw_blk = _pad_rows8(wdw_full_g.reshape(-1))
    wdw_w = wdw_blk.shape[1]
    gathered_small = allgather8("gather_small_grads", jnp.concatenate(pieces + [wdw_blk], axis=1))
    acc_sum = group_sum("small_total", gathered_small[:, :, :n_acc].reshape(N_DEV * SUB, n_acc),
                        N_DEV * SUB)
    wdw_sum = group_sum("wdw_total", gathered_small[:, :, n_acc:].reshape(N_DEV, SUB * wdw_w), N_DEV)
    offs = np.cumsum([0] + widths)
    take = lambda i: acc_sum[:, offs[i]:offs[i + 1]]
    loss = lane_total("loss_total", take(0))[0, 0]
    g_small = {"g_final": take(1).reshape(-1)}
    i = 2
    for k in small_acc:
        g_small[k] = jnp.concatenate([take(i + l) for l in range(n_l)], axis=0)
        i += n_l
    wdw_total = wdw_sum.reshape(-1)[:n_wdw * N_CHIPS].reshape(wdw_full_g.shape)
    grad_w_dw = lax.dynamic_slice_in_dim(wdw_total, chip * w_dw.shape[2], w_dw.shape[2], axis=2)

    grads = dict(w_ada=grad_w_ada, b_ada=grad_b_ada, w_dw=grad_w_dw, **g_small, **red)

    deltas, new_m, new_v = {}, {}, {}
    for k in order:
        shp = weights[k].shape
        two = (1, shp[0]) if len(shp) == 1 else (int(np.prod(shp[:-1])), shp[-1])
        d, nm, nv = adamw("adamw_" + k, weights[k].reshape(two), grads[k].reshape(two),
                          mom[k].reshape(two), var[k].reshape(two))
        deltas[k], new_m[k], new_v[k] = d.reshape(shp), nm.reshape(shp), nv.reshape(shp)
        grads[k] = grads[k].reshape(shp)

    return (loss, grad_x, *[grads[k] for k in order], *[deltas[k] for k in order],
            *[new_m[k] for k in order], *[new_v[k] for k in order])
```
